```python
import math
import jax, jax.numpy as jnp
from jax import lax
import numpy as np

D_MODEL = 4096
BATCH = 4
SEQ = 4096
DEPTH = 1

HEAD_DIM = 128
N_DIFF_HEADS = D_MODEL // (4 * HEAD_DIM)
N_FOX_HEADS = D_MODEL // (2 * HEAD_DIM)
DIFF_WIDTH = N_DIFF_HEADS * 2 * HEAD_DIM
FOX_WIDTH = N_FOX_HEADS * HEAD_DIM
MIX_WIDTH = DIFF_WIDTH + FOX_WIDTH
IN_COLS = 3 * DIFF_WIDTH + 3 * FOX_WIDTH + N_FOX_HEADS
D_FF = ((8 * D_MODEL + 3 * 256 - 1) // (3 * 256)) * 256
NUM_BUCKETS = 32
MAX_DISTANCE = 128
Q_BLOCK = 128
EPS = 1e-6
NEG_INF = -1e30

kernel_name = "hybrid_diffattn_fox_parallel_heads"


def rmsnorm(x, g):
    xf = x.astype(jnp.float32)
    y = xf * lax.rsqrt(jnp.mean(xf * xf, axis=-1, keepdims=True) + EPS)
    return (y * g.astype(jnp.float32)).astype(x.dtype)


def lambda_init(layer_idx):
    return 0.8 - 0.6 * math.exp(-0.3 * layer_idx)


def t5_causal_bucket(n):
    max_exact = NUM_BUCKETS // 2
    nf = jnp.maximum(n, 1).astype(jnp.float32)
    large = max_exact + (jnp.log(nf / max_exact) / math.log(MAX_DISTANCE / max_exact)
                         * (NUM_BUCKETS - max_exact)).astype(jnp.int32)
    large = jnp.minimum(large, NUM_BUCKETS - 1)
    return jnp.where(n < max_exact, n, large)


def hybrid_mixer(h, w_in, b_f, lam, rel_bias_table, diff_subln_g, lam_init, w_o):
    B, S, _ = h.shape
    proj = h @ w_in
    o = 0
    dq = proj[..., o:o + DIFF_WIDTH]; o += DIFF_WIDTH
    dk = proj[..., o:o + DIFF_WIDTH]; o += DIFF_WIDTH
    dv = proj[..., o:o + DIFF_WIDTH]; o += DIFF_WIDTH
    fq = proj[..., o:o + FOX_WIDTH]; o += FOX_WIDTH
    fk = proj[..., o:o + FOX_WIDTH]; o += FOX_WIDTH
    fv = proj[..., o:o + FOX_WIDTH]; o += FOX_WIDTH
    f_logit = proj[..., o:o + N_FOX_HEADS] + b_f

    dq = dq.reshape(B, S, N_DIFF_HEADS, 2, HEAD_DIM).transpose(0, 2, 3, 1, 4)
    dk = dk.reshape(B, S, N_DIFF_HEADS, 2, HEAD_DIM).transpose(0, 2, 3, 1, 4)
    dv = dv.reshape(B, S, N_DIFF_HEADS, 2 * HEAD_DIM).transpose(0, 2, 1, 3)
    fq = fq.reshape(B, S, N_FOX_HEADS, HEAD_DIM).transpose(0, 2, 1, 3)
    fk = fk.reshape(B, S, N_FOX_HEADS, HEAD_DIM).transpose(0, 2, 1, 3)
    fv = fv.reshape(B, S, N_FOX_HEADS, HEAD_DIM).transpose(0, 2, 1, 3)
    log_f = jax.nn.log_sigmoid(f_logit.astype(jnp.float32))
    cum = jnp.cumsum(log_f, axis=1).transpose(0, 2, 1)

    scale = HEAD_DIM ** -0.5
    k_pos = jnp.arange(S, dtype=jnp.int32)
    n_blocks = S // Q_BLOCK

    def block(i):
        start = i * Q_BLOCK
        q_pos = start + jnp.arange(Q_BLOCK, dtype=jnp.int32)
        dist = q_pos[:, None] - k_pos[None, :]
        causal = dist >= 0
        bias = rel_bias_table[t5_causal_bucket(jnp.maximum(dist, 0))]
        bias = bias.astype(jnp.float32).transpose(2, 0, 1)

        q_d = lax.dynamic_slice_in_dim(dq, start, Q_BLOCK, axis=3)
        s_d = jnp.einsum('bhmqd,bhmkd->bhmqk', q_d, dk).astype(jnp.float32) * scale
        s_d = s_d + bias[None, :, None]
        s_d = jnp.where(causal, s_d, NEG_INF)
        p_d = jax.nn.softmax(s_d, axis=-1)
        a_d = p_d[:, :, 0] - lam * p_d[:, :, 1]
        o_d = jnp.einsum('bhqk,bhkd->bhqd', a_d.astype(dv.dtype), dv)

        q_f = lax.dynamic_slice_in_dim(fq, start, Q_BLOCK, axis=2)
        c_q = lax.dynamic_slice_in_dim(cum, start, Q_BLOCK, axis=2)
        s_f = jnp.einsum('bhqd,bhkd->bhqk', q_f, fk).astype(jnp.float32) * scale
        s_f = s_f + (c_q[..., :, None] - cum[..., None, :])
        s_f = jnp.where(causal, s_f, NEG_INF)
        p_f = jax.nn.softmax(s_f, axis=-1)
        o_f = jnp.einsum('bhqk,bhkd->bhqd', p_f.astype(fv.dtype), fv)
        return o_d, o_f

    o_d, o_f = lax.map(block, jnp.arange(n_blocks))
    o_d = o_d.transpose(1, 0, 3, 2, 4).reshape(B, S, N_DIFF_HEADS, 2 * HEAD_DIM)
    o_f = o_f.transpose(1, 0, 3, 2, 4).reshape(B, S, N_FOX_HEADS, HEAD_DIM)
    o_d = rmsnorm(o_d, diff_subln_g) * (1.0 - lam_init)
    mixed = jnp.concatenate([o_d.reshape(B, S, DIFF_WIDTH),
                             o_f.reshape(B, S, FOX_WIDTH)], axis=-1)
    return mixed @ w_o


def swiglu(h, w_gate, w_up, w_down):
    return (jax.nn.silu(h @ w_gate) * (h @ w_up)) @ w_down


def setup_inputs(seed: int = 0) -> dict:
    key = jax.random.key(seed)
    ks = jax.random.split(key, 20)
    f32 = jnp.float32
    nrm = lambda k, shape, s: jax.random.normal(k, shape, f32) * s
    return {
        "x": nrm(ks[0], (BATCH, SEQ, D_MODEL), 1.0),
        "attn_norm_g": 1.0 + nrm(ks[1], (DEPTH, D_MODEL), 0.02),
        "w_in": nrm(ks[2], (DEPTH, D_MODEL, IN_COLS), D_MODEL ** -0.5),
        "b_f": 3.0 + nrm(ks[3], (DEPTH, N_FOX_HEADS), 0.5),
        "lambda_q1": nrm(ks[4], (DEPTH, HEAD_DIM), 0.1),
        "lambda_k1": nrm(ks[5], (DEPTH, HEAD_DIM), 0.1),
        "lambda_q2": nrm(ks[6], (DEPTH, HEAD_DIM), 0.1),
        "lambda_k2": nrm(ks[7], (DEPTH, HEAD_DIM), 0.1),
        "rel_bias_table": nrm(ks[8], (NUM_BUCKETS, N_DIFF_HEADS), 0.5),
        "diff_subln_g": 1.0 + nrm(ks[9], (DEPTH, 2 * HEAD_DIM), 0.02),
        "w_o": nrm(ks[10], (DEPTH, MIX_WIDTH, D_MODEL), MIX_WIDTH ** -0.5),
        "ffn_norm_g": 1.0 + nrm(ks[11], (DEPTH, D_MODEL), 0.02),
        "w_gate": nrm(ks[12], (DEPTH, D_MODEL, D_FF), D_MODEL ** -0.5),
        "w_up": nrm(ks[13], (DEPTH, D_MODEL, D_FF), D_MODEL ** -0.5),
        "w_down": nrm(ks[14], (DEPTH, D_FF, D_MODEL), D_FF ** -0.5),
        "final_norm_g": 1.0 + nrm(ks[15], (D_MODEL,), 0.02),
    }


def reference(x, attn_norm_g, w_in, b_f, lambda_q1, lambda_k1, lambda_q2, lambda_k2,
              rel_bias_table, diff_subln_g, w_o, ffn_norm_g, w_gate, w_up, w_down,
              final_norm_g):
    for l in range(DEPTH):
        lam_init = lambda_init(l)
        lam = (jnp.exp(jnp.sum(lambda_q1[l].astype(jnp.float32) * lambda_k1[l].astype(jnp.float32)))
               - jnp.exp(jnp.sum(lambda_q2[l].astype(jnp.float32) * lambda_k2[l].astype(jnp.float32)))
               + lam_init)
        h = rmsnorm(x, attn_norm_g[l])
        x = x + hybrid_mixer(h, w_in[l], b_f[l], lam, rel_bias_table, diff_subln_g[l],
                             lam_init, w_o[l])
        h = rmsnorm(x, ffn_norm_g[l])
        x = x + swiglu(h, w_gate[l], w_up[l], w_down[l])
    return rmsnorm(x, final_norm_g)
```

```python
import functools
import math

import jax
import jax.numpy as jnp
from jax import lax
from jax.experimental import pallas as pl
from jax.experimental.pallas import tpu as pltpu

D_MODEL = 4096
HEAD_DIM = 128
N_DIFF_HEADS = D_MODEL // (4 * HEAD_DIM)
N_FOX_HEADS = D_MODEL // (2 * HEAD_DIM)
DIFF_WIDTH = N_DIFF_HEADS * 2 * HEAD_DIM
FOX_WIDTH = N_FOX_HEADS * HEAD_DIM
MIX_WIDTH = DIFF_WIDTH + FOX_WIDTH
QKV_COLS = 3 * DIFF_WIDTH + 3 * FOX_WIDTH
D_FF = ((8 * D_MODEL + 3 * 256 - 1) // (3 * 256)) * 256
NUM_BUCKETS = 32
MAX_DISTANCE = 128
EPS = 1e-6
NEG_INF = -1e30
LAM_INIT = 0.8 - 0.6 * math.exp(-0.3 * 0)
SCALE = HEAD_DIM ** -0.5

LANES = 128
FF_TILE = 512
D_FF_PAD = ((D_FF + FF_TILE - 1) // FF_TILE) * FF_TILE
VMEM_LIMIT = 56 * 1024 * 1024

ATTN_BLOCK = 512
PROJ_COL_CHUNK = 1024
PROJ_ROW_CHUNK = 128


def _params(sem, vmem=VMEM_LIMIT):
    return pltpu.CompilerParams(dimension_semantics=sem, vmem_limit_bytes=vmem)


def _norm_kernel(x_ref, g_ref, wf_ref, h_ref, fl_ref):
    x = x_ref[...]
    y = x * lax.rsqrt(jnp.mean(x * x, axis=-1, keepdims=True) + EPS)
    h = (y * g_ref[...]).astype(jnp.bfloat16)
    h_ref[...] = h
    fl_ref[...] = lax.dot_general(wf_ref[...], h, (((1,), (1,)), ((), ())),
                                  preferred_element_type=jnp.float32)


def _norm_and_gate_logits(x2d, g, wf_t, bm=512):
    m, d = x2d.shape
    nh = wf_t.shape[0]
    return pl.pallas_call(
        _norm_kernel,
        grid=(m // bm,),
        in_specs=[pl.BlockSpec((bm, d), lambda i: (i, 0)),
                  pl.BlockSpec((1, d), lambda i: (0, 0)),
                  pl.BlockSpec((nh, d), lambda i: (0, 0))],
        out_specs=[pl.BlockSpec((bm, d), lambda i: (i, 0)),
                   pl.BlockSpec((nh, bm), lambda i: (0, i))],
        out_shape=[jax.ShapeDtypeStruct((m, d), jnp.bfloat16),
                   jax.ShapeDtypeStruct((nh, m), jnp.float32)],
        compiler_params=_params(("arbitrary",)),
        name="attn_norm",
    )(x2d, g, wf_t)


def _matmul_kernel(a_ref, w_ref, o_ref):
    o_ref[...] = jnp.dot(a_ref[...], w_ref[...],
                         preferred_element_type=jnp.float32).astype(o_ref.dtype)


def _matmul(a, w, n_out, bm=1024, bn=1024, name="matmul"):
    m, k = a.shape
    return pl.pallas_call(
        _matmul_kernel,
        grid=(m // bm, n_out // bn),
        in_specs=[pl.BlockSpec((bm, k), lambda i, j: (i, 0)),
                  pl.BlockSpec((k, bn), lambda i, j: (0, j))],
        out_specs=pl.BlockSpec((bm, bn), lambda i, j: (i, j)),
        out_shape=jax.ShapeDtypeStruct((m, n_out), jnp.bfloat16),
        compiler_params=_params(("arbitrary", "arbitrary")),
        name=name,
    )(a, w)


def _cumsum_kernel(fl_ref, bf_ref, c_ref):
    z = fl_ref[...] + bf_ref[...]
    x = jnp.minimum(z, 0.0) - jnp.log(1.0 + jnp.exp(-jnp.abs(z)))
    n = x.shape[-1]
    pos = lax.broadcasted_iota(jnp.int32, x.shape, 1)
    shift = 1
    while shift < n:
        x = x + jnp.where(pos >= shift, pltpu.roll(x, shift, 1), 0.0)
        shift *= 2
    c_ref[...] = x


def _cum_log_forget(fl_t, b_f, seq):
    nh, m = fl_t.shape
    return pl.pallas_call(
        _cumsum_kernel,
        grid=(m // seq,),
        in_specs=[pl.BlockSpec((nh, seq), lambda b: (0, b)),
                  pl.BlockSpec((nh, 1), lambda b: (0, 0))],
        out_specs=pl.BlockSpec((nh, seq), lambda b: (0, b)),
        out_shape=jax.ShapeDtypeStruct((nh, m), jnp.float32),
        compiler_params=_params(("arbitrary",)),
        name="cum_log_forget",
    )(fl_t, b_f)


def _qk(q, k):
    return lax.dot_general(q, k, (((1,), (1,)), ((), ())), preferred_element_type=jnp.float32)


def _online_update(s, v, m_ref, l_ref, acc_ref):
    m_old = m_ref[...]
    m_new = jnp.maximum(m_old, jnp.max(s, axis=-1, keepdims=True))
    alpha = jnp.exp(m_old - m_new)
    p = jnp.exp(s - m_new)
    l_ref[...] = alpha * l_ref[...] + jnp.sum(p, axis=-1, keepdims=True)
    acc_ref[...] = alpha * acc_ref[...] + jnp.dot(p.astype(v.dtype), v,
                                                  preferred_element_type=jnp.float32)
    m_ref[...] = m_new


def _t5_bucket(n):
    max_exact = NUM_BUCKETS // 2
    nf = jnp.maximum(n, 1).astype(jnp.float32)
    large = max_exact + (jnp.log(nf / max_exact) / math.log(MAX_DISTANCE / max_exact)
                         * (NUM_BUCKETS - max_exact)).astype(jnp.int32)
    large = jnp.minimum(large, NUM_BUCKETS - 1)
    return jnp.where(n < max_exact, n, large)


def _diff_attn_kernel(tab_ref, lq1_ref, lk1_ref, lq2_ref, lk2_ref, q_ref, k_ref, v_ref, g_ref,
                      o_ref, bias_near_ref, bias_diag_ref, m_ref, l_ref, acc_ref):
    h = pl.program_id(0)
    b = pl.program_id(1)
    i = pl.program_id(2)
    blk = ATTN_BLOCK
    row = lax.broadcasted_iota(jnp.int32, (blk, blk), 0)
    col = lax.broadcasted_iota(jnp.int32, (blk, blk), 1)

    @pl.when((b == 0) & (i == 0))
    def _build_bias():
        for ref, offset in ((bias_diag_ref, 0), (bias_near_ref, blk)):
            bucket = _t5_bucket(jnp.maximum(row - col + offset, 0))
            bias = jnp.zeros((blk, blk), jnp.float32)
            for t in range(NUM_BUCKETS):
                bias = jnp.where(bucket == t, tab_ref[t, h], bias)
            ref[...] = bias

    m_ref[...] = jnp.full(m_ref.shape, NEG_INF, jnp.float32)
    l_ref[...] = jnp.zeros(l_ref.shape, jnp.float32)
    acc_ref[...] = jnp.zeros(acc_ref.shape, jnp.float32)

    def step(j, bias, mask):
        start = pl.multiple_of(j * blk, blk)
        v = v_ref[pl.ds(start, blk), :]
        for mp in range(2):
            q = q_ref[:, mp * HEAD_DIM:(mp + 1) * HEAD_DIM]
            k = k_ref[pl.ds(start, blk), mp * HEAD_DIM:(mp + 1) * HEAD_DIM]
            s = _qk(q, k) * SCALE + bias
            if mask is not None:
                s = jnp.where(mask, s, NEG_INF)
            _online_update(s, v, m_ref.at[mp], l_ref.at[mp], acc_ref.at[mp])

    far_bias = tab_ref[NUM_BUCKETS - 1, h]

    def far_body(j, carry):
        step(j, far_bias, None)
        return carry

    lax.fori_loop(0, jnp.maximum(i - 1, 0), far_body, 0)

    @pl.when(i > 0)
    def _near():
        step(i - 1, bias_near_ref[...], None)

    step(i, bias_diag_ref[...], row >= col)

    lam = (jnp.exp(jnp.sum(lq1_ref[...] * lk1_ref[...], axis=-1, keepdims=True))
           - jnp.exp(jnp.sum(lq2_ref[...] * lk2_ref[...], axis=-1, keepdims=True))
           + LAM_INIT)
    o = acc_ref[0] / l_ref[0] - lam * (acc_ref[1] / l_ref[1])
    y = o * lax.rsqrt(jnp.mean(o * o, axis=-1, keepdims=True) + EPS)
    o_ref[...] = ((y * g_ref[...]) * (1.0 - LAM_INIT)).astype(o_ref.dtype)


def _diff_attention(qkv, tab, lq1, lk1, lq2, lk2, g):
    bsz, seq, _ = qkv.shape
    blk = ATTN_BLOCK
    w = 2 * HEAD_DIM
    k_off = DIFF_WIDTH // w
    v_off = 2 * DIFF_WIDTH // w
    vec = pl.BlockSpec((1, HEAD_DIM), lambda h, b, i: (0, 0))
    return pl.pallas_call(
        _diff_attn_kernel,
        grid=(N_DIFF_HEADS, bsz, seq // blk),
        in_specs=[pl.BlockSpec(memory_space=pltpu.SMEM),
                  vec, vec, vec, vec,
                  pl.BlockSpec((None, blk, w), lambda h, b, i: (b, i, h)),
                  pl.BlockSpec((None, seq, w), lambda h, b, i: (b, 0, k_off + h)),
                  pl.BlockSpec((None, seq, w), lambda h, b, i: (b, 0, v_off + h)),
                  pl.BlockSpec((1, w), lambda h, b, i: (0, 0))],
        out_specs=pl.BlockSpec((None, blk, w), lambda h, b, i: (b, i, h)),
        out_shape=jax.ShapeDtypeStruct((bsz, seq, MIX_WIDTH), jnp.bfloat16),
        scratch_shapes=[pltpu.VMEM((blk, blk), jnp.float32),
                        pltpu.VMEM((blk, blk), jnp.float32),
                        pltpu.VMEM((2, blk, 1), jnp.float32),
                        pltpu.VMEM((2, blk, 1), jnp.float32),
                        pltpu.VMEM((2, blk, w), jnp.float32)],
        compiler_params=_params(("arbitrary", "arbitrary", "arbitrary")),
        name="diff_attention",
    )(tab, lq1, lk1, lq2, lk2, qkv, qkv, qkv, g)


def _fox_attn_kernel(q_ref, k_ref, v_ref, cq_ref, ck_ref, mixed_hbm_ref, o_ref, m_ref, l_ref, acc_ref):
    del mixed_hbm_ref
    i = pl.program_id(2)
    blk = ATTN_BLOCK
    m_ref[...] = jnp.full(m_ref.shape, NEG_INF, jnp.float32)
    l_ref[...] = jnp.zeros(l_ref.shape, jnp.float32)
    acc_ref[...] = jnp.zeros(acc_ref.shape, jnp.float32)
    q = q_ref[...]
    cq = cq_ref[...]

    def step(j, mask):
        start = pl.multiple_of(j * blk, blk)
        k = k_ref[pl.ds(start, blk), :]
        v = v_ref[pl.ds(start, blk), :]
        ck = ck_ref[:, pl.ds(start, blk)]
        s = _qk(q, k) * SCALE + (cq - ck)
        if mask is not None:
            s = jnp.where(mask, s, NEG_INF)
        _online_update(s, v, m_ref, l_ref, acc_ref)

    def body(j, carry):
        step(j, None)
        return carry

    lax.fori_loop(0, i, body, 0)
    row = lax.broadcasted_iota(jnp.int32, (blk, blk), 0)
    col = lax.broadcasted_iota(jnp.int32, (blk, blk), 1)
    step(i, row >= col)
    o_ref[...] = (acc_ref[...] / l_ref[...]).astype(o_ref.dtype)


def _fox_attention(qkv, cum_col, cum_row, mixed):
    bsz, seq, _ = qkv.shape
    blk = ATTN_BLOCK
    q_off = 3 * DIFF_WIDTH // HEAD_DIM
    k_off = q_off + N_FOX_HEADS
    v_off = k_off + N_FOX_HEADS
    o_off = DIFF_WIDTH // HEAD_DIM
    return pl.pallas_call(
        _fox_attn_kernel,
        grid=(N_FOX_HEADS, bsz, seq // blk),
        in_specs=[pl.BlockSpec((None, blk, HEAD_DIM), lambda h, b, i: (b, i, q_off + h)),
                  pl.BlockSpec((None, seq, HEAD_DIM), lambda h, b, i: (b, 0, k_off + h)),
                  pl.BlockSpec((None, seq, HEAD_DIM), lambda h, b, i: (b, 0, v_off + h)),
                  pl.BlockSpec((None, None, blk, 1), lambda h, b, i: (h, b, i, 0)),
                  pl.BlockSpec((None, None, 1, seq), lambda h, b, i: (h, b, 0, 0)),
                  pl.BlockSpec(memory_space=pl.ANY)],
        out_specs=pl.BlockSpec((None, blk, HEAD_DIM), lambda h, b, i: (b, i, o_off + h)),
        out_shape=jax.ShapeDtypeStruct(mixed.shape, mixed.dtype),
        scratch_shapes=[pltpu.VMEM((blk, 1), jnp.float32),
                        pltpu.VMEM((blk, 1), jnp.float32),
                        pltpu.VMEM((blk, HEAD_DIM), jnp.float32)],
        input_output_aliases={5: 0},
        compiler_params=_params(("arbitrary", "arbitrary", "arbitrary")),
        name="fox_attention",
    )(qkv, qkv, qkv, cum_col, cum_row, mixed)


def _proj_residual_kernel(a_ref, w_ref, res_ref, g_ref, *out_refs, emit_sum, n_k):
    k = pl.program_id(1)
    acc_ref = out_refs[0]

    @pl.when(k == 0)
    def _init():
        acc_ref[...] = res_ref[...]

    a = a_ref[...]
    for c in range(0, acc_ref.shape[1], PROJ_COL_CHUNK):
        cols = slice(c, c + PROJ_COL_CHUNK)
        acc_ref[:, cols] += jnp.dot(a, w_ref[:, cols], preferred_element_type=jnp.float32)

    @pl.when(k == n_k - 1)
    def _finish():
        for r in range(0, acc_ref.shape[0], PROJ_ROW_CHUNK):
            rows = slice(r, r + PROJ_ROW_CHUNK)
            x = acc_ref[rows, :]
            y = x * lax.rsqrt(jnp.mean(x * x, axis=-1, keepdims=True) + EPS) * g_ref[...]
            if emit_sum:
                out_refs[1][rows, :] = y.astype(out_refs[1].dtype)
            else:
                acc_ref[rows, :] = y


def _proj_residual_norm(a, w, res, g, *, emit_sum, bm, bk, name):
    m, kdim = a.shape
    d = w.shape[1]
    n_k = kdim // bk
    row_block = pl.BlockSpec((bm, d), lambda i, k: (i, 0))
    if emit_sum:
        out_specs = [row_block, row_block]
        out_shape = [jax.ShapeDtypeStruct((m, d), jnp.float32),
                     jax.ShapeDtypeStruct((m, d), jnp.bfloat16)]
    else:
        out_specs = [row_block]
        out_shape = [jax.ShapeDtypeStruct((m, d), jnp.float32)]
    return pl.pallas_call(
        functools.partial(_proj_residual_kernel, emit_sum=emit_sum, n_k=n_k),
        grid=(m // bm, n_k),
        in_specs=[pl.BlockSpec((bm, bk), lambda i, k: (i, k)),
                  pl.BlockSpec((bk, d), lambda i, k: (k, 0)),
                  row_block,
                  pl.BlockSpec((1, d), lambda i, k: (0, 0))],
        out_specs=out_specs,
        out_shape=out_shape,
        compiler_params=_params(("arbitrary", "arbitrary")),
        name=name,
    )(a, w, res, g)


def _gate_up_kernel(h_ref, wg_ref, wu_ref, o_ref):
    h = h_ref[...]
    gate = jnp.dot(h, wg_ref[...], preferred_element_type=jnp.float32)
    up = jnp.dot(h, wu_ref[...], preferred_element_type=jnp.float32)
    o_ref[...] = (gate * jax.nn.sigmoid(gate) * up).astype(o_ref.dtype)


def _gate_up(h, wg, wu, bm=1024, bn=FF_TILE):
    m, k = h.shape
    n = wg.shape[1]
    return pl.pallas_call(
        _gate_up_kernel,
        grid=(m // bm, n // bn),
        in_specs=[pl.BlockSpec((bm, k), lambda i, j: (i, 0)),
                  pl.BlockSpec((k, bn), lambda i, j: (0, j)),
                  pl.BlockSpec((k, bn), lambda i, j: (0, j))],
        out_specs=pl.BlockSpec((bm, bn), lambda i, j: (i, j)),
        out_shape=jax.ShapeDtypeStruct((m, n), jnp.bfloat16),
        compiler_params=_params(("arbitrary", "arbitrary")),
        name="ffn_gate_up",
    )(h, wg, wu)


def kernel(x, attn_norm_g, w_in, b_f, lambda_q1, lambda_k1, lambda_q2, lambda_k2, rel_bias_table,
           diff_subln_g, w_o, ffn_norm_g, w_gate, w_up, w_down, final_norm_g):
    bsz, seq, d = x.shape
    m = bsz * seq
    bf16 = jnp.bfloat16
    x2d = x.reshape(m, d)
    ff_pad = D_FF_PAD - D_FF

    w_qkv = w_in[0, :, :QKV_COLS].astype(bf16)
    wf_t = w_in[0, :, QKV_COLS:].T.astype(bf16)
    w_o_b = w_o[0].astype(bf16)
    w_gate_b = jnp.pad(w_gate[0].astype(bf16), ((0, 0), (0, ff_pad)))
    w_up_b = jnp.pad(w_up[0].astype(bf16), ((0, 0), (0, ff_pad)))
    w_down_b = jnp.pad(w_down[0].astype(bf16), ((0, ff_pad), (0, 0)))

    h, fl_t = _norm_and_gate_logits(x2d, attn_norm_g[0].reshape(1, d), wf_t)
    qkv = _matmul(h, w_qkv, QKV_COLS, name="in_proj").reshape(bsz, seq, QKV_COLS)

    cum = _cum_log_forget(fl_t, b_f[0].reshape(N_FOX_HEADS, 1), seq)
    cum_row = cum.reshape(N_FOX_HEADS, bsz, 1, seq)
    cum_col = cum.reshape(N_FOX_HEADS, bsz, seq, 1)

    mixed = _diff_attention(qkv, rel_bias_table,
                            lambda_q1[0].reshape(1, HEAD_DIM), lambda_k1[0].reshape(1, HEAD_DIM),
                            lambda_q2[0].reshape(1, HEAD_DIM), lambda_k2[0].reshape(1, HEAD_DIM),
                            diff_subln_g[0].reshape(1, 2 * HEAD_DIM))
    mixed = _fox_attention(qkv, cum_col, cum_row, mixed)

    x1, h2 = _proj_residual_norm(mixed.reshape(m, MIX_WIDTH), w_o_b, x2d,
                                 ffn_norm_g[0].reshape(1, d),
                                 emit_sum=True, bm=512, bk=512, name="out_proj")
    act = _gate_up(h2, w_gate_b, w_up_b)
    (out,) = _proj_residual_norm(act, w_down_b, x1, final_norm_g.reshape(1, d),
                                 emit_sum=False, bm=512, bk=1024, name="ffn_down")
    return out.reshape(bsz, seq, d)
```

```python
import functools
import math

import jax
import jax.numpy as jnp
from jax import lax
from jax.experimental import pallas as pl
from jax.experimental.pallas import tpu as pltpu

D_MODEL = 4096
HEAD_DIM = 128
N_DIFF_HEADS = D_MODEL // (4 * HEAD_DIM)
N_FOX_HEADS = D_MODEL // (2 * HEAD_DIM)
DIFF_WIDTH = N_DIFF_HEADS * 2 * HEAD_DIM
FOX_WIDTH = N_FOX_HEADS * HEAD_DIM
MIX_WIDTH = DIFF_WIDTH + FOX_WIDTH
QKV_COLS = 3 * DIFF_WIDTH + 3 * FOX_WIDTH
D_FF = ((8 * D_MODEL + 3 * 256 - 1) // (3 * 256)) * 256
NUM_BUCKETS = 32
MAX_DISTANCE = 128
EPS = 1e-6
NEG_INF = -1e30
LAM_INIT = 0.8 - 0.6 * math.exp(-0.3 * 0)
SCALE = HEAD_DIM ** -0.5
LOG2E = math.log2(math.e)

LANES = 128
FF_TILE = 512
D_FF_PAD = ((D_FF + FF_TILE - 1) // FF_TILE) * FF_TILE
VMEM_LIMIT = 56 * 1024 * 1024

ATTN_BLOCK = 512
PROJ_COL_CHUNK = 1024
PROJ_ROW_CHUNK = 128


def _params(sem, vmem=VMEM_LIMIT):
    return pltpu.CompilerParams(dimension_semantics=sem, vmem_limit_bytes=vmem)


def _norm_kernel(x_ref, g_ref, wf_ref, h_ref, fl_ref):
    x = x_ref[...]
    y = x * lax.rsqrt(jnp.mean(x * x, axis=-1, keepdims=True) + EPS)
    h = (y * g_ref[...]).astype(jnp.bfloat16)
    h_ref[...] = h
    fl_ref[...] = lax.dot_general(wf_ref[...], h, (((1,), (1,)), ((), ())),
                                  preferred_element_type=jnp.float32)


def _norm_and_gate_logits(x2d, g, wf_t, bm=512):
    m, d = x2d.shape
    nh = wf_t.shape[0]
    return pl.pallas_call(
        _norm_kernel,
        grid=(m // bm,),
        in_specs=[pl.BlockSpec((bm, d), lambda i: (i, 0)),
                  pl.BlockSpec((1, d), lambda i: (0, 0)),
                  pl.BlockSpec((nh, d), lambda i: (0, 0))],
        out_specs=[pl.BlockSpec((bm, d), lambda i: (i, 0)),
                   pl.BlockSpec((nh, bm), lambda i: (0, i))],
        out_shape=[jax.ShapeDtypeStruct((m, d), jnp.bfloat16),
                   jax.ShapeDtypeStruct((nh, m), jnp.float32)],
        compiler_params=_params(("arbitrary",)),
        name="attn_norm",
    )(x2d, g, wf_t)


def _matmul_kernel(a_ref, w_ref, o_ref):
    o_ref[...] = jnp.dot(a_ref[...], w_ref[...],
                         preferred_element_type=jnp.float32).astype(o_ref.dtype)


def _matmul(a, w, n_out, bm=1024, bn=1024, name="matmul"):
    m, k = a.shape
    return pl.pallas_call(
        _matmul_kernel,
        grid=(m // bm, n_out // bn),
        in_specs=[pl.BlockSpec((bm, k), lambda i, j: (i, 0)),
                  pl.BlockSpec((k, bn), lambda i, j: (0, j))],
        out_specs=pl.BlockSpec((bm, bn), lambda i, j: (i, j)),
        out_shape=jax.ShapeDtypeStruct((m, n_out), jnp.bfloat16),
        compiler_params=_params(("arbitrary", "arbitrary")),
        name=name,
    )(a, w)


def _cumsum_kernel(fl_ref, bf_ref, c_ref):
    z = fl_ref[...] + bf_ref[...]
    x = jnp.minimum(z, 0.0) - jnp.log(1.0 + jnp.exp(-jnp.abs(z)))
    n = x.shape[-1]
    pos = lax.broadcasted_iota(jnp.int32, x.shape, 1)
    shift = 1
    while shift < n:
        x = x + jnp.where(pos >= shift, pltpu.roll(x, shift, 1), 0.0)
        shift *= 2
    c_ref[...] = x


def _cum_log_forget(fl_t, b_f, seq):
    nh, m = fl_t.shape
    return pl.pallas_call(
        _cumsum_kernel,
        grid=(m // seq,),
        in_specs=[pl.BlockSpec((nh, seq), lambda b: (0, b)),
                  pl.BlockSpec((nh, 1), lambda b: (0, 0))],
        out_specs=pl.BlockSpec((nh, seq), lambda b: (0, b)),
        out_shape=jax.ShapeDtypeStruct((nh, m), jnp.float32),
        compiler_params=_params(("arbitrary",)),
        name="cum_log_forget",
    )(fl_t, b_f)


def _scores_t(k, q):
    return lax.dot_general(k, q, (((1,), (1,)), ((), ())), preferred_element_type=jnp.float32)


def _online_update_t(u, shift, v_t, m_ref, l_ref, acc_ref):
    m_old = m_ref[...]
    m_new = jnp.maximum(m_old, jnp.max(u, axis=0, keepdims=True) + shift)
    alpha = jnp.exp2(m_old - m_new)
    p = jnp.exp2(u + (shift - m_new))
    l_ref[...] = alpha * l_ref[...] + jnp.sum(p, axis=0, keepdims=True)
    acc_ref[...] = alpha * acc_ref[...] + jnp.dot(v_t, p.astype(v_t.dtype),
                                                  preferred_element_type=jnp.float32)
    m_ref[...] = m_new


def _init_stats(m_ref, l_ref, acc_ref):
    m_ref[...] = jnp.full(m_ref.shape, NEG_INF, jnp.float32)
    l_ref[...] = jnp.zeros(l_ref.shape, jnp.float32)
    acc_ref[...] = jnp.zeros(acc_ref.shape, jnp.float32)


def _store_transposed(src_ref, dst_ref):
    for c in range(0, src_ref.shape[0], ATTN_BLOCK):
        dst_ref[:, c:c + ATTN_BLOCK] = (
            src_ref[c:c + ATTN_BLOCK, :].astype(jnp.float32).T.astype(dst_ref.dtype))


def _t5_bucket(n):
    max_exact = NUM_BUCKETS // 2
    nf = jnp.maximum(n, 1).astype(jnp.float32)
    large = max_exact + (jnp.log(nf / max_exact) / math.log(MAX_DISTANCE / max_exact)
                         * (NUM_BUCKETS - max_exact)).astype(jnp.int32)
    large = jnp.minimum(large, NUM_BUCKETS - 1)
    return jnp.where(n < max_exact, n, large)


def _diff_attn_kernel(tab_ref, lq1_ref, lk1_ref, lq2_ref, lk2_ref, q_ref, k_ref, v_ref, g_ref,
                      o_ref, bias_near_ref, bias_diag_ref, vt_ref, m_ref, l_ref, acc_ref):
    h = pl.program_id(0)
    b = pl.program_id(1)
    i = pl.program_id(2)
    blk = ATTN_BLOCK
    key = lax.broadcasted_iota(jnp.int32, (blk, blk), 0)
    qry = lax.broadcasted_iota(jnp.int32, (blk, blk), 1)
    far_bias = tab_ref[NUM_BUCKETS - 1, h]

    @pl.when((b == 0) & (i == 0))
    def _build_bias():
        for ref, offset in ((bias_diag_ref, 0), (bias_near_ref, blk)):
            bucket = _t5_bucket(jnp.maximum(qry - key + offset, 0))
            bias = jnp.zeros((blk, blk), jnp.float32)
            for t in range(NUM_BUCKETS):
                bias = jnp.where(bucket == t, tab_ref[t, h], bias)
            ref[...] = (bias - far_bias) * LOG2E

    @pl.when(i == 0)
    def _transpose_v():
        _store_transposed(v_ref, vt_ref)

    _init_stats(m_ref, l_ref, acc_ref)

    def step(j, bias, mask):
        start = pl.multiple_of(j * blk, blk)
        v_t = vt_ref[:, pl.ds(start, blk)]
        for mp in range(2):
            q = q_ref[:, mp * HEAD_DIM:(mp + 1) * HEAD_DIM]
            k = k_ref[pl.ds(start, blk), mp * HEAD_DIM:(mp + 1) * HEAD_DIM]
            u = _scores_t(k, q) * (SCALE * LOG2E)
            if bias is not None:
                u = u + bias
            if mask is not None:
                u = jnp.where(mask, u, NEG_INF)
            _online_update_t(u, far_bias * LOG2E, v_t, m_ref.at[mp], l_ref.at[mp], acc_ref.at[mp])

    def far_body(j, carry):
        step(j, None, None)
        return carry

    lax.fori_loop(0, jnp.maximum(i - 1, 0), far_body, 0)

    @pl.when(i > 0)
    def _near():
        step(i - 1, bias_near_ref[...], None)

    step(i, bias_diag_ref[...], qry >= key)

    lam = (jnp.exp(jnp.sum(lq1_ref[...] * lk1_ref[...], axis=-1, keepdims=True))
           - jnp.exp(jnp.sum(lq2_ref[...] * lk2_ref[...], axis=-1, keepdims=True))
           + LAM_INIT)
    o_t = acc_ref[0] * (1.0 / l_ref[0]) - lam * (acc_ref[1] * (1.0 / l_ref[1]))
    y_t = o_t * lax.rsqrt(jnp.mean(o_t * o_t, axis=0, keepdims=True) + EPS)
    y_t = (y_t * g_ref[...]) * (1.0 - LAM_INIT)
    o_ref[...] = y_t.T.astype(o_ref.dtype)


def _diff_attention(qkv, tab, lq1, lk1, lq2, lk2, g_col):
    bsz, seq, _ = qkv.shape
    blk = ATTN_BLOCK
    w = 2 * HEAD_DIM
    k_off = DIFF_WIDTH // w
    v_off = 2 * DIFF_WIDTH // w
    vec = pl.BlockSpec((1, HEAD_DIM), lambda h, b, i: (0, 0))
    return pl.pallas_call(
        _diff_attn_kernel,
        grid=(N_DIFF_HEADS, bsz, seq // blk),
        in_specs=[pl.BlockSpec(memory_space=pltpu.SMEM),
                  vec, vec, vec, vec,
                  pl.BlockSpec((None, blk, w), lambda h, b, i: (b, i, h)),
                  pl.BlockSpec((None, seq, w), lambda h, b, i: (b, 0, k_off + h)),
                  pl.BlockSpec((None, seq, w), lambda h, b, i: (b, 0, v_off + h)),
                  pl.BlockSpec((w, 1), lambda h, b, i: (0, 0))],
        out_specs=pl.BlockSpec((None, blk, w), lambda h, b, i: (b, i, h)),
        out_shape=jax.ShapeDtypeStruct((bsz, seq, MIX_WIDTH), jnp.bfloat16),
        scratch_shapes=[pltpu.VMEM((blk, blk), jnp.float32),
                        pltpu.VMEM((blk, blk), jnp.float32),
                        pltpu.VMEM((w, seq), jnp.bfloat16),
                        pltpu.VMEM((2, 1, blk), jnp.float32),
                        pltpu.VMEM((2, 1, blk), jnp.float32),
                        pltpu.VMEM((2, w, blk), jnp.float32)],
        compiler_params=_params(("arbitrary", "arbitrary", "arbitrary")),
        name="diff_attention",
    )(tab, lq1, lk1, lq2, lk2, qkv, qkv, qkv, g_col)


def _fox_attn_kernel(q_ref, k_ref, v_ref, cq_ref, ck_ref, mixed_hbm_ref, o_ref,
                     vt_ref, ckb_ref, m_ref, l_ref, acc_ref):
    del mixed_hbm_ref
    i = pl.program_id(2)
    blk = ATTN_BLOCK
    n_rep = blk // LANES

    @pl.when(i == 0)
    def _per_head_setup():
        _store_transposed(v_ref, vt_ref)
        for c in range(0, ckb_ref.shape[0], blk):
            row = ck_ref[:, c:c + blk] * LOG2E
            ckb_ref[c:c + blk, :] = jnp.broadcast_to(row, (LANES, blk)).T

    _init_stats(m_ref, l_ref, acc_ref)
    q = q_ref[...]
    cq = cq_ref[...] * LOG2E

    def step(j, mask):
        start = pl.multiple_of(j * blk, blk)
        k = k_ref[pl.ds(start, blk), :]
        v_t = vt_ref[:, pl.ds(start, blk)]
        ck = ckb_ref[pl.ds(start, blk), :]
        u = _scores_t(k, q) * (SCALE * LOG2E) - jnp.concatenate([ck] * n_rep, axis=1)
        if mask is not None:
            u = jnp.where(mask, u, NEG_INF)
        _online_update_t(u, cq, v_t, m_ref, l_ref, acc_ref)

    def body(j, carry):
        step(j, None)
        return carry

    lax.fori_loop(0, i, body, 0)
    key = lax.broadcasted_iota(jnp.int32, (blk, blk), 0)
    qry = lax.broadcasted_iota(jnp.int32, (blk, blk), 1)
    step(i, qry >= key)
    o_t = acc_ref[...] * (1.0 / l_ref[...])
    o_ref[...] = o_t.T.astype(o_ref.dtype)


def _fox_attention(qkv, cum, mixed):
    bsz, seq, _ = qkv.shape
    blk = ATTN_BLOCK
    q_off = 3 * DIFF_WIDTH // HEAD_DIM
    k_off = q_off + N_FOX_HEADS
    v_off = k_off + N_FOX_HEADS
    o_off = DIFF_WIDTH // HEAD_DIM
    return pl.pallas_call(
        _fox_attn_kernel,
        grid=(N_FOX_HEADS, bsz, seq // blk),
        in_specs=[pl.BlockSpec((None, blk, HEAD_DIM), lambda h, b, i: (b, i, q_off + h)),
                  pl.BlockSpec((None, seq, HEAD_DIM), lambda h, b, i: (b, 0, k_off + h)),
                  pl.BlockSpec((None, seq, HEAD_DIM), lambda h, b, i: (b, 0, v_off + h)),
                  pl.BlockSpec((None, None, 1, blk), lambda h, b, i: (h, b, 0, i)),
                  pl.BlockSpec((None, None, 1, seq), lambda h, b, i: (h, b, 0, 0)),
                  pl.BlockSpec(memory_space=pl.ANY)],
        out_specs=pl.BlockSpec((None, blk, HEAD_DIM), lambda h, b, i: (b, i, o_off + h)),
        out_shape=jax.ShapeDtypeStruct(mixed.shape, mixed.dtype),
        scratch_shapes=[pltpu.VMEM((HEAD_DIM, seq), jnp.bfloat16),
                        pltpu.VMEM((seq, LANES), jnp.float32),
                        pltpu.VMEM((1, blk), jnp.float32),
                        pltpu.VMEM((1, blk), jnp.float32),
                        pltpu.VMEM((HEAD_DIM, blk), jnp.float32)],
        input_output_aliases={5: 0},
        compiler_params=_params(("arbitrary", "arbitrary", "arbitrary")),
        name="fox_attention",
    )(qkv, qkv, qkv, cum, cum, mixed)


def _proj_residual_kernel(a_ref, w_ref, res_ref, g_ref, *out_refs, emit_sum, n_k):
    k = pl.program_id(1)
    acc_ref = out_refs[0]

    @pl.when(k == 0)
    def _init():
        acc_ref[...] = res_ref[...]

    a = a_ref[...]
    for c in range(0, acc_ref.shape[1], PROJ_COL_CHUNK):
        cols = slice(c, c + PROJ_COL_CHUNK)
        acc_ref[:, cols] += jnp.dot(a, w_ref[:, cols], preferred_element_type=jnp.float32)

    @pl.when(k == n_k - 1)
    def _finish():
        for r in range(0, acc_ref.shape[0], PROJ_ROW_CHUNK):
            rows = slice(r, r + PROJ_ROW_CHUNK)
            x = acc_ref[rows, :]
            y = x * lax.rsqrt(jnp.mean(x * x, axis=-1, keepdims=True) + EPS) * g_ref[...]
            if emit_sum:
                out_refs[1][rows, :] = y.astype(out_refs[1].dtype)
            else:
                acc_ref[rows, :] = y


def _proj_residual_norm(a, w, res, g, *, emit_sum, bm, bk, name):
    m, kdim = a.shape
    d = w.shape[1]
    n_k = kdim // bk
    row_block = pl.BlockSpec((bm, d), lambda i, k: (i, 0))
    if emit_sum:
        out_specs = [row_block, row_block]
        out_shape = [jax.ShapeDtypeStruct((m, d), jnp.float32),
                     jax.ShapeDtypeStruct((m, d), jnp.bfloat16)]
    else:
        out_specs = [row_block]
        out_shape = [jax.ShapeDtypeStruct((m, d), jnp.float32)]
    return pl.pallas_call(
        functools.partial(_proj_residual_kernel, emit_sum=emit_sum, n_k=n_k),
        grid=(m // bm, n_k),
        in_specs=[pl.BlockSpec((bm, bk), lambda i, k: (i, k)),
                  pl.BlockSpec((bk, d), lambda i, k: (k, 0)),
                  row_block,
                  pl.BlockSpec((1, d), lambda i, k: (0, 0))],
        out_specs=out_specs,
        out_shape=out_shape,
        compiler_params=_params(("arbitrary", "arbitrary")),
        name=name,
    )(a, w, res, g)


def _gate_up_kernel(h_ref, wg_ref, wu_ref, o_ref):
    h = h_ref[...]
    gate = jnp.dot(h, wg_ref[...], preferred_element_type=jnp.float32)
    up = jnp.dot(h, wu_ref[...], preferred_element_type=jnp.float32)
    o_ref[...] = (gate * jax.nn.sigmoid(gate) * up).astype(o_ref.dtype)


def _gate_up(h, wg, wu, bm=1024, bn=FF_TILE):
    m, k = h.shape
    n = wg.shape[1]
    return pl.pallas_call(
        _gate_up_kernel,
        grid=(m // bm, n // bn),
        in_specs=[pl.BlockSpec((bm, k), lambda i, j: (i, 0)),
                  pl.BlockSpec((k, bn), lambda i, j: (0, j)),
                  pl.BlockSpec((k, bn), lambda i, j: (0, j))],
        out_specs=pl.BlockSpec((bm, bn), lambda i, j: (i, j)),
        out_shape=jax.ShapeDtypeStruct((m, n), jnp.bfloat16),
        compiler_params=_params(("arbitrary", "arbitrary")),
        name="ffn_gate_up",
    )(h, wg, wu)


def kernel(x, attn_norm_g, w_in, b_f, lambda_q1, lambda_k1, lambda_q2, lambda_k2, rel_bias_table,
           diff_subln_g, w_o, ffn_norm_g, w_gate, w_up, w_down, final_norm_g):
    bsz, seq, d = x.shape
    m = bsz * seq
    bf16 = jnp.bfloat16
    x2d = x.reshape(m, d)
    ff_pad = D_FF_PAD - D_FF

    w_in_b = w_in[0].astype(bf16)
    wf_t = w_in[0, :, QKV_COLS:].T.astype(bf16)
    w_o_b = w_o[0].astype(bf16)
    w_gate_b = jnp.pad(w_gate[0].astype(bf16), ((0, 0), (0, ff_pad)))
    w_up_b = jnp.pad(w_up[0].astype(bf16), ((0, 0), (0, ff_pad)))
    w_down_b = jnp.pad(w_down[0].astype(bf16), ((0, ff_pad), (0, 0)))

    h, fl_t = _norm_and_gate_logits(x2d, attn_norm_g[0].reshape(1, d), wf_t)
    qkv = _matmul(h, w_in_b, QKV_COLS, name="in_proj").reshape(bsz, seq, QKV_COLS)

    cum = _cum_log_forget(fl_t, b_f[0].reshape(N_FOX_HEADS, 1), seq)
    cum = cum.reshape(N_FOX_HEADS, bsz, 1, seq)

    mixed = _diff_attention(qkv, rel_bias_table,
                            lambda_q1[0].reshape(1, HEAD_DIM), lambda_k1[0].reshape(1, HEAD_DIM),
                            lambda_q2[0].reshape(1, HEAD_DIM), lambda_k2[0].reshape(1, HEAD_DIM),
                            diff_subln_g[0].reshape(2 * HEAD_DIM, 1))
    mixed = _fox_attention(qkv, cum, mixed)

    x1, h2 = _proj_residual_norm(mixed.reshape(m, MIX_WIDTH), w_o_b, x2d,
                                 ffn_norm_g[0].reshape(1, d),
                                 emit_sum=True, bm=512, bk=512, name="out_proj")
    act = _gate_up(h2, w_gate_b, w_up_b)
    (out,) = _proj_residual_norm(act, w_down_b, x1, final_norm_g.reshape(1, d),
                                 emit_sum=False, bm=512, bk=1024, name="ffn_down")
    return out.reshape(bsz, seq, d)
```

```python
import functools
import math

import numpy as np

import jax
import jax.numpy as jnp
from jax import lax
from jax.experimental import pallas as pl
from jax.experimental.pallas import tpu as pltpu

D_MODEL = 4096
HEAD_DIM = 128
N_DIFF_HEADS = D_MODEL // (4 * HEAD_DIM)
N_FOX_HEADS = D_MODEL // (2 * HEAD_DIM)
DIFF_WIDTH = N_DIFF_HEADS * 2 * HEAD_DIM
FOX_WIDTH = N_FOX_HEADS * HEAD_DIM
MIX_WIDTH = DIFF_WIDTH + FOX_WIDTH
QKV_COLS = 3 * DIFF_WIDTH + 3 * FOX_WIDTH
D_FF = ((8 * D_MODEL + 3 * 256 - 1) // (3 * 256)) * 256
NUM_BUCKETS = 32
MAX_DISTANCE = 128
EPS = 1e-6
NEG_INF = -1e30
LAM_INIT = 0.8 - 0.6 * math.exp(-0.3 * 0)
SCALE = HEAD_DIM ** -0.5
LOG2E = math.log2(math.e)

LANES = 128
FF_TILE = 512
D_FF_PAD = ((D_FF + FF_TILE - 1) // FF_TILE) * FF_TILE
VMEM_LIMIT = 56 * 1024 * 1024

ATTN_BLOCK = 512
PROJ_COL_CHUNK = 1024
PROJ_ROW_CHUNK = 128

FAR, NEAR, DIAG = "far", "near", "diag"


def _params(sem, vmem=VMEM_LIMIT):
    return pltpu.CompilerParams(dimension_semantics=sem, vmem_limit_bytes=vmem)


def _norm_kernel(x_ref, g_ref, wf_ref, h_ref, fl_ref):
    x = x_ref[...]
    y = x * lax.rsqrt(jnp.mean(x * x, axis=-1, keepdims=True) + EPS)
    h = (y * g_ref[...]).astype(jnp.bfloat16)
    h_ref[...] = h
    fl_ref[...] = lax.dot_general(wf_ref[...], h, (((1,), (1,)), ((), ())),
                                  preferred_element_type=jnp.float32)


def _norm_and_gate_logits(x2d, g, wf_t, bm=512):
    m, d = x2d.shape
    nh = wf_t.shape[0]
    return pl.pallas_call(
        _norm_kernel,
        grid=(m // bm,),
        in_specs=[pl.BlockSpec((bm, d), lambda i: (i, 0)),
                  pl.BlockSpec((1, d), lambda i: (0, 0)),
                  pl.BlockSpec((nh, d), lambda i: (0, 0))],
        out_specs=[pl.BlockSpec((bm, d), lambda i: (i, 0)),
                   pl.BlockSpec((nh, bm), lambda i: (0, i))],
        out_shape=[jax.ShapeDtypeStruct((m, d), jnp.bfloat16),
                   jax.ShapeDtypeStruct((nh, m), jnp.float32)],
        compiler_params=_params(("arbitrary",)),
        name="attn_norm",
    )(x2d, g, wf_t)


def _matmul_kernel(a_ref, w_ref, o_ref):
    o_ref[...] = jnp.dot(a_ref[...], w_ref[...],
                         preferred_element_type=jnp.float32).astype(o_ref.dtype)


def _matmul(a, w, n_out, bm=1024, bn=1024, name="matmul"):
    m, k = a.shape
    return pl.pallas_call(
        _matmul_kernel,
        grid=(m // bm, n_out // bn),
        in_specs=[pl.BlockSpec((bm, k), lambda i, j: (i, 0)),
                  pl.BlockSpec((k, bn), lambda i, j: (0, j))],
        out_specs=pl.BlockSpec((bm, bn), lambda i, j: (i, j)),
        out_shape=jax.ShapeDtypeStruct((m, n_out), jnp.bfloat16),
        compiler_params=_params(("arbitrary", "arbitrary")),
        name=name,
    )(a, w)


def _cumsum_kernel(fl_ref, bf_ref, c_ref):
    z = fl_ref[...] + bf_ref[...]
    x = jnp.minimum(z, 0.0) - jnp.log(1.0 + jnp.exp(-jnp.abs(z)))
    n = x.shape[-1]
    pos = lax.broadcasted_iota(jnp.int32, x.shape, 1)
    shift = 1
    while shift < n:
        x = x + jnp.where(pos >= shift, pltpu.roll(x, shift, 1), 0.0)
        shift *= 2
    c_ref[...] = x


def _cum_log_forget(fl_t, b_f, seq):
    nh, m = fl_t.shape
    return pl.pallas_call(
        _cumsum_kernel,
        grid=(m // seq,),
        in_specs=[pl.BlockSpec((nh, seq), lambda b: (0, b)),
                  pl.BlockSpec((nh, 1), lambda b: (0, 0))],
        out_specs=pl.BlockSpec((nh, seq), lambda b: (0, b)),
        out_shape=jax.ShapeDtypeStruct((nh, m), jnp.float32),
        compiler_params=_params(("arbitrary",)),
        name="cum_log_forget",
    )(fl_t, b_f)


def _scores_t(k, q):
    return lax.dot_general(k, q, (((1,), (1,)), ((), ())), preferred_element_type=jnp.float32)


def _softmax_pv_update(u, blk_max, shift, v_t, m_ref, l_ref, acc_ref, qs):
    blk = u.shape[1]
    m_old = m_ref[:, pl.ds(qs, blk)]
    m_new = jnp.maximum(m_old, blk_max + shift)
    alpha = jnp.exp2(m_old - m_new)
    p = jnp.exp2(u + (shift - m_new))
    l_ref[:, pl.ds(qs, blk)] = alpha * l_ref[:, pl.ds(qs, blk)] + jnp.sum(p, axis=0, keepdims=True)
    acc_ref[:, pl.ds(qs, blk)] = alpha * acc_ref[:, pl.ds(qs, blk)] + jnp.dot(
        v_t, p.astype(v_t.dtype), preferred_element_type=jnp.float32)
    m_ref[:, pl.ds(qs, blk)] = m_new


def _pair_schedule(n_blocks, with_near):
    far, near, diag = [], [], []
    for i in range(n_blocks):
        for j in range(i + 1):
            if j == i:
                diag.append((i, j))
            elif with_near and j == i - 1:
                near.append((i, j))
            else:
                far.append((i, j))
    pairs = far + near + diag
    kinds = [FAR] * len(far) + [NEAR] * len(near) + [DIAG] * len(diag)
    return kinds, np.asarray(pairs, np.int32).T.copy()


def _run_pipeline(kinds, stage_a, stage_b):
    n = len(kinds)
    next_kind = list(kinds[1:]) + [None]
    stage_a(0, kinds[0], 0)
    s = 0
    while s < n:
        e = s
        while e < n and next_kind[e] == next_kind[s]:
            e += 1
        kind, par, n_double = next_kind[s], s % 2, (e - s) // 2

        def one(step, parity):
            if kind is not None:
                stage_a(step + 1, kind, 1 - parity)
            stage_b(step, parity)

        if n_double:
            def body(r, carry):
                base = s + 2 * r
                one(base, par)
                one(base + 1, 1 - par)
                return carry

            lax.fori_loop(0, n_double, body, 0)
        if (e - s) % 2:
            one(e - 1, (e - 1) % 2)
        s = e


def _store_transposed(src_ref, dst_ref):
    for c in range(0, src_ref.shape[0], ATTN_BLOCK):
        dst_ref[:, c:c + ATTN_BLOCK] = (
            src_ref[c:c + ATTN_BLOCK, :].astype(jnp.float32).T.astype(dst_ref.dtype))


def _t5_bucket(n):
    max_exact = NUM_BUCKETS // 2
    nf = jnp.maximum(n, 1).astype(jnp.float32)
    large = max_exact + (jnp.log(nf / max_exact) / math.log(MAX_DISTANCE / max_exact)
                         * (NUM_BUCKETS - max_exact)).astype(jnp.int32)
    large = jnp.minimum(large, NUM_BUCKETS - 1)
    return jnp.where(n < max_exact, n, large)


def _diff_attn_kernel(kinds, pair_ref, tab_ref, lq1_ref, lk1_ref, lq2_ref, lk2_ref,
                      q_ref, k_ref, v_ref, g_ref, o_ref,
                      bias_near_ref, bias_diag_ref, vt_ref, u0_ref, u1_ref, bm0_ref, bm1_ref,
                      m_ref, l_ref, acc_ref):
    h = pl.program_id(0)
    b = pl.program_id(1)
    blk = ATTN_BLOCK
    key = lax.broadcasted_iota(jnp.int32, (blk, blk), 0)
    qry = lax.broadcasted_iota(jnp.int32, (blk, blk), 1)
    u_refs, bm_refs = (u0_ref, u1_ref), (bm0_ref, bm1_ref)
    far_bias = tab_ref[NUM_BUCKETS - 1, h]

    @pl.when(b == 0)
    def _build_bias():
        for ref, offset in ((bias_diag_ref, 0), (bias_near_ref, blk)):
            bucket = _t5_bucket(jnp.maximum(qry - key + offset, 0))
            bias = jnp.zeros((blk, blk), jnp.float32)
            for t in range(NUM_BUCKETS):
                bias = jnp.where(bucket == t, tab_ref[t, h], bias)
            ref[...] = (bias - far_bias) * LOG2E

    _store_transposed(v_ref, vt_ref)
    m_ref[...] = jnp.full(m_ref.shape, NEG_INF, jnp.float32)
    l_ref[...] = jnp.zeros(l_ref.shape, jnp.float32)
    acc_ref[...] = jnp.zeros(acc_ref.shape, jnp.float32)

    def starts(t):
        return (pl.multiple_of(pair_ref[0, t] * blk, blk), pl.multiple_of(pair_ref[1, t] * blk, blk))

    def stage_a(t, kind, par):
        qs, ks = starts(t)
        for mp in range(2):
            cols = slice(mp * HEAD_DIM, (mp + 1) * HEAD_DIM)
            u = _scores_t(k_ref[pl.ds(ks, blk), cols], q_ref[pl.ds(qs, blk), cols]) * (SCALE * LOG2E)
            if kind == NEAR:
                u = u + bias_near_ref[...]
            elif kind == DIAG:
                u = jnp.where(qry >= key, u + bias_diag_ref[...], NEG_INF)
            u_refs[par][mp] = u
            bm_refs[par][mp] = jnp.max(u, axis=0, keepdims=True)

    def stage_b(t, par):
        qs, ks = starts(t)
        v_t = vt_ref[:, pl.ds(ks, blk)]
        for mp in range(2):
            _softmax_pv_update(u_refs[par][mp], bm_refs[par][mp], far_bias * LOG2E, v_t,
                               m_ref.at[mp], l_ref.at[mp], acc_ref.at[mp], qs)

    _run_pipeline(kinds, stage_a, stage_b)

    lam = (jnp.exp(jnp.sum(lq1_ref[...] * lk1_ref[...], axis=-1, keepdims=True))
           - jnp.exp(jnp.sum(lq2_ref[...] * lk2_ref[...], axis=-1, keepdims=True))
           + LAM_INIT)
    for c in range(0, o_ref.shape[0], blk):
        cols = slice(c, c + blk)
        o_t = (acc_ref[0, :, cols] * (1.0 / l_ref[0, :, cols])
               - lam * (acc_ref[1, :, cols] * (1.0 / l_ref[1, :, cols])))
        y_t = o_t * lax.rsqrt(jnp.mean(o_t * o_t, axis=0, keepdims=True) + EPS)
        y_t = (y_t * g_ref[...]) * (1.0 - LAM_INIT)
        o_ref[cols, :] = y_t.T.astype(o_ref.dtype)


def _diff_attention(qkv, tab, lq1, lk1, lq2, lk2, g_col):
    bsz, seq, _ = qkv.shape
    blk = ATTN_BLOCK
    w = 2 * HEAD_DIM
    k_off = DIFF_WIDTH // w
    v_off = 2 * DIFF_WIDTH // w
    kinds, pairs = _pair_schedule(seq // blk, with_near=True)
    smem = pl.BlockSpec(memory_space=pltpu.SMEM)
    vec = pl.BlockSpec((1, HEAD_DIM), lambda h, b: (0, 0))
    return pl.pallas_call(
        functools.partial(_diff_attn_kernel, kinds),
        grid=(N_DIFF_HEADS, bsz),
        in_specs=[smem, smem, vec, vec, vec, vec,
                  pl.BlockSpec((None, seq, w), lambda h, b: (b, 0, h)),
                  pl.BlockSpec((None, seq, w), lambda h, b: (b, 0, k_off + h)),
                  pl.BlockSpec((None, seq, w), lambda h, b: (b, 0, v_off + h)),
                  pl.BlockSpec((w, 1), lambda h, b: (0, 0))],
        out_specs=pl.BlockSpec((None, seq, w), lambda h, b: (b, 0, h)),
        out_shape=jax.ShapeDtypeStruct((bsz, seq, MIX_WIDTH), jnp.bfloat16),
        scratch_shapes=[pltpu.VMEM((blk, blk), jnp.float32),
                        pltpu.VMEM((blk, blk), jnp.float32),
                        pltpu.VMEM((w, seq), jnp.bfloat16),
                        pltpu.VMEM((2, blk, blk), jnp.float32),
                        pltpu.VMEM((2, blk, blk), jnp.float32),
                        pltpu.VMEM((2, 1, blk), jnp.float32),
                        pltpu.VMEM((2, 1, blk), jnp.float32),
                        pltpu.VMEM((2, 1, seq), jnp.float32),
                        pltpu.VMEM((2, 1, seq), jnp.float32),
                        pltpu.VMEM((2, w, seq), jnp.float32)],
        compiler_params=_params(("arbitrary", "arbitrary")),
        name="diff_attention",
    )(jnp.asarray(pairs), tab, lq1, lk1, lq2, lk2, qkv, qkv, qkv, g_col)


def _fox_attn_kernel(kinds, pair_ref, q_ref, k_ref, v_ref, cum_ref, mixed_hbm_ref, o_ref,
                     vt_ref, ckb_ref, u0_ref, u1_ref, bm0_ref, bm1_ref, m_ref, l_ref, acc_ref):
    del mixed_hbm_ref
    blk = ATTN_BLOCK
    n_rep = blk // LANES
    key = lax.broadcasted_iota(jnp.int32, (blk, blk), 0)
    qry = lax.broadcasted_iota(jnp.int32, (blk, blk), 1)
    u_refs, bm_refs = (u0_ref, u1_ref), (bm0_ref, bm1_ref)

    _store_transposed(v_ref, vt_ref)
    for c in range(0, ckb_ref.shape[0], blk):
        row = cum_ref[:, c:c + blk] * LOG2E
        ckb_ref[c:c + blk, :] = jnp.broadcast_to(row, (LANES, blk)).T
    m_ref[...] = jnp.full(m_ref.shape, NEG_INF, jnp.float32)
    l_ref[...] = jnp.zeros(l_ref.shape, jnp.float32)
    acc_ref[...] = jnp.zeros(acc_ref.shape, jnp.float32)

    def starts(t):
        return (pl.multiple_of(pair_ref[0, t] * blk, blk), pl.multiple_of(pair_ref[1, t] * blk, blk))

    def stage_a(t, kind, par):
        qs, ks = starts(t)
        ck = ckb_ref[pl.ds(ks, blk), :]
        u = (_scores_t(k_ref[pl.ds(ks, blk), :], q_ref[pl.ds(qs, blk), :]) * (SCALE * LOG2E)
             - jnp.concatenate([ck] * n_rep, axis=1))
        if kind == DIAG:
            u = jnp.where(qry >= key, u, NEG_INF)
        u_refs[par][...] = u
        bm_refs[par][...] = jnp.max(u, axis=0, keepdims=True)

    def stage_b(t, par):
        qs, ks = starts(t)
        cq = cum_ref[:, pl.ds(qs, blk)] * LOG2E
        _softmax_pv_update(u_refs[par][...], bm_refs[par][...], cq, vt_ref[:, pl.ds(ks, blk)],
                           m_ref, l_ref, acc_ref, qs)

    _run_pipeline(kinds, stage_a, stage_b)

    for c in range(0, o_ref.shape[0], blk):
        cols = slice(c, c + blk)
        o_t = acc_ref[:, cols] * (1.0 / l_ref[:, cols])
        o_ref[cols, :] = o_t.T.astype(o_ref.dtype)


def _fox_attention(qkv, cum, mixed):
    bsz, seq, _ = qkv.shape
    blk = ATTN_BLOCK
    q_off = 3 * DIFF_WIDTH // HEAD_DIM
    k_off = q_off + N_FOX_HEADS
    v_off = k_off + N_FOX_HEADS
    o_off = DIFF_WIDTH // HEAD_DIM
    kinds, pairs = _pair_schedule(seq // blk, with_near=False)
    return pl.pallas_call(
        functools.partial(_fox_attn_kernel, kinds),
        grid=(N_FOX_HEADS, bsz),
        in_specs=[pl.BlockSpec(memory_space=pltpu.SMEM),
                  pl.BlockSpec((None, seq, HEAD_DIM), lambda h, b: (b, 0, q_off + h)),
                  pl.BlockSpec((None, seq, HEAD_DIM), lambda h, b: (b, 0, k_off + h)),
                  pl.BlockSpec((None, seq, HEAD_DIM), lambda h, b: (b, 0, v_off + h)),
                  pl.BlockSpec((None, None, 1, seq), lambda h, b: (h, b, 0, 0)),
                  pl.BlockSpec(memory_space=pl.ANY)],
        out_specs=pl.BlockSpec((None, seq, HEAD_DIM), lambda h, b: (b, 0, o_off + h)),
        out_shape=jax.ShapeDtypeStruct(mixed.shape, mixed.dtype),
        scratch_shapes=[pltpu.VMEM((HEAD_DIM, seq), jnp.bfloat16),
                        pltpu.VMEM((seq, LANES), jnp.float32),
                        pltpu.VMEM((blk, blk), jnp.float32),
                        pltpu.VMEM((blk, blk), jnp.float32),
                        pltpu.VMEM((1, blk), jnp.float32),
                        pltpu.VMEM((1, blk), jnp.float32),
                        pltpu.VMEM((1, seq), jnp.float32),
                        pltpu.VMEM((1, seq), jnp.float32),
                        pltpu.VMEM((HEAD_DIM, seq), jnp.float32)],
        input_output_aliases={5: 0},
        compiler_params=_params(("arbitrary", "arbitrary")),
        name="fox_attention",
    )(jnp.asarray(pairs), qkv, qkv, qkv, cum, mixed)


def _proj_residual_kernel(a_ref, w_ref, res_ref, g_ref, *out_refs, emit_sum, n_k):
    k = pl.program_id(1)
    acc_ref = out_refs[0]

    @pl.when(k == 0)
    def _init():
        acc_ref[...] = res_ref[...]

    a = a_ref[...]
    for c in range(0, acc_ref.shape[1], PROJ_COL_CHUNK):
        cols = slice(c, c + PROJ_COL_CHUNK)
        acc_ref[:, cols] += jnp.dot(a, w_ref[:, cols], preferred_element_type=jnp.float32)

    @pl.when(k == n_k - 1)
    def _finish():
        for r in range(0, acc_ref.shape[0], PROJ_ROW_CHUNK):
            rows = slice(r, r + PROJ_ROW_CHUNK)
            x = acc_ref[rows, :]
            y = x * lax.rsqrt(jnp.mean(x * x, axis=-1, keepdims=True) + EPS) * g_ref[...]
            if emit_sum:
                out_refs[1][rows, :] = y.astype(out_refs[1].dtype)
            else:
                acc_ref[rows, :] = y


def _proj_residual_norm(a, w, res, g, *, emit_sum, bm, bk, name):
    m, kdim = a.shape
    d = w.shape[1]
    n_k = kdim // bk
    row_block = pl.BlockSpec((bm, d), lambda i, k: (i, 0))
    if emit_sum:
        out_specs = [row_block, row_block]
        out_shape = [jax.ShapeDtypeStruct((m, d), jnp.float32),
                     jax.ShapeDtypeStruct((m, d), jnp.bfloat16)]
    else:
        out_specs = [row_block]
        out_shape = [jax.ShapeDtypeStruct((m, d), jnp.float32)]
    return pl.pallas_call(
        functools.partial(_proj_residual_kernel, emit_sum=emit_sum, n_k=n_k),
        grid=(m // bm, n_k),
        in_specs=[pl.BlockSpec((bm, bk), lambda i, k: (i, k)),
                  pl.BlockSpec((bk, d), lambda i, k: (k, 0)),
                  row_block,
                  pl.BlockSpec((1, d), lambda i, k: (0, 0))],
        out_specs=out_specs,
        out_shape=out_shape,
        compiler_params=_params(("arbitrary", "arbitrary")),
        name=name,
    )(a, w, res, g)


def _gate_up_kernel(h_ref, wg_ref, wu_ref, o_ref):
    h = h_ref[...]
    gate = jnp.dot(h, wg_ref[...], preferred_element_type=jnp.float32)
    up = jnp.dot(h, wu_ref[...], preferred_element_type=jnp.float32)
    o_ref[...] = (gate * jax.nn.sigmoid(gate) * up).astype(o_ref.dtype)


def _gate_up(h, wg, wu, bm=1024, bn=FF_TILE):
    m, k = h.shape
    n = wg.shape[1]
    return pl.pallas_call(
        _gate_up_kernel,
        grid=(m // bm, n // bn),
        in_specs=[pl.BlockSpec((bm, k), lambda i, j: (i, 0)),
                  pl.BlockSpec((k, bn), lambda i, j: (0, j)),
                  pl.BlockSpec((k, bn), lambda i, j: (0, j))],
        out_specs=pl.BlockSpec((bm, bn), lambda i, j: (i, j)),
        out_shape=jax.ShapeDtypeStruct((m, n), jnp.bfloat16),
        compiler_params=_params(("arbitrary", "arbitrary")),
        name="ffn_gate_up",
    )(h, wg, wu)


def kernel(x, attn_norm_g, w_in, b_f, lambda_q1, lambda_k1, lambda_q2, lambda_k2, rel_bias_table,
           diff_subln_g, w_o, ffn_norm_g, w_gate, w_up, w_down, final_norm_g):
    bsz, seq, d = x.shape
    m = bsz * seq
    bf16 = jnp.bfloat16
    x2d = x.reshape(m, d)
    ff_pad = D_FF_PAD - D_FF

    w_in_b = w_in[0].astype(bf16)
    wf_t = w_in[0, :, QKV_COLS:].T.astype(bf16)
    w_o_b = w_o[0].astype(bf16)
    w_gate_b = jnp.pad(w_gate[0].astype(bf16), ((0, 0), (0, ff_pad)))
    w_up_b = jnp.pad(w_up[0].astype(bf16), ((0, 0), (0, ff_pad)))
    w_down_b = jnp.pad(w_down[0].astype(bf16), ((0, ff_pad), (0, 0)))

    h, fl_t = _norm_and_gate_logits(x2d, attn_norm_g[0].reshape(1, d), wf_t)
    qkv = _matmul(h, w_in_b, QKV_COLS, name="in_proj").reshape(bsz, seq, QKV_COLS)

    cum = _cum_log_forget(fl_t, b_f[0].reshape(N_FOX_HEADS, 1), seq)
    cum = cum.reshape(N_FOX_HEADS, bsz, 1, seq)

    mixed = _diff_attention(qkv, rel_bias_table,
                            lambda_q1[0].reshape(1, HEAD_DIM), lambda_k1[0].reshape(1, HEAD_DIM),
                            lambda_q2[0].reshape(1, HEAD_DIM), lambda_k2[0].reshape(1, HEAD_DIM),
                            diff_subln_g[0].reshape(2 * HEAD_DIM, 1))
    mixed = _fox_attention(qkv, cum, mixed)

    x1, h2 = _proj_residual_norm(mixed.reshape(m, MIX_WIDTH), w_o_b, x2d,
                                 ffn_norm_g[0].reshape(1, d),
                                 emit_sum=True, bm=512, bk=512, name="out_proj")
    act = _gate_up(h2, w_gate_b, w_up_b)
    (out,) = _proj_residual_norm(act, w_down_b, x1, final_norm_g.reshape(1, d),
                                 emit_sum=False, bm=512, bk=1024, name="ffn_down")
    return out.reshape(bsz, seq, d)
```

```python
import functools
import math

import numpy as np

import jax
import jax.numpy as jnp
from jax import lax
from jax.experimental import pallas as pl
from jax.experimental.pallas import tpu as pltpu

D_MODEL = 4096
HEAD_DIM = 128
N_DIFF_HEADS = D_MODEL // (4 * HEAD_DIM)
N_FOX_HEADS = D_MODEL // (2 * HEAD_DIM)
DIFF_WIDTH = N_DIFF_HEADS * 2 * HEAD_DIM
FOX_WIDTH = N_FOX_HEADS * HEAD_DIM
MIX_WIDTH = DIFF_WIDTH + FOX_WIDTH
QKV_COLS = 3 * DIFF_WIDTH + 3 * FOX_WIDTH
D_FF = ((8 * D_MODEL + 3 * 256 - 1) // (3 * 256)) * 256
NUM_BUCKETS = 32
MAX_DISTANCE = 128
EPS = 1e-6
NEG_INF = -1e30
LAM_INIT = 0.8 - 0.6 * math.exp(-0.3 * 0)
SCALE = HEAD_DIM ** -0.5
LOG2E = math.log2(math.e)

LANES = 128
ROW_TILE = 2048
FF_TILE = 256
DOWN_K_TILE = 1024
VMEM_LIMIT = 56 * 1024 * 1024

ATTN_BLOCK = 512
PROJ_COL_CHUNK = 1024
PROJ_ROW_CHUNK = 128

FAR, NEAR, DIAG = "far", "near", "diag"


def _params(sem, vmem=VMEM_LIMIT):
    return pltpu.CompilerParams(dimension_semantics=sem, vmem_limit_bytes=vmem)


def _norm_kernel(x_ref, g_ref, wf_ref, h_ref, fl_ref):
    x = x_ref[...]
    y = x * lax.rsqrt(jnp.mean(x * x, axis=-1, keepdims=True) + EPS)
    h = (y * g_ref[...]).astype(jnp.bfloat16)
    h_ref[...] = h
    fl_ref[...] = lax.dot_general(wf_ref[...], h, (((1,), (1,)), ((), ())),
                                  preferred_element_type=jnp.float32)


def _norm_and_gate_logits(x2d, g, wf_t, bm=512):
    m, d = x2d.shape
    nh = wf_t.shape[0]
    return pl.pallas_call(
        _norm_kernel,
        grid=(m // bm,),
        in_specs=[pl.BlockSpec((bm, d), lambda i: (i, 0)),
                  pl.BlockSpec((1, d), lambda i: (0, 0)),
                  pl.BlockSpec((nh, d), lambda i: (0, 0))],
        out_specs=[pl.BlockSpec((bm, d), lambda i: (i, 0)),
                   pl.BlockSpec((nh, bm), lambda i: (0, i))],
        out_shape=[jax.ShapeDtypeStruct((m, d), jnp.bfloat16),
                   jax.ShapeDtypeStruct((nh, m), jnp.float32)],
        compiler_params=_params(("arbitrary",)),
        name="attn_norm",
    )(x2d, g, wf_t)


def _matmul_kernel(a_ref, w_ref, o_ref):
    o_ref[...] = jnp.dot(a_ref[...], w_ref[...].astype(a_ref.dtype),
                         preferred_element_type=jnp.float32).astype(o_ref.dtype)


def _matmul(a, w, n_out, bm=ROW_TILE, bn=512, name="matmul"):
    m, k = a.shape
    return pl.pallas_call(
        _matmul_kernel,
        grid=(m // bm, n_out // bn),
        in_specs=[pl.BlockSpec((bm, k), lambda i, j: (i, 0), pipeline_mode=pl.Buffered(1)),
                  pl.BlockSpec((k, bn), lambda i, j: (0, j))],
        out_specs=pl.BlockSpec((bm, bn), lambda i, j: (i, j)),
        out_shape=jax.ShapeDtypeStruct((m, n_out), jnp.bfloat16),
        compiler_params=_params(("arbitrary", "arbitrary")),
        name=name,
    )(a, w)


def _cumsum_kernel(fl_ref, bf_ref, c_ref):
    z = fl_ref[...] + bf_ref[...]
    x = jnp.minimum(z, 0.0) - jnp.log(1.0 + jnp.exp(-jnp.abs(z)))
    n = x.shape[-1]
    pos = lax.broadcasted_iota(jnp.int32, x.shape, 1)
    shift = 1
    while shift < n:
        x = x + jnp.where(pos >= shift, pltpu.roll(x, shift, 1), 0.0)
        shift *= 2
    c_ref[...] = x


def _cum_log_forget(fl_t, b_f, seq):
    nh, m = fl_t.shape
    return pl.pallas_call(
        _cumsum_kernel,
        grid=(m // seq,),
        in_specs=[pl.BlockSpec((nh, seq), lambda b: (0, b)),
                  pl.BlockSpec((nh, 1), lambda b: (0, 0))],
        out_specs=pl.BlockSpec((nh, seq), lambda b: (0, b)),
        out_shape=jax.ShapeDtypeStruct((nh, m), jnp.float32),
        compiler_params=_params(("arbitrary",)),
        name="cum_log_forget",
    )(fl_t, b_f)


def _scores_t(k, q):
    return lax.dot_general(k, q, (((1,), (1,)), ((), ())), preferred_element_type=jnp.float32)


def _softmax_pv_update(u, blk_max, shift, v_t, m_ref, l_ref, acc_ref, qs):
    blk = u.shape[1]
    m_old = m_ref[:, pl.ds(qs, blk)]
    m_new = jnp.maximum(m_old, blk_max + shift)
    alpha = jnp.exp2(m_old - m_new)
    p = jnp.exp2(u + (shift - m_new))
    l_ref[:, pl.ds(qs, blk)] = alpha * l_ref[:, pl.ds(qs, blk)] + jnp.sum(p, axis=0, keepdims=True)
    acc_ref[:, pl.ds(qs, blk)] = alpha * acc_ref[:, pl.ds(qs, blk)] + jnp.dot(
        v_t, p.astype(v_t.dtype), preferred_element_type=jnp.float32)
    m_ref[:, pl.ds(qs, blk)] = m_new


def _pair_schedule(n_blocks, with_near):
    far, near, diag = [], [], []
    for i in range(n_blocks):
        for j in range(i + 1):
            if j == i:
                diag.append((i, j))
            elif with_near and j == i - 1:
                near.append((i, j))
            else:
                far.append((i, j))
    pairs = far + near + diag
    kinds = [FAR] * len(far) + [NEAR] * len(near) + [DIAG] * len(diag)
    return kinds, np.asarray(pairs, np.int32).T.copy()


def _run_pipeline(kinds, stage_a, stage_b):
    n = len(kinds)
    next_kind = list(kinds[1:]) + [None]
    stage_a(0, kinds[0], 0)
    s = 0
    while s < n:
        e = s
        while e < n and next_kind[e] == next_kind[s]:
            e += 1
        kind, par, n_double = next_kind[s], s % 2, (e - s) // 2

        def one(step, parity):
            if kind is not None:
                stage_a(step + 1, kind, 1 - parity)
            stage_b(step, parity)

        if n_double:
            def body(r, carry):
                base = s + 2 * r
                one(base, par)
                one(base + 1, 1 - par)
                return carry

            lax.fori_loop(0, n_double, body, 0)
        if (e - s) % 2:
            one(e - 1, (e - 1) % 2)
        s = e


def _store_transposed(src_ref, dst_ref):
    for c in range(0, src_ref.shape[0], ATTN_BLOCK):
        dst_ref[:, c:c + ATTN_BLOCK] = (
            src_ref[c:c + ATTN_BLOCK, :].astype(jnp.float32).T.astype(dst_ref.dtype))


def _t5_bucket(n):
    max_exact = NUM_BUCKETS // 2
    nf = jnp.maximum(n, 1).astype(jnp.float32)
    large = max_exact + (jnp.log(nf / max_exact) / math.log(MAX_DISTANCE / max_exact)
                         * (NUM_BUCKETS - max_exact)).astype(jnp.int32)
    large = jnp.minimum(large, NUM_BUCKETS - 1)
    return jnp.where(n < max_exact, n, large)


def _diff_attn_kernel(kinds, pair_ref, tab_ref, lq1_ref, lk1_ref, lq2_ref, lk2_ref,
                      q_ref, k_ref, v_ref, g_ref, o_ref,
                      bias_near_ref, bias_diag_ref, vt_ref, u0_ref, u1_ref, bm0_ref, bm1_ref,
                      m_ref, l_ref, acc_ref):
    h = pl.program_id(0)
    b = pl.program_id(1)
    blk = ATTN_BLOCK
    key = lax.broadcasted_iota(jnp.int32, (blk, blk), 0)
    qry = lax.broadcasted_iota(jnp.int32, (blk, blk), 1)
    u_refs, bm_refs = (u0_ref, u1_ref), (bm0_ref, bm1_ref)
    far_bias = tab_ref[NUM_BUCKETS - 1, h]

    @pl.when(b == 0)
    def _build_bias():
        for ref, offset in ((bias_diag_ref, 0), (bias_near_ref, blk)):
            bucket = _t5_bucket(jnp.maximum(qry - key + offset, 0))
            bias = jnp.zeros((blk, blk), jnp.float32)
            for t in range(NUM_BUCKETS):
                bias = jnp.where(bucket == t, tab_ref[t, h], bias)
            ref[...] = (bias - far_bias) * LOG2E

    _store_transposed(v_ref, vt_ref)
    m_ref[...] = jnp.full(m_ref.shape, NEG_INF, jnp.float32)
    l_ref[...] = jnp.zeros(l_ref.shape, jnp.float32)
    acc_ref[...] = jnp.zeros(acc_ref.shape, jnp.float32)

    def starts(t):
        return (pl.multiple_of(pair_ref[0, t] * blk, blk), pl.multiple_of(pair_ref[1, t] * blk, blk))

    def stage_a(t, kind, par):
        qs, ks = starts(t)
        for mp in range(2):
            cols = slice(mp * HEAD_DIM, (mp + 1) * HEAD_DIM)
            u = _scores_t(k_ref[pl.ds(ks, blk), cols], q_ref[pl.ds(qs, blk), cols]) * (SCALE * LOG2E)
            if kind == NEAR:
                u = u + bias_near_ref[...]
            elif kind == DIAG:
                u = jnp.where(qry >= key, u + bias_diag_ref[...], NEG_INF)
            u_refs[par][mp] = u
            bm_refs[par][mp] = jnp.max(u, axis=0, keepdims=True)

    def stage_b(t, par):
        qs, ks = starts(t)
        v_t = vt_ref[:, pl.ds(ks, blk)]
        for mp in range(2):
            _softmax_pv_update(u_refs[par][mp], bm_refs[par][mp], far_bias * LOG2E, v_t,
                               m_ref.at[mp], l_ref.at[mp], acc_ref.at[mp], qs)

    _run_pipeline(kinds, stage_a, stage_b)

    lam = (jnp.exp(jnp.sum(lq1_ref[...] * lk1_ref[...], axis=-1, keepdims=True))
           - jnp.exp(jnp.sum(lq2_ref[...] * lk2_ref[...], axis=-1, keepdims=True))
           + LAM_INIT)
    for c in range(0, o_ref.shape[0], blk):
        cols = slice(c, c + blk)
        o_t = (acc_ref[0, :, cols] * (1.0 / l_ref[0, :, cols])
               - lam * (acc_ref[1, :, cols] * (1.0 / l_ref[1, :, cols])))
        y_t = o_t * lax.rsqrt(jnp.mean(o_t * o_t, axis=0, keepdims=True) + EPS)
        y_t = (y_t * g_ref[...]) * (1.0 - LAM_INIT)
        o_ref[cols, :] = y_t.T.astype(o_ref.dtype)


def _diff_attention(qkv, tab, lq1, lk1, lq2, lk2, g_col):
    bsz, seq, _ = qkv.shape
    blk = ATTN_BLOCK
    w = 2 * HEAD_DIM
    k_off = DIFF_WIDTH // w
    v_off = 2 * DIFF_WIDTH // w
    kinds, pairs = _pair_schedule(seq // blk, with_near=True)
    smem = pl.BlockSpec(memory_space=pltpu.SMEM)
    vec = pl.BlockSpec((1, HEAD_DIM), lambda h, b: (0, 0))
    return pl.pallas_call(
        functools.partial(_diff_attn_kernel, kinds),
        grid=(N_DIFF_HEADS, bsz),
        in_specs=[smem, smem, vec, vec, vec, vec,
                  pl.BlockSpec((None, seq, w), lambda h, b: (b, 0, h)),
                  pl.BlockSpec((None, seq, w), lambda h, b: (b, 0, k_off + h)),
                  pl.BlockSpec((None, seq, w), lambda h, b: (b, 0, v_off + h)),
                  pl.BlockSpec((w, 1), lambda h, b: (0, 0))],
        out_specs=pl.BlockSpec((None, seq, w), lambda h, b: (b, 0, h)),
        out_shape=jax.ShapeDtypeStruct((bsz, seq, MIX_WIDTH), jnp.bfloat16),
        scratch_shapes=[pltpu.VMEM((blk, blk), jnp.float32),
                        pltpu.VMEM((blk, blk), jnp.float32),
                        pltpu.VMEM((w, seq), jnp.bfloat16),
                        pltpu.VMEM((2, blk, blk), jnp.float32),
                        pltpu.VMEM((2, blk, blk), jnp.float32),
                        pltpu.VMEM((2, 1, blk), jnp.float32),
                        pltpu.VMEM((2, 1, blk), jnp.float32),
                        pltpu.VMEM((2, 1, seq), jnp.float32),
                        pltpu.VMEM((2, 1, seq), jnp.float32),
                        pltpu.VMEM((2, w, seq), jnp.float32)],
        compiler_params=_params(("arbitrary", "arbitrary")),
        name="diff_attention",
    )(jnp.asarray(pairs), tab, lq1, lk1, lq2, lk2, qkv, qkv, qkv, g_col)


def _fox_attn_kernel(kinds, pair_ref, q_ref, k_ref, v_ref, cum_ref, mixed_hbm_ref, o_ref,
                     vt_ref, ckb_ref, u0_ref, u1_ref, bm0_ref, bm1_ref, m_ref, l_ref, acc_ref):
    del mixed_hbm_ref
    blk = ATTN_BLOCK
    n_rep = blk // LANES
    key = lax.broadcasted_iota(jnp.int32, (blk, blk), 0)
    qry = lax.broadcasted_iota(jnp.int32, (blk, blk), 1)
    u_refs, bm_refs = (u0_ref, u1_ref), (bm0_ref, bm1_ref)

    _store_transposed(v_ref, vt_ref)
    for c in range(0, ckb_ref.shape[0], blk):
        row = cum_ref[:, c:c + blk] * LOG2E
        ckb_ref[c:c + blk, :] = jnp.broadcast_to(row, (LANES, blk)).T
    m_ref[...] = jnp.full(m_ref.shape, NEG_INF, jnp.float32)
    l_ref[...] = jnp.zeros(l_ref.shape, jnp.float32)
    acc_ref[...] = jnp.zeros(acc_ref.shape, jnp.float32)

    def starts(t):
        return (pl.multiple_of(pair_ref[0, t] * blk, blk), pl.multiple_of(pair_ref[1, t] * blk, blk))

    def stage_a(t, kind, par):
        qs, ks = starts(t)
        ck = ckb_ref[pl.ds(ks, blk), :]
        u = (_scores_t(k_ref[pl.ds(ks, blk), :], q_ref[pl.ds(qs, blk), :]) * (SCALE * LOG2E)
             - jnp.concatenate([ck] * n_rep, axis=1))
        if kind == DIAG:
            u = jnp.where(qry >= key, u, NEG_INF)
        u_refs[par][...] = u
        bm_refs[par][...] = jnp.max(u, axis=0, keepdims=True)

    def stage_b(t, par):
        qs, ks = starts(t)
        cq = cum_ref[:, pl.ds(qs, blk)] * LOG2E
        _softmax_pv_update(u_refs[par][...], bm_refs[par][...], cq, vt_ref[:, pl.ds(ks, blk)],
                           m_ref, l_ref, acc_ref, qs)

    _run_pipeline(kinds, stage_a, stage_b)

    for c in range(0, o_ref.shape[0], blk):
        cols = slice(c, c + blk)
        o_t = acc_ref[:, cols] * (1.0 / l_ref[:, cols])
        o_ref[cols, :] = o_t.T.astype(o_ref.dtype)


def _fox_attention(qkv, cum, mixed):
    bsz, seq, _ = qkv.shape
    blk = ATTN_BLOCK
    q_off = 3 * DIFF_WIDTH // HEAD_DIM
    k_off = q_off + N_FOX_HEADS
    v_off = k_off + N_FOX_HEADS
    o_off = DIFF_WIDTH // HEAD_DIM
    kinds, pairs = _pair_schedule(seq // blk, with_near=False)
    return pl.pallas_call(
        functools.partial(_fox_attn_kernel, kinds),
        grid=(N_FOX_HEADS, bsz),
        in_specs=[pl.BlockSpec(memory_space=pltpu.SMEM),
                  pl.BlockSpec((None, seq, HEAD_DIM), lambda h, b: (b, 0, q_off + h)),
                  pl.BlockSpec((None, seq, HEAD_DIM), lambda h, b: (b, 0, k_off + h)),
                  pl.BlockSpec((None, seq, HEAD_DIM), lambda h, b: (b, 0, v_off + h)),
                  pl.BlockSpec((None, None, 1, seq), lambda h, b: (h, b, 0, 0)),
                  pl.BlockSpec(memory_space=pl.ANY)],
        out_specs=pl.BlockSpec((None, seq, HEAD_DIM), lambda h, b: (b, 0, o_off + h)),
        out_shape=jax.ShapeDtypeStruct(mixed.shape, mixed.dtype),
        scratch_shapes=[pltpu.VMEM((HEAD_DIM, seq), jnp.bfloat16),
                        pltpu.VMEM((seq, LANES), jnp.float32),
                        pltpu.VMEM((blk, blk), jnp.float32),
                        pltpu.VMEM((blk, blk), jnp.float32),
                        pltpu.VMEM((1, blk), jnp.float32),
                        pltpu.VMEM((1, blk), jnp.float32),
                        pltpu.VMEM((1, seq), jnp.float32),
                        pltpu.VMEM((1, seq), jnp.float32),
                        pltpu.VMEM((HEAD_DIM, seq), jnp.float32)],
        input_output_aliases={5: 0},
        compiler_params=_params(("arbitrary", "arbitrary")),
        name="fox_attention",
    )(jnp.asarray(pairs), qkv, qkv, qkv, cum, mixed)


def _proj_residual_kernel(a_ref, w_ref, res_ref, g_ref, *out_refs, emit_sum, n_k, k_last):
    k = pl.program_id(1)
    acc_ref = out_refs[0]
    bk = a_ref.shape[1]

    @pl.when(k == 0)
    def _init():
        acc_ref[...] = res_ref[...]

    def accumulate(k_len):
        a = a_ref[:, :k_len]
        for c in range(0, acc_ref.shape[1], PROJ_COL_CHUNK):
            cols = slice(c, c + PROJ_COL_CHUNK)
            acc_ref[:, cols] += jnp.dot(a, w_ref[:k_len, cols], preferred_element_type=jnp.float32)

    if k_last == bk:
        accumulate(bk)
    else:
        pl.when(k < n_k - 1)(lambda: accumulate(bk))

    @pl.when(k == n_k - 1)
    def _finish():
        if k_last != bk:
            accumulate(k_last)
        for r in range(0, acc_ref.shape[0], PROJ_ROW_CHUNK):
            rows = slice(r, r + PROJ_ROW_CHUNK)
            x = acc_ref[rows, :]
            y = x * lax.rsqrt(jnp.mean(x * x, axis=-1, keepdims=True) + EPS) * g_ref[...]
            if emit_sum:
                out_refs[1][rows, :] = y.astype(out_refs[1].dtype)
            else:
                acc_ref[rows, :] = y


def _proj_residual_norm(a, w, res, g, *, emit_sum, bm, bk, name):
    m, kdim = a.shape
    d = w.shape[1]
    n_k = pl.cdiv(kdim, bk)
    k_last = kdim - (n_k - 1) * bk
    row_block = pl.BlockSpec((bm, d), lambda i, k: (i, 0))
    if emit_sum:
        out_specs = [row_block, row_block]
        out_shape = [jax.ShapeDtypeStruct((m, d), jnp.float32),
                     jax.ShapeDtypeStruct((m, d), jnp.bfloat16)]
    else:
        out_specs = [row_block]
        out_shape = [jax.ShapeDtypeStruct((m, d), jnp.float32)]
    return pl.pallas_call(
        functools.partial(_proj_residual_kernel, emit_sum=emit_sum, n_k=n_k, k_last=k_last),
        grid=(m // bm, n_k),
        in_specs=[pl.BlockSpec((bm, bk), lambda i, k: (i, k)),
                  pl.BlockSpec((bk, d), lambda i, k: (k, 0)),
                  row_block,
                  pl.BlockSpec((1, d), lambda i, k: (0, 0))],
        out_specs=out_specs,
        out_shape=out_shape,
        compiler_params=_params(("arbitrary", "arbitrary")),
        name=name,
    )(a, w, res, g)


def _gate_up_kernel(h_ref, wg_ref, wu_ref, wd_ref, o_ref, wd_out_ref):
    h = h_ref[...]
    gate = jnp.dot(h, wg_ref[...].astype(h.dtype), preferred_element_type=jnp.float32)
    up = jnp.dot(h, wu_ref[...].astype(h.dtype), preferred_element_type=jnp.float32)
    o_ref[...] = (gate * jax.nn.sigmoid(gate) * up).astype(o_ref.dtype)
    wd_out_ref[...] = wd_ref[...].astype(wd_out_ref.dtype)


def _gate_up(h, wg, wu, wd, bm=ROW_TILE, bn=FF_TILE):
    m, k = h.shape
    n = wg.shape[1]
    n_j = n // bn
    slab = wd.shape[0] // ((m // bm) * n_j)
    assert slab * (m // bm) * n_j == wd.shape[0] and slab % 16 == 0
    return pl.pallas_call(
        _gate_up_kernel,
        grid=(m // bm, n_j),
        in_specs=[pl.BlockSpec((bm, k), lambda i, j: (i, 0), pipeline_mode=pl.Buffered(1)),
                  pl.BlockSpec((k, bn), lambda i, j: (0, j)),
                  pl.BlockSpec((k, bn), lambda i, j: (0, j)),
                  pl.BlockSpec((slab, wd.shape[1]), lambda i, j: (i * n_j + j, 0))],
        out_specs=[pl.BlockSpec((bm, bn), lambda i, j: (i, j)),
                   pl.BlockSpec((slab, wd.shape[1]), lambda i, j: (i * n_j + j, 0))],
        out_shape=[jax.ShapeDtypeStruct((m, n), jnp.bfloat16),
                   jax.ShapeDtypeStruct(wd.shape, jnp.bfloat16)],
        compiler_params=_params(("arbitrary", "arbitrary")),
        name="ffn_gate_up",
    )(h, wg, wu, wd)


def kernel(x, attn_norm_g, w_in, b_f, lambda_q1, lambda_k1, lambda_q2, lambda_k2, rel_bias_table,
           diff_subln_g, w_o, ffn_norm_g, w_gate, w_up, w_down, final_norm_g):
    bsz, seq, d = x.shape
    m = bsz * seq
    bf16 = jnp.bfloat16
    x2d = x.reshape(m, d)

    wf_t = w_in[0, :, QKV_COLS:].T.astype(bf16)
    w_o_b = w_o[0].astype(bf16)

    h, fl_t = _norm_and_gate_logits(x2d, attn_norm_g[0].reshape(1, d), wf_t)
    qkv = _matmul(h, w_in[0], QKV_COLS, name="in_proj").reshape(bsz, seq, QKV_COLS)

    cum = _cum_log_forget(fl_t, b_f[0].reshape(N_FOX_HEADS, 1), seq)
    cum = cum.reshape(N_FOX_HEADS, bsz, 1, seq)

    mixed = _diff_attention(qkv, rel_bias_table,
                            lambda_q1[0].reshape(1, HEAD_DIM), lambda_k1[0].reshape(1, HEAD_DIM),
                            lambda_q2[0].reshape(1, HEAD_DIM), lambda_k2[0].reshape(1, HEAD_DIM),
                            diff_subln_g[0].reshape(2 * HEAD_DIM, 1))
    mixed = _fox_attention(qkv, cum, mixed)

    x1, h2 = _proj_residual_norm(mixed.reshape(m, MIX_WIDTH), w_o_b, x2d,
                                 ffn_norm_g[0].reshape(1, d),
                                 emit_sum=True, bm=512, bk=512, name="out_proj")
    act, w_down_b = _gate_up(h2, w_gate[0], w_up[0], w_down[0])
    (out,) = _proj_residual_norm(act, w_down_b, x1, final_norm_g.reshape(1, d),
                                 emit_sum=False, bm=512, bk=DOWN_K_TILE, name="ffn_down")
    return out.reshape(bsz, seq, d)
```

```python
import functools
import math

import numpy as np

import jax
import jax.numpy as jnp
from jax import lax
from jax.experimental import pallas as pl
from jax.experimental.pallas import tpu as pltpu

D_MODEL = 4096
HEAD_DIM = 128
N_DIFF_HEADS = D_MODEL // (4 * HEAD_DIM)
N_FOX_HEADS = D_MODEL // (2 * HEAD_DIM)
DIFF_WIDTH = N_DIFF_HEADS * 2 * HEAD_DIM
FOX_WIDTH = N_FOX_HEADS * HEAD_DIM
MIX_WIDTH = DIFF_WIDTH + FOX_WIDTH
QKV_COLS = 3 * DIFF_WIDTH + 3 * FOX_WIDTH
D_FF = ((8 * D_MODEL + 3 * 256 - 1) // (3 * 256)) * 256
NUM_BUCKETS = 32
MAX_DISTANCE = 128
EPS = 1e-6
NEG_INF = -1e30
LAM_INIT = 0.8 - 0.6 * math.exp(-0.3 * 0)
SCALE = HEAD_DIM ** -0.5
LOG2E = math.log2(math.e)

LANES = 128
ROW_TILE = 2048
FF_TILE = 256
DOWN_K_TILE = 1024
VMEM_LIMIT = 56 * 1024 * 1024

ATTN_BLOCK = 512
MATMUL_ROW_CHUNK = 512
PROJ_COL_CHUNK = 1024
PROJ_ROW_CHUNK = 128

FAR, NEAR, DIAG = "far", "near", "diag"


def _params(sem, vmem=VMEM_LIMIT):
    return pltpu.CompilerParams(dimension_semantics=sem, vmem_limit_bytes=vmem)


def _norm_kernel(x_ref, g_ref, wf_ref, h_ref, fl_ref):
    x = x_ref[...]
    y = x * lax.rsqrt(jnp.mean(x * x, axis=-1, keepdims=True) + EPS)
    h = (y * g_ref[...]).astype(jnp.bfloat16)
    h_ref[...] = h
    fl_ref[...] = lax.dot_general(wf_ref[...], h, (((1,), (1,)), ((), ())),
                                  preferred_element_type=jnp.float32)


def _norm_and_gate_logits(x2d, g, wf_t, bm=512):
    m, d = x2d.shape
    nh = wf_t.shape[0]
    return pl.pallas_call(
        _norm_kernel,
        grid=(m // bm,),
        in_specs=[pl.BlockSpec((bm, d), lambda i: (i, 0)),
                  pl.BlockSpec((1, d), lambda i: (0, 0)),
                  pl.BlockSpec((nh, d), lambda i: (0, 0))],
        out_specs=[pl.BlockSpec((bm, d), lambda i: (i, 0)),
                   pl.BlockSpec((nh, bm), lambda i: (0, i))],
        out_shape=[jax.ShapeDtypeStruct((m, d), jnp.bfloat16),
                   jax.ShapeDtypeStruct((nh, m), jnp.float32)],
        compiler_params=_params(("arbitrary",)),
        name="attn_norm",
    )(x2d, g, wf_t)


def _matmul_kernel(a_ref, w_ref, o_ref):
    o_ref[...] = jnp.dot(a_ref[...], w_ref[...].astype(a_ref.dtype),
                         preferred_element_type=jnp.float32).astype(o_ref.dtype)


def _matmul(a, w, n_out, bm=ROW_TILE, bn=512, name="matmul"):
    m, k = a.shape
    return pl.pallas_call(
        _matmul_kernel,
        grid=(m // bm, n_out // bn),
        in_specs=[pl.BlockSpec((bm, k), lambda i, j: (i, 0), pipeline_mode=pl.Buffered(1)),
                  pl.BlockSpec((None, k, bn), lambda i, j: (0, 0, j))],
        out_specs=pl.BlockSpec((bm, bn), lambda i, j: (i, j)),
        out_shape=jax.ShapeDtypeStruct((m, n_out), jnp.bfloat16),
        compiler_params=_params(("arbitrary", "arbitrary")),
        name=name,
    )(a, w)


def _cumsum_kernel(fl_ref, bf_ref, c_ref):
    z = fl_ref[...] + bf_ref[...]
    x = jnp.minimum(z, 0.0) - jnp.log(1.0 + jnp.exp(-jnp.abs(z)))
    n = x.shape[-1]
    pos = lax.broadcasted_iota(jnp.int32, x.shape, 1)
    shift = 1
    while shift < n:
        x = x + jnp.where(pos >= shift, pltpu.roll(x, shift, 1), 0.0)
        shift *= 2
    c_ref[...] = x


def _cum_log_forget(fl_t, b_f, seq):
    nh, m = fl_t.shape
    return pl.pallas_call(
        _cumsum_kernel,
        grid=(m // seq,),
        in_specs=[pl.BlockSpec((nh, seq), lambda b: (0, b)),
                  pl.BlockSpec((nh, 1), lambda b: (0, 0))],
        out_specs=pl.BlockSpec((nh, seq), lambda b: (0, b)),
        out_shape=jax.ShapeDtypeStruct((nh, m), jnp.float32),
        compiler_params=_params(("arbitrary",)),
        name="cum_log_forget",
    )(fl_t, b_f)


def _scores_t(k, q):
    return lax.dot_general(k, q, (((1,), (1,)), ((), ())), preferred_element_type=jnp.float32)


def _softmax_pv_update(u, blk_max, shift, v_t, m_ref, l_ref, acc_ref, qs):
    blk = u.shape[1]
    m_old = m_ref[:, pl.ds(qs, blk)]
    m_new = jnp.maximum(m_old, blk_max + shift)
    alpha = jnp.exp2(m_old - m_new)
    p = jnp.exp2(u + (shift - m_new))
    l_ref[:, pl.ds(qs, blk)] = alpha * l_ref[:, pl.ds(qs, blk)] + jnp.sum(p, axis=0, keepdims=True)
    acc_ref[:, pl.ds(qs, blk)] = alpha * acc_ref[:, pl.ds(qs, blk)] + jnp.dot(
        v_t, p.astype(v_t.dtype), preferred_element_type=jnp.float32)
    m_ref[:, pl.ds(qs, blk)] = m_new


def _pair_schedule(n_blocks, with_near):
    far, near, diag = [], [], []
    for i in range(n_blocks):
        for j in range(i + 1):
            if j == i:
                diag.append((i, j))
            elif with_near and j == i - 1:
                near.append((i, j))
            else:
                far.append((i, j))
    pairs = far + near + diag
    kinds = [FAR] * len(far) + [NEAR] * len(near) + [DIAG] * len(diag)
    return kinds, np.asarray(pairs, np.int32).T.copy()


def _run_pipeline(kinds, stage_a, stage_b):
    n = len(kinds)
    next_kind = list(kinds[1:]) + [None]
    stage_a(0, kinds[0], 0)
    s = 0
    while s < n:
        e = s
        while e < n and next_kind[e] == next_kind[s]:
            e += 1
        kind, par, n_double = next_kind[s], s % 2, (e - s) // 2

        def one(step, parity):
            if kind is not None:
                stage_a(step + 1, kind, 1 - parity)
            stage_b(step, parity)

        if n_double:
            def body(r, carry):
                base = s + 2 * r
                one(base, par)
                one(base + 1, 1 - par)
                return carry

            lax.fori_loop(0, n_double, body, 0)
        if (e - s) % 2:
            one(e - 1, (e - 1) % 2)
        s = e


def _store_transposed(src_ref, dst_ref):
    for c in range(0, src_ref.shape[0], ATTN_BLOCK):
        dst_ref[:, c:c + ATTN_BLOCK] = (
            src_ref[c:c + ATTN_BLOCK, :].astype(jnp.float32).T.astype(dst_ref.dtype))


def _t5_bucket(n):
    max_exact = NUM_BUCKETS // 2
    nf = jnp.maximum(n, 1).astype(jnp.float32)
    large = max_exact + (jnp.log(nf / max_exact) / math.log(MAX_DISTANCE / max_exact)
                         * (NUM_BUCKETS - max_exact)).astype(jnp.int32)
    large = jnp.minimum(large, NUM_BUCKETS - 1)
    return jnp.where(n < max_exact, n, large)


def _diff_attn_kernel(kinds, pair_ref, tab_ref, lq1_ref, lk1_ref, lq2_ref, lk2_ref,
                      q_ref, k_ref, v_ref, g_ref, o_ref,
                      bias_near_ref, bias_diag_ref, vt_ref, u0_ref, u1_ref, bm0_ref, bm1_ref,
                      m_ref, l_ref, acc_ref):
    h = pl.program_id(0)
    b = pl.program_id(1)
    blk = ATTN_BLOCK
    key = lax.broadcasted_iota(jnp.int32, (blk, blk), 0)
    qry = lax.broadcasted_iota(jnp.int32, (blk, blk), 1)
    u_refs, bm_refs = (u0_ref, u1_ref), (bm0_ref, bm1_ref)
    far_bias = tab_ref[NUM_BUCKETS - 1, h]

    @pl.when(b == 0)
    def _build_bias():
        for ref, offset in ((bias_diag_ref, 0), (bias_near_ref, blk)):
            bucket = _t5_bucket(jnp.maximum(qry - key + offset, 0))
            bias = jnp.zeros((blk, blk), jnp.float32)
            for t in range(NUM_BUCKETS):
                bias = jnp.where(bucket == t, tab_ref[t, h], bias)
            ref[...] = (bias - far_bias) * LOG2E

    _store_transposed(v_ref, vt_ref)
    m_ref[...] = jnp.full(m_ref.shape, NEG_INF, jnp.float32)
    l_ref[...] = jnp.zeros(l_ref.shape, jnp.float32)
    acc_ref[...] = jnp.zeros(acc_ref.shape, jnp.float32)

    def starts(t):
        return (pl.multiple_of(pair_ref[0, t] * blk, blk), pl.multiple_of(pair_ref[1, t] * blk, blk))

    def stage_a(t, kind, par):
        qs, ks = starts(t)
        for mp in range(2):
            cols = slice(mp * HEAD_DIM, (mp + 1) * HEAD_DIM)
            u = _scores_t(k_ref[pl.ds(ks, blk), cols], q_ref[pl.ds(qs, blk), cols]) * (SCALE * LOG2E)
            if kind == NEAR:
                u = u + bias_near_ref[...]
            elif kind == DIAG:
                u = jnp.where(qry >= key, u + bias_diag_ref[...], NEG_INF)
            u_refs[par][mp] = u
            bm_refs[par][mp] = jnp.max(u, axis=0, keepdims=True)

    def stage_b(t, par):
        qs, ks = starts(t)
        v_t = vt_ref[:, pl.ds(ks, blk)]
        for mp in range(2):
            _softmax_pv_update(u_refs[par][mp], bm_refs[par][mp], far_bias * LOG2E, v_t,
                               m_ref.at[mp], l_ref.at[mp], acc_ref.at[mp], qs)

    _run_pipeline(kinds, stage_a, stage_b)

    lam = (jnp.exp(jnp.sum(lq1_ref[...] * lk1_ref[...], axis=-1, keepdims=True))
           - jnp.exp(jnp.sum(lq2_ref[...] * lk2_ref[...], axis=-1, keepdims=True))
           + LAM_INIT)
    for c in range(0, o_ref.shape[0], blk):
        cols = slice(c, c + blk)
        o_t = (acc_ref[0, :, cols] * (1.0 / l_ref[0, :, cols])
               - lam * (acc_ref[1, :, cols] * (1.0 / l_ref[1, :, cols])))
        y_t = o_t * lax.rsqrt(jnp.mean(o_t * o_t, axis=0, keepdims=True) + EPS)
        y_t = (y_t * g_ref[...]) * (1.0 - LAM_INIT)
        o_ref[cols, :] = y_t.T.astype(o_ref.dtype)


def _diff_attention(qkv, tab, lq1, lk1, lq2, lk2, g_col):
    bsz, seq, _ = qkv.shape
    blk = ATTN_BLOCK
    w = 2 * HEAD_DIM
    k_off = DIFF_WIDTH // w
    v_off = 2 * DIFF_WIDTH // w
    kinds, pairs = _pair_schedule(seq // blk, with_near=True)
    smem = pl.BlockSpec(memory_space=pltpu.SMEM)
    vec = pl.BlockSpec((1, HEAD_DIM), lambda h, b: (0, 0))
    return pl.pallas_call(
        functools.partial(_diff_attn_kernel, kinds),
        grid=(N_DIFF_HEADS, bsz),
        in_specs=[smem, smem, vec, vec, vec, vec,
                  pl.BlockSpec((None, seq, w), lambda h, b: (b, 0, h)),
                  pl.BlockSpec((None, seq, w), lambda h, b: (b, 0, k_off + h)),
                  pl.BlockSpec((None, seq, w), lambda h, b: (b, 0, v_off + h)),
                  pl.BlockSpec((w, 1), lambda h, b: (0, 0))],
        out_specs=pl.BlockSpec((None, seq, w), lambda h, b: (b, 0, h)),
        out_shape=jax.ShapeDtypeStruct((bsz, seq, MIX_WIDTH), jnp.bfloat16),
        scratch_shapes=[pltpu.VMEM((blk, blk), jnp.float32),
                        pltpu.VMEM((blk, blk), jnp.float32),
                        pltpu.VMEM((w, seq), jnp.bfloat16),
                        pltpu.VMEM((2, blk, blk), jnp.float32),
                        pltpu.VMEM((2, blk, blk), jnp.float32),
                        pltpu.VMEM((2, 1, blk), jnp.float32),
                        pltpu.VMEM((2, 1, blk), jnp.float32),
                        pltpu.VMEM((2, 1, seq), jnp.float32),
                        pltpu.VMEM((2, 1, seq), jnp.float32),
                        pltpu.VMEM((2, w, seq), jnp.float32)],
        compiler_params=_params(("arbitrary", "arbitrary")),
        name="diff_attention",
    )(jnp.asarray(pairs), tab, lq1, lk1, lq2, lk2, qkv, qkv, qkv, g_col)


def _fox_attn_kernel(kinds, pair_ref, q_ref, k_ref, v_ref, cum_ref, mixed_hbm_ref, o_ref,
                     vt_ref, ckb_ref, u0_ref, u1_ref, bm0_ref, bm1_ref, m_ref, l_ref, acc_ref):
    del mixed_hbm_ref
    blk = ATTN_BLOCK
    n_rep = blk // LANES
    key = lax.broadcasted_iota(jnp.int32, (blk, blk), 0)
    qry = lax.broadcasted_iota(jnp.int32, (blk, blk), 1)
    u_refs, bm_refs = (u0_ref, u1_ref), (bm0_ref, bm1_ref)

    _store_transposed(v_ref, vt_ref)
    for c in range(0, ckb_ref.shape[0], blk):
        row = cum_ref[:, c:c + blk] * LOG2E
        ckb_ref[c:c + blk, :] = jnp.broadcast_to(row, (LANES, blk)).T
    m_ref[...] = jnp.full(m_ref.shape, NEG_INF, jnp.float32)
    l_ref[...] = jnp.zeros(l_ref.shape, jnp.float32)
    acc_ref[...] = jnp.zeros(acc_ref.shape, jnp.float32)

    def starts(t):
        return (pl.multiple_of(pair_ref[0, t] * blk, blk), pl.multiple_of(pair_ref[1, t] * blk, blk))

    def stage_a(t, kind, par):
        qs, ks = starts(t)
        ck = ckb_ref[pl.ds(ks, blk), :]
        u = (_scores_t(k_ref[pl.ds(ks, blk), :], q_ref[pl.ds(qs, blk), :]) * (SCALE * LOG2E)
             - jnp.concatenate([ck] * n_rep, axis=1))
        if kind == DIAG:
            u = jnp.where(qry >= key, u, NEG_INF)
        u_refs[par][...] = u
        bm_refs[par][...] = jnp.max(u, axis=0, keepdims=True)

    def stage_b(t, par):
        qs, ks = starts(t)
        cq = cum_ref[:, pl.ds(qs, blk)] * LOG2E
        _softmax_pv_update(u_refs[par][...], bm_refs[par][...], cq, vt_ref[:, pl.ds(ks, blk)],
                           m_ref, l_ref, acc_ref, qs)

    _run_pipeline(kinds, stage_a, stage_b)

    for c in range(0, o_ref.shape[0], blk):
        cols = slice(c, c + blk)
        o_t = acc_ref[:, cols] * (1.0 / l_ref[:, cols])
        o_ref[cols, :] = o_t.T.astype(o_ref.dtype)


def _fox_attention(qkv, cum, mixed):
    bsz, seq, _ = qkv.shape
    blk = ATTN_BLOCK
    q_off = 3 * DIFF_WIDTH // HEAD_DIM
    k_off = q_off + N_FOX_HEADS
    v_off = k_off + N_FOX_HEADS
    o_off = DIFF_WIDTH // HEAD_DIM
    kinds, pairs = _pair_schedule(seq // blk, with_near=False)
    return pl.pallas_call(
        functools.partial(_fox_attn_kernel, kinds),
        grid=(N_FOX_HEADS, bsz),
        in_specs=[pl.BlockSpec(memory_space=pltpu.SMEM),
                  pl.BlockSpec((None, seq, HEAD_DIM), lambda h, b: (b, 0, q_off + h)),
                  pl.BlockSpec((None, seq, HEAD_DIM), lambda h, b: (b, 0, k_off + h)),
                  pl.BlockSpec((None, seq, HEAD_DIM), lambda h, b: (b, 0, v_off + h)),
                  pl.BlockSpec((None, None, 1, seq), lambda h, b: (h, b, 0, 0)),
                  pl.BlockSpec(memory_space=pl.ANY)],
        out_specs=pl.BlockSpec((None, seq, HEAD_DIM), lambda h, b: (b, 0, o_off + h)),
        out_shape=jax.ShapeDtypeStruct(mixed.shape, mixed.dtype),
        scratch_shapes=[pltpu.VMEM((HEAD_DIM, seq), jnp.bfloat16),
                        pltpu.VMEM((seq, LANES), jnp.float32),
                        pltpu.VMEM((blk, blk), jnp.float32),
                        pltpu.VMEM((blk, blk), jnp.float32),
                        pltpu.VMEM((1, blk), jnp.float32),
                        pltpu.VMEM((1, blk), jnp.float32),
                        pltpu.VMEM((1, seq), jnp.float32),
                        pltpu.VMEM((1, seq), jnp.float32),
                        pltpu.VMEM((HEAD_DIM, seq), jnp.float32)],
        input_output_aliases={5: 0},
        compiler_params=_params(("arbitrary", "arbitrary")),
        name="fox_attention",
    )(jnp.asarray(pairs), qkv, qkv, qkv, cum, mixed)


def _out_proj_kernel(a_ref, w_ref, res_ref, g_ref, x1_ref, xg_ref, ssq_ref):
    j = pl.program_id(1)
    w = w_ref[...].astype(a_ref.dtype)
    parts = []
    for c in range(0, a_ref.shape[0], MATMUL_ROW_CHUNK):
        rows = slice(c, c + MATMUL_ROW_CHUNK)
        y = res_ref[rows, :] + jnp.dot(a_ref[rows, :], w, preferred_element_type=jnp.float32)
        x1_ref[rows, :] = y
        xg_ref[rows, :] = (y * g_ref[...]).astype(xg_ref.dtype)
        sq = y * y
        part = sq[:, :LANES]
        for lc in range(LANES, sq.shape[1], LANES):
            part = part + sq[:, lc:lc + LANES]
        parts.append(part)
    part = jnp.concatenate(parts, axis=0)

    @pl.when(j == 0)
    def _first():
        ssq_ref[...] = part

    @pl.when(j > 0)
    def _rest():
        ssq_ref[...] += part


def _out_proj(a, w, res, g, bm=1024, bn=512):
    m, k = a.shape
    d = w.shape[-1]
    return pl.pallas_call(
        _out_proj_kernel,
        grid=(m // bm, d // bn),
        in_specs=[pl.BlockSpec((bm, k), lambda i, j: (i, 0)),
                  pl.BlockSpec((None, k, bn), lambda i, j: (0, 0, j)),
                  pl.BlockSpec((bm, bn), lambda i, j: (i, j)),
                  pl.BlockSpec((1, bn), lambda i, j: (0, j))],
        out_specs=[pl.BlockSpec((bm, bn), lambda i, j: (i, j)),
                   pl.BlockSpec((bm, bn), lambda i, j: (i, j)),
                   pl.BlockSpec((bm, LANES), lambda i, j: (i, 0))],
        out_shape=[jax.ShapeDtypeStruct((m, d), jnp.float32),
                   jax.ShapeDtypeStruct((m, d), jnp.bfloat16),
                   jax.ShapeDtypeStruct((m, LANES), jnp.float32)],
        compiler_params=_params(("arbitrary", "arbitrary")),
        name="out_proj",
    )(a, w, res, g)


def _down_kernel(a_ref, w_ref, res_ref, g_ref, o_ref, *, n_k, k_last):
    k = pl.program_id(1)
    bk = a_ref.shape[1]

    def accumulate(k_len, base_ref):
        a = a_ref[:, :k_len]
        for c in range(0, o_ref.shape[1], PROJ_COL_CHUNK):
            cols = slice(c, c + PROJ_COL_CHUNK)
            o_ref[:, cols] = base_ref[:, cols] + jnp.dot(a, w_ref[:k_len, cols],
                                                         preferred_element_type=jnp.float32)

    pl.when(k == 0)(lambda: accumulate(bk, res_ref))
    pl.when((k > 0) & (k < n_k - 1))(lambda: accumulate(bk, o_ref))

    @pl.when(k == n_k - 1)
    def _finish():
        accumulate(k_last, o_ref)
        for r in range(0, o_ref.shape[0], PROJ_ROW_CHUNK):
            rows = slice(r, r + PROJ_ROW_CHUNK)
            x = o_ref[rows, :]
            o_ref[rows, :] = x * lax.rsqrt(jnp.mean(x * x, axis=-1, keepdims=True) + EPS) * g_ref[...]


def _down_proj_norm(a, w, res, g, bm=512, bk=DOWN_K_TILE):
    m, kdim = a.shape
    d = w.shape[1]
    n_k = pl.cdiv(kdim, bk)
    assert n_k >= 2
    k_last = kdim - (n_k - 1) * bk
    row_block = pl.BlockSpec((bm, d), lambda i, k: (i, 0))
    return pl.pallas_call(
        functools.partial(_down_kernel, n_k=n_k, k_last=k_last),
        grid=(m // bm, n_k),
        in_specs=[pl.BlockSpec((bm, bk), lambda i, k: (i, k)),
                  pl.BlockSpec((bk, d), lambda i, k: (k, 0)),
                  row_block,
                  pl.BlockSpec((1, d), lambda i, k: (0, 0))],
        out_specs=row_block,
        out_shape=jax.ShapeDtypeStruct((m, d), jnp.float32),
        compiler_params=_params(("arbitrary", "arbitrary")),
        name="ffn_down",
    )(a, w, res, g)


def _gate_up_kernel(xg_ref, ssq_ref, wg_ref, wu_ref, wd_ref, o_ref, wd_out_ref, r_ref):
    @pl.when(pl.program_id(1) == 0)
    def _row_scale():
        mean_sq = jnp.sum(ssq_ref[...], axis=-1, keepdims=True) * (1.0 / xg_ref.shape[1])
        r_ref[...] = jnp.broadcast_to(lax.rsqrt(mean_sq + EPS), r_ref.shape)

    wg = wg_ref[...].astype(xg_ref.dtype)
    wu = wu_ref[...].astype(xg_ref.dtype)
    for c in range(0, xg_ref.shape[0], MATMUL_ROW_CHUNK):
        rows = slice(c, c + MATMUL_ROW_CHUNK)
        xg = xg_ref[rows, :]
        r = jnp.concatenate([r_ref[rows, :]] * (o_ref.shape[1] // LANES), axis=1)
        gate = jnp.dot(xg, wg, preferred_element_type=jnp.float32) * r
        up = jnp.dot(xg, wu, preferred_element_type=jnp.float32) * r
        o_ref[rows, :] = (gate * jax.nn.sigmoid(gate) * up).astype(o_ref.dtype)
    wd_out_ref[...] = wd_ref[...].astype(wd_out_ref.dtype)


def _gate_up(xg, ssq, wg, wu, wd, bm=ROW_TILE, bn=FF_TILE):
    m, k = xg.shape
    _, f_rows, d = wd.shape
    n = wg.shape[2]
    n_j = n // bn
    slab = f_rows // ((m // bm) * n_j)
    assert slab * (m // bm) * n_j == f_rows and slab % 16 == 0
    return pl.pallas_call(
        _gate_up_kernel,
        grid=(m // bm, n_j),
        in_specs=[pl.BlockSpec((bm, k), lambda i, j: (i, 0), pipeline_mode=pl.Buffered(1)),
                  pl.BlockSpec((bm, LANES), lambda i, j: (i, 0)),
                  pl.BlockSpec((None, k, bn), lambda i, j: (0, 0, j)),
                  pl.BlockSpec((None, k, bn), lambda i, j: (0, 0, j)),
                  pl.BlockSpec((None, slab, d), lambda i, j: (0, i * n_j + j, 0))],
        out_specs=[pl.BlockSpec((bm, bn), lambda i, j: (i, j)),
                   pl.BlockSpec((slab, d), lambda i, j: (i * n_j + j, 0))],
        out_shape=[jax.ShapeDtypeStruct((m, n), jnp.bfloat16),
                   jax.ShapeDtypeStruct((f_rows, d), jnp.bfloat16)],
        scratch_shapes=[pltpu.VMEM((bm, LANES), jnp.float32)],
        compiler_params=_params(("arbitrary", "arbitrary")),
        name="ffn_gate_up",
    )(xg, ssq, wg, wu, wd)


def kernel(x, attn_norm_g, w_in, b_f, lambda_q1, lambda_k1, lambda_q2, lambda_k2, rel_bias_table,
           diff_subln_g, w_o, ffn_norm_g, w_gate, w_up, w_down, final_norm_g):
    bsz, seq, d = x.shape
    m = bsz * seq
    bf16 = jnp.bfloat16
    x2d = x.reshape(m, d)

    wf_t = w_in[0, :, QKV_COLS:].T.astype(bf16)

    h, fl_t = _norm_and_gate_logits(x2d, attn_norm_g[0].reshape(1, d), wf_t)
    qkv = _matmul(h, w_in, QKV_COLS, name="in_proj").reshape(bsz, seq, QKV_COLS)

    cum = _cum_log_forget(fl_t, b_f[0].reshape(N_FOX_HEADS, 1), seq)
    cum = cum.reshape(N_FOX_HEADS, bsz, 1, seq)

    mixed = _diff_attention(qkv, rel_bias_table,
                            lambda_q1[0].reshape(1, HEAD_DIM), lambda_k1[0].reshape(1, HEAD_DIM),
                            lambda_q2[0].reshape(1, HEAD_DIM), lambda_k2[0].reshape(1, HEAD_DIM),
                            diff_subln_g[0].reshape(2 * HEAD_DIM, 1))
    mixed = _fox_attention(qkv, cum, mixed)

    x1, x1g, ssq = _out_proj(mixed.reshape(m, MIX_WIDTH), w_o, x2d, ffn_norm_g[0].reshape(1, d))
    act, w_down_b = _gate_up(x1g, ssq, w_gate, w_up, w_down)
    out = _down_proj_norm(act, w_down_b, x1, final_norm_g.reshape(1, d))
    return out.reshape(bsz, seq, d)
```

```python
import functools
import math

import numpy as np

import jax
import jax.numpy as jnp
from jax import lax
from jax.experimental import pallas as pl
from jax.experimental.pallas import tpu as pltpu

D_MODEL = 4096
HEAD_DIM = 128
N_DIFF_HEADS = D_MODEL // (4 * HEAD_DIM)
N_FOX_HEADS = D_MODEL // (2 * HEAD_DIM)
DIFF_WIDTH = N_DIFF_HEADS * 2 * HEAD_DIM
FOX_WIDTH = N_FOX_HEADS * HEAD_DIM
MIX_WIDTH = DIFF_WIDTH + FOX_WIDTH
QKV_COLS = 3 * DIFF_WIDTH + 3 * FOX_WIDTH
D_FF = ((8 * D_MODEL + 3 * 256 - 1) // (3 * 256)) * 256
NUM_BUCKETS = 32
MAX_DISTANCE = 128
EPS = 1e-6
NEG_INF = -1e30
LAM_INIT = 0.8 - 0.6 * math.exp(-0.3 * 0)
SCALE = HEAD_DIM ** -0.5
LOG2E = math.log2(math.e)

LANES = 128
ROW_TILE = 2048
FF_TILE = 256
DOWN_K_TILE = 1024
VMEM_LIMIT = 56 * 1024 * 1024

ATTN_BLOCK = 512
MATMUL_ROW_CHUNK = 512
PROJ_COL_CHUNK = 1024
PROJ_ROW_CHUNK = 128

FAR, NEAR, DIAG = "far", "near", "diag"


def _params(sem, vmem=VMEM_LIMIT):
    return pltpu.CompilerParams(dimension_semantics=sem, vmem_limit_bytes=vmem)


def _norm_kernel(x_ref, g_ref, wf_ref, h_ref, fl_ref):
    x = x_ref[...]
    y = x * lax.rsqrt(jnp.mean(x * x, axis=-1, keepdims=True) + EPS)
    h = (y * g_ref[...]).astype(jnp.bfloat16)
    h_ref[...] = h
    fl_ref[...] = lax.dot_general(wf_ref[...].astype(jnp.bfloat16), h, (((1,), (1,)), ((), ())),
                                  preferred_element_type=jnp.float32)


def _norm_and_gate_logits(x2d, g, w_in_t, gate_row, nh, bm=512):
    m, d = x2d.shape
    assert gate_row % nh == 0 and gate_row + nh == w_in_t.shape[1]
    return pl.pallas_call(
        _norm_kernel,
        grid=(m // bm,),
        in_specs=[pl.BlockSpec((bm, d), lambda i: (i, 0)),
                  pl.BlockSpec((1, d), lambda i: (0, 0)),
                  pl.BlockSpec((None, nh, d), lambda i: (0, gate_row // nh, 0))],
        out_specs=[pl.BlockSpec((bm, d), lambda i: (i, 0)),
                   pl.BlockSpec((nh, bm), lambda i: (0, i))],
        out_shape=[jax.ShapeDtypeStruct((m, d), jnp.bfloat16),
                   jax.ShapeDtypeStruct((nh, m), jnp.float32)],
        compiler_params=_params(("arbitrary",)),
        name="attn_norm",
    )(x2d, g, w_in_t)


def _matmul_kernel(a_ref, wt_ref, o_ref):
    o_ref[...] = lax.dot_general(a_ref[...], wt_ref[...].astype(a_ref.dtype),
                                 (((1,), (1,)), ((), ())),
                                 preferred_element_type=jnp.float32).astype(o_ref.dtype)


def _matmul(a, w_t, n_out, bm=ROW_TILE, bn=512, name="matmul"):
    m, k = a.shape
    return pl.pallas_call(
        _matmul_kernel,
        grid=(m // bm, n_out // bn),
        in_specs=[pl.BlockSpec((bm, k), lambda i, j: (i, 0), pipeline_mode=pl.Buffered(1)),
                  pl.BlockSpec((None, bn, k), lambda i, j: (0, j, 0))],
        out_specs=pl.BlockSpec((bm, bn), lambda i, j: (i, j)),
        out_shape=jax.ShapeDtypeStruct((m, n_out), jnp.bfloat16),
        compiler_params=_params(("arbitrary", "arbitrary")),
        name=name,
    )(a, w_t)


def _cumsum_kernel(fl_ref, bf_ref, c_ref):
    z = fl_ref[...] + bf_ref[...]
    x = jnp.minimum(z, 0.0) - jnp.log(1.0 + jnp.exp(-jnp.abs(z)))
    n = x.shape[-1]
    pos = lax.broadcasted_iota(jnp.int32, x.shape, 1)
    shift = 1
    while shift < n:
        x = x + jnp.where(pos >= shift, pltpu.roll(x, shift, 1), 0.0)
        shift *= 2
    c_ref[...] = x


def _cum_log_forget(fl_t, b_f, seq):
    nh, m = fl_t.shape
    return pl.pallas_call(
        _cumsum_kernel,
        grid=(m // seq,),
        in_specs=[pl.BlockSpec((nh, seq), lambda b: (0, b)),
                  pl.BlockSpec((nh, 1), lambda b: (0, 0))],
        out_specs=pl.BlockSpec((nh, seq), lambda b: (0, b)),
        out_shape=jax.ShapeDtypeStruct((nh, m), jnp.float32),
        compiler_params=_params(("arbitrary",)),
        name="cum_log_forget",
    )(fl_t, b_f)


def _scores_t(k, q):
    return lax.dot_general(k, q, (((1,), (1,)), ((), ())), preferred_element_type=jnp.float32)


def _softmax_pv_update(u, blk_max, shift, v_t, m_ref, l_ref, acc_ref, qs):
    blk = u.shape[1]
    m_old = m_ref[:, pl.ds(qs, blk)]
    m_new = jnp.maximum(m_old, blk_max + shift)
    alpha = jnp.exp2(m_old - m_new)
    p = jnp.exp2(u + (shift - m_new))
    l_ref[:, pl.ds(qs, blk)] = alpha * l_ref[:, pl.ds(qs, blk)] + jnp.sum(p, axis=0, keepdims=True)
    acc_ref[:, pl.ds(qs, blk)] = alpha * acc_ref[:, pl.ds(qs, blk)] + jnp.dot(
        v_t, p.astype(v_t.dtype), preferred_element_type=jnp.float32)
    m_ref[:, pl.ds(qs, blk)] = m_new


def _pair_schedule(n_blocks, with_near):
    far, near, diag = [], [], []
    for i in range(n_blocks):
        for j in range(i + 1):
            if j == i:
                diag.append((i, j))
            elif with_near and j == i - 1:
                near.append((i, j))
            else:
                far.append((i, j))
    pairs = far + near + diag
    kinds = [FAR] * len(far) + [NEAR] * len(near) + [DIAG] * len(diag)
    return kinds, np.asarray(pairs, np.int32).T.copy()


def _run_pipeline(kinds, stage_a, stage_b):
    n = len(kinds)
    next_kind = list(kinds[1:]) + [None]
    stage_a(0, kinds[0], 0)
    s = 0
    while s < n:
        e = s
        while e < n and next_kind[e] == next_kind[s]:
            e += 1
        kind, par, n_double = next_kind[s], s % 2, (e - s) // 2

        def one(step, parity):
            if kind is not None:
                stage_a(step + 1, kind, 1 - parity)
            stage_b(step, parity)

        if n_double:
            def body(r, carry):
                base = s + 2 * r
                one(base, par)
                one(base + 1, 1 - par)
                return carry

            lax.fori_loop(0, n_double, body, 0)
        if (e - s) % 2:
            one(e - 1, (e - 1) % 2)
        s = e


def _store_transposed(src_ref, dst_ref):
    for c in range(0, src_ref.shape[0], ATTN_BLOCK):
        dst_ref[:, c:c + ATTN_BLOCK] = (
            src_ref[c:c + ATTN_BLOCK, :].astype(jnp.float32).T.astype(dst_ref.dtype))


def _t5_bucket(n):
    max_exact = NUM_BUCKETS // 2
    nf = jnp.maximum(n, 1).astype(jnp.float32)
    large = max_exact + (jnp.log(nf / max_exact) / math.log(MAX_DISTANCE / max_exact)
                         * (NUM_BUCKETS - max_exact)).astype(jnp.int32)
    large = jnp.minimum(large, NUM_BUCKETS - 1)
    return jnp.where(n < max_exact, n, large)


def _diff_attn_kernel(kinds, pair_ref, tab_ref, lq1_ref, lk1_ref, lq2_ref, lk2_ref,
                      q_ref, k_ref, v_ref, g_ref, o_ref,
                      bias_near_ref, bias_diag_ref, vt_ref, u0_ref, u1_ref, bm0_ref, bm1_ref,
                      m_ref, l_ref, acc_ref):
    h = pl.program_id(0)
    b = pl.program_id(1)
    blk = ATTN_BLOCK
    key = lax.broadcasted_iota(jnp.int32, (blk, blk), 0)
    qry = lax.broadcasted_iota(jnp.int32, (blk, blk), 1)
    u_refs, bm_refs = (u0_ref, u1_ref), (bm0_ref, bm1_ref)
    far_bias = tab_ref[NUM_BUCKETS - 1, h]

    @pl.when(b == 0)
    def _build_bias():
        for ref, offset in ((bias_diag_ref, 0), (bias_near_ref, blk)):
            bucket = _t5_bucket(jnp.maximum(qry - key + offset, 0))
            bias = jnp.zeros((blk, blk), jnp.float32)
            for t in range(NUM_BUCKETS):
                bias = jnp.where(bucket == t, tab_ref[t, h], bias)
            ref[...] = (bias - far_bias) * LOG2E

    _store_transposed(v_ref, vt_ref)
    m_ref[...] = jnp.full(m_ref.shape, NEG_INF, jnp.float32)
    l_ref[...] = jnp.zeros(l_ref.shape, jnp.float32)
    acc_ref[...] = jnp.zeros(acc_ref.shape, jnp.float32)

    def starts(t):
        return (pl.multiple_of(pair_ref[0, t] * blk, blk), pl.multiple_of(pair_ref[1, t] * blk, blk))

    def stage_a(t, kind, par):
        qs, ks = starts(t)
        for mp in range(2):
            cols = slice(mp * HEAD_DIM, (mp + 1) * HEAD_DIM)
            u = _scores_t(k_ref[pl.ds(ks, blk), cols], q_ref[pl.ds(qs, blk), cols]) * (SCALE * LOG2E)
            if kind == NEAR:
                u = u + bias_near_ref[...]
            elif kind == DIAG:
                u = jnp.where(qry >= key, u + bias_diag_ref[...], NEG_INF)
            u_refs[par][mp] = u
            bm_refs[par][mp] = jnp.max(u, axis=0, keepdims=True)

    def stage_b(t, par):
        qs, ks = starts(t)
        v_t = vt_ref[:, pl.ds(ks, blk)]
        for mp in range(2):
            _softmax_pv_update(u_refs[par][mp], bm_refs[par][mp], far_bias * LOG2E, v_t,
                               m_ref.at[mp], l_ref.at[mp], acc_ref.at[mp], qs)

    _run_pipeline(kinds, stage_a, stage_b)

    lam = (jnp.exp(jnp.sum(lq1_ref[...] * lk1_ref[...], axis=-1, keepdims=True))
           - jnp.exp(jnp.sum(lq2_ref[...] * lk2_ref[...], axis=-1, keepdims=True))
           + LAM_INIT)
    for c in range(0, o_ref.shape[0], blk):
        cols = slice(c, c + blk)
        o_t = (acc_ref[0, :, cols] * (1.0 / l_ref[0, :, cols])
               - lam * (acc_ref[1, :, cols] * (1.0 / l_ref[1, :, cols])))
        y_t = o_t * lax.rsqrt(jnp.mean(o_t * o_t, axis=0, keepdims=True) + EPS)
        y_t = (y_t * g_ref[...]) * (1.0 - LAM_INIT)
        o_ref[cols, :] = y_t.T.astype(o_ref.dtype)


def _diff_attention(qkv, tab, lq1, lk1, lq2, lk2, g_col):
    bsz, seq, _ = qkv.shape
    blk = ATTN_BLOCK
    w = 2 * HEAD_DIM
    k_off = DIFF_WIDTH // w
    v_off = 2 * DIFF_WIDTH // w
    kinds, pairs = _pair_schedule(seq // blk, with_near=True)
    smem = pl.BlockSpec(memory_space=pltpu.SMEM)
    vec = pl.BlockSpec((1, HEAD_DIM), lambda h, b: (0, 0))
    return pl.pallas_call(
        functools.partial(_diff_attn_kernel, kinds),
        grid=(N_DIFF_HEADS, bsz),
        in_specs=[smem, smem, vec, vec, vec, vec,
                  pl.BlockSpec((None, seq, w), lambda h, b: (b, 0, h)),
                  pl.BlockSpec((None, seq, w), lambda h, b: (b, 0, k_off + h)),
                  pl.BlockSpec((None, seq, w), lambda h, b: (b, 0, v_off + h)),
                  pl.BlockSpec((w, 1), lambda h, b: (0, 0))],
        out_specs=pl.BlockSpec((None, seq, w), lambda h, b: (b, 0, h)),
        out_shape=jax.ShapeDtypeStruct((bsz, seq, DIFF_WIDTH), jnp.bfloat16),
        scratch_shapes=[pltpu.VMEM((blk, blk), jnp.float32),
                        pltpu.VMEM((blk, blk), jnp.float32),
                        pltpu.VMEM((w, seq), jnp.bfloat16),
                        pltpu.VMEM((2, blk, blk), jnp.float32),
                        pltpu.VMEM((2, blk, blk), jnp.float32),
                        pltpu.VMEM((2, 1, blk), jnp.float32),
                        pltpu.VMEM((2, 1, blk), jnp.float32),
                        pltpu.VMEM((2, 1, seq), jnp.float32),
                        pltpu.VMEM((2, 1, seq), jnp.float32),
                        pltpu.VMEM((2, w, seq), jnp.float32)],
        compiler_params=_params(("arbitrary", "arbitrary")),
        name="diff_attention",
    )(jnp.asarray(pairs), tab, lq1, lk1, lq2, lk2, qkv, qkv, qkv, g_col)


def _fox_attn_kernel(kinds, pair_ref, q_ref, k_ref, v_ref, cum_ref, o_ref,
                     vt_ref, ckb_ref, u0_ref, u1_ref, bm0_ref, bm1_ref, m_ref, l_ref, acc_ref):
    blk = ATTN_BLOCK
    n_rep = blk // LANES
    key = lax.broadcasted_iota(jnp.int32, (blk, blk), 0)
    qry = lax.broadcasted_iota(jnp.int32, (blk, blk), 1)
    u_refs, bm_refs = (u0_ref, u1_ref), (bm0_ref, bm1_ref)

    _store_transposed(v_ref, vt_ref)
    for c in range(0, ckb_ref.shape[0], blk):
        row = cum_ref[:, c:c + blk] * LOG2E
        ckb_ref[c:c + blk, :] = jnp.broadcast_to(row, (LANES, blk)).T
    m_ref[...] = jnp.full(m_ref.shape, NEG_INF, jnp.float32)
    l_ref[...] = jnp.zeros(l_ref.shape, jnp.float32)
    acc_ref[...] = jnp.zeros(acc_ref.shape, jnp.float32)

    def starts(t):
        return (pl.multiple_of(pair_ref[0, t] * blk, blk), pl.multiple_of(pair_ref[1, t] * blk, blk))

    def stage_a(t, kind, par):
        qs, ks = starts(t)
        ck = ckb_ref[pl.ds(ks, blk), :]
        u = (_scores_t(k_ref[pl.ds(ks, blk), :], q_ref[pl.ds(qs, blk), :]) * (SCALE * LOG2E)
             - jnp.concatenate([ck] * n_rep, axis=1))
        if kind == DIAG:
            u = jnp.where(qry >= key, u, NEG_INF)
        u_refs[par][...] = u
        bm_refs[par][...] = jnp.max(u, axis=0, keepdims=True)

    def stage_b(t, par):
        qs, ks = starts(t)
        cq = cum_ref[:, pl.ds(qs, blk)] * LOG2E
        _softmax_pv_update(u_refs[par][...], bm_refs[par][...], cq, vt_ref[:, pl.ds(ks, blk)],
                           m_ref, l_ref, acc_ref, qs)

    _run_pipeline(kinds, stage_a, stage_b)

    for c in range(0, o_ref.shape[0], blk):
        cols = slice(c, c + blk)
        o_t = acc_ref[:, cols] * (1.0 / l_ref[:, cols])
        o_ref[cols, :] = o_t.T.astype(o_ref.dtype)


def _fox_attention(qkv, cum):
    bsz, seq, _ = qkv.shape
    blk = ATTN_BLOCK
    q_off = 3 * DIFF_WIDTH // HEAD_DIM
    k_off = q_off + N_FOX_HEADS
    v_off = k_off + N_FOX_HEADS
    kinds, pairs = _pair_schedule(seq // blk, with_near=False)
    return pl.pallas_call(
        functools.partial(_fox_attn_kernel, kinds),
        grid=(N_FOX_HEADS, bsz),
        in_specs=[pl.BlockSpec(memory_space=pltpu.SMEM),
                  pl.BlockSpec((None, seq, HEAD_DIM), lambda h, b: (b, 0, q_off + h)),
                  pl.BlockSpec((None, seq, HEAD_DIM), lambda h, b: (b, 0, k_off + h)),
                  pl.BlockSpec((None, seq, HEAD_DIM), lambda h, b: (b, 0, v_off + h)),
                  pl.BlockSpec((None, None, 1, seq), lambda h, b: (h, b, 0, 0))],
        out_specs=pl.BlockSpec((None, seq, HEAD_DIM), lambda h, b: (b, 0, h)),
        out_shape=jax.ShapeDtypeStruct((bsz, seq, FOX_WIDTH), jnp.bfloat16),
        scratch_shapes=[pltpu.VMEM((HEAD_DIM, seq), jnp.bfloat16),
                        pltpu.VMEM((seq, LANES), jnp.float32),
                        pltpu.VMEM((blk, blk), jnp.float32),
                        pltpu.VMEM((blk, blk), jnp.float32),
                        pltpu.VMEM((1, blk), jnp.float32),
                        pltpu.VMEM((1, blk), jnp.float32),
                        pltpu.VMEM((1, seq), jnp.float32),
                        pltpu.VMEM((1, seq), jnp.float32),
                        pltpu.VMEM((HEAD_DIM, seq), jnp.float32)],
        compiler_params=_params(("arbitrary", "arbitrary")),
        name="fox_attention",
    )(jnp.asarray(pairs), qkv, qkv, qkv, cum)


def _out_proj_kernel(ad_ref, af_ref, w_ref, res_ref, g_ref, x1_ref, xg_ref, ssq_ref):
    j = pl.program_id(1)
    kd = ad_ref.shape[1]
    wd = w_ref[:kd, :].astype(ad_ref.dtype)
    wf = w_ref[kd:, :].astype(af_ref.dtype)
    parts = []
    for c in range(0, ad_ref.shape[0], MATMUL_ROW_CHUNK):
        rows = slice(c, c + MATMUL_ROW_CHUNK)
        y = (res_ref[rows, :] + jnp.dot(ad_ref[rows, :], wd, preferred_element_type=jnp.float32)
             + jnp.dot(af_ref[rows, :], wf, preferred_element_type=jnp.float32))
        x1_ref[rows, :] = y
        xg_ref[rows, :] = (y * g_ref[...]).astype(xg_ref.dtype)
        sq = y * y
        part = sq[:, :LANES]
        for lc in range(LANES, sq.shape[1], LANES):
            part = part + sq[:, lc:lc + LANES]
        parts.append(part)
    part = jnp.concatenate(parts, axis=0)

    @pl.when(j == 0)
    def _first():
        ssq_ref[...] = part

    @pl.when(j > 0)
    def _rest():
        ssq_ref[...] += part


def _out_proj(a_diff, a_fox, w, res, g, bm=1024, bn=512):
    m, kd = a_diff.shape
    kf = a_fox.shape[1]
    d = w.shape[-1]
    return pl.pallas_call(
        _out_proj_kernel,
        grid=(m // bm, d // bn),
        in_specs=[pl.BlockSpec((bm, kd), lambda i, j: (i, 0)),
                  pl.BlockSpec((bm, kf), lambda i, j: (i, 0)),
                  pl.BlockSpec((None, kd + kf, bn), lambda i, j: (0, 0, j)),
                  pl.BlockSpec((bm, bn), lambda i, j: (i, j)),
                  pl.BlockSpec((1, bn), lambda i, j: (0, j))],
        out_specs=[pl.BlockSpec((bm, bn), lambda i, j: (i, j)),
                   pl.BlockSpec((bm, bn), lambda i, j: (i, j)),
                   pl.BlockSpec((bm, LANES), lambda i, j: (i, 0))],
        out_shape=[jax.ShapeDtypeStruct((m, d), jnp.float32),
                   jax.ShapeDtypeStruct((m, d), jnp.bfloat16),
                   jax.ShapeDtypeStruct((m, LANES), jnp.float32)],
        compiler_params=_params(("arbitrary", "arbitrary")),
        name="out_proj",
    )(a_diff, a_fox, w, res, g)


def _down_kernel(a_ref, w_ref, res_ref, g_ref, o_ref, *, n_k, k_last):
    k = pl.program_id(1)
    bk = a_ref.shape[1]

    def accumulate(k_len, base_ref):
        a = a_ref[:, :k_len]
        for c in range(0, o_ref.shape[1], PROJ_COL_CHUNK):
            cols = slice(c, c + PROJ_COL_CHUNK)
            o_ref[:, cols] = base_ref[:, cols] + jnp.dot(a, w_ref[:k_len, cols],
                                                         preferred_element_type=jnp.float32)

    pl.when(k == 0)(lambda: accumulate(bk, res_ref))
    pl.when((k > 0) & (k < n_k - 1))(lambda: accumulate(bk, o_ref))

    @pl.when(k == n_k - 1)
    def _finish():
        accumulate(k_last, o_ref)
        for r in range(0, o_ref.shape[0], PROJ_ROW_CHUNK):
            rows = slice(r, r + PROJ_ROW_CHUNK)
            x = o_ref[rows, :]
            o_ref[rows, :] = x * lax.rsqrt(jnp.mean(x * x, axis=-1, keepdims=True) + EPS) * g_ref[...]


def _down_proj_norm(a, w, res, g, bm=512, bk=DOWN_K_TILE):
    m, kdim = a.shape
    d = w.shape[1]
    n_k = pl.cdiv(kdim, bk)
    assert n_k >= 2
    k_last = kdim - (n_k - 1) * bk
    row_block = pl.BlockSpec((bm, d), lambda i, k: (i, 0))
    return pl.pallas_call(
        functools.partial(_down_kernel, n_k=n_k, k_last=k_last),
        grid=(m // bm, n_k),
        in_specs=[pl.BlockSpec((bm, bk), lambda i, k: (i, k)),
                  pl.BlockSpec((bk, d), lambda i, k: (k, 0)),
                  row_block,
                  pl.BlockSpec((1, d), lambda i, k: (0, 0))],
        out_specs=row_block,
        out_shape=jax.ShapeDtypeStruct((m, d), jnp.float32),
        compiler_params=_params(("arbitrary", "arbitrary")),
        name="ffn_down",
    )(a, w, res, g)


def _gate_up_kernel(xg_ref, ssq_ref, wg_ref, wu_ref, wd_ref, o_ref, wd_out_ref, r_ref):
    @pl.when(pl.program_id(1) == 0)
    def _row_scale():
        mean_sq = jnp.sum(ssq_ref[...], axis=-1, keepdims=True) * (1.0 / xg_ref.shape[1])
        r_ref[...] = jnp.broadcast_to(lax.rsqrt(mean_sq + EPS), r_ref.shape)

    wg = wg_ref[...].astype(xg_ref.dtype)
    wu = wu_ref[...].astype(xg_ref.dtype)
    for c in range(0, xg_ref.shape[0], MATMUL_ROW_CHUNK):
        rows = slice(c, c + MATMUL_ROW_CHUNK)
        xg = xg_ref[rows, :]
        r = jnp.concatenate([r_ref[rows, :]] * (o_ref.shape[1] // LANES), axis=1)
        gate = jnp.dot(xg, wg, preferred_element_type=jnp.float32) * r
        up = jnp.dot(xg, wu, preferred_element_type=jnp.float32) * r
        o_ref[rows, :] = (gate * jax.nn.sigmoid(gate) * up).astype(o_ref.dtype)
    wd_out_ref[...] = wd_ref[...].astype(wd_out_ref.dtype)


def _gate_up(xg, ssq, wg, wu, wd, bm=ROW_TILE, bn=FF_TILE):
    m, k = xg.shape
    _, f_rows, d = wd.shape
    n = wg.shape[2]
    n_j = n // bn
    slab = f_rows // ((m // bm) * n_j)
    assert slab * (m // bm) * n_j == f_rows and slab % 16 == 0
    return pl.pallas_call(
        _gate_up_kernel,
        grid=(m // bm, n_j),
        in_specs=[pl.BlockSpec((bm, k), lambda i, j: (i, 0), pipeline_mode=pl.Buffered(1)),
                  pl.BlockSpec((bm, LANES), lambda i, j: (i, 0)),
                  pl.BlockSpec((None, k, bn), lambda i, j: (0, 0, j)),
                  pl.BlockSpec((None, k, bn), lambda i, j: (0, 0, j)),
                  pl.BlockSpec((None, slab, d), lambda i, j: (0, i * n_j + j, 0))],
        out_specs=[pl.BlockSpec((bm, bn), lambda i, j: (i, j)),
                   pl.BlockSpec((slab, d), lambda i, j: (i * n_j + j, 0))],
        out_shape=[jax.ShapeDtypeStruct((m, n), jnp.bfloat16),
                   jax.ShapeDtypeStruct((f_rows, d), jnp.bfloat16)],
        scratch_shapes=[pltpu.VMEM((bm, LANES), jnp.float32)],
        compiler_params=_params(("arbitrary", "arbitrary")),
        name="ffn_gate_up",
    )(xg, ssq, wg, wu, wd)


def kernel(x, attn_norm_g, w_in, b_f, lambda_q1, lambda_k1, lambda_q2, lambda_k2, rel_bias_table,
           diff_subln_g, w_o, ffn_norm_g, w_gate, w_up, w_down, final_norm_g):
    bsz, seq, d = x.shape
    m = bsz * seq
    x2d = x.reshape(m, d)

    w_in_t = jnp.swapaxes(w_in, 1, 2)
    h, fl_t = _norm_and_gate_logits(x2d, attn_norm_g[0].reshape(1, d), w_in_t, QKV_COLS, N_FOX_HEADS)
    qkv = _matmul(h, w_in_t, QKV_COLS, name="in_proj").reshape(bsz, seq, QKV_COLS)

    cum = _cum_log_forget(fl_t, b_f[0].reshape(N_FOX_HEADS, 1), seq)
    cum = cum.reshape(N_FOX_HEADS, bsz, 1, seq)

    o_diff = _diff_attention(qkv, rel_bias_table,
                             lambda_q1[0].reshape(1, HEAD_DIM), lambda_k1[0].reshape(1, HEAD_DIM),
                             lambda_q2[0].reshape(1, HEAD_DIM), lambda_k2[0].reshape(1, HEAD_DIM),
                             diff_subln_g[0].reshape(2 * HEAD_DIM, 1))
    o_fox = _fox_attention(qkv, cum)

    x1, x1g, ssq = _out_proj(o_diff.reshape(m, DIFF_WIDTH), o_fox.reshape(m, FOX_WIDTH), w_o, x2d,
                             ffn_norm_g[0].reshape(1, d))
    act, w_down_b = _gate_up(x1g, ssq, w_gate, w_up, w_down)
    out = _down_proj_norm(act, w_down_b, x1, final_norm_g.reshape(1, d))
    return out.reshape(bsz, seq, d)
```

```python
import functools
import math

import numpy as np

import jax
import jax.numpy as jnp
from jax import lax
from jax.experimental import pallas as pl
from jax.experimental.pallas import tpu as pltpu

D_MODEL = 4096
HEAD_DIM = 128
N_DIFF_HEADS = D_MODEL // (4 * HEAD_DIM)
N_FOX_HEADS = D_MODEL // (2 * HEAD_DIM)
DIFF_WIDTH = N_DIFF_HEADS * 2 * HEAD_DIM
FOX_WIDTH = N_FOX_HEADS * HEAD_DIM
MIX_WIDTH = DIFF_WIDTH + FOX_WIDTH
QKV_COLS = 3 * DIFF_WIDTH + 3 * FOX_WIDTH
D_FF = ((8 * D_MODEL + 3 * 256 - 1) // (3 * 256)) * 256
NUM_BUCKETS = 32
MAX_DISTANCE = 128
EPS = 1e-6
NEG_INF = -1e30
LAM_INIT = 0.8 - 0.6 * math.exp(-0.3 * 0)
SCALE = HEAD_DIM ** -0.5
LOG2E = math.log2(math.e)

LANES = 128
ROW_TILE = 2048
FF_TILE = 256
DOWN_K_TILE = 1024
VMEM_LIMIT = 56 * 1024 * 1024

ATTN_BLOCK = 512
PIPE_DEPTH = 2
PIPE_BUFS = PIPE_DEPTH + 1
FOX_SUM_ROWS = 16
MATMUL_ROW_CHUNK = 512
PROJ_COL_CHUNK = 1024
PROJ_ROW_CHUNK = 128

FAR, NEAR, DIAG = "far", "near", "diag"


def _params(sem, vmem=VMEM_LIMIT):
    return pltpu.CompilerParams(dimension_semantics=sem, vmem_limit_bytes=vmem)


def _norm_kernel(x_ref, g_ref, wf_ref, h_ref, fl_ref):
    x = x_ref[...]
    y = x * lax.rsqrt(jnp.mean(x * x, axis=-1, keepdims=True) + EPS)
    h = (y * g_ref[...]).astype(jnp.bfloat16)
    h_ref[...] = h
    fl_ref[...] = lax.dot_general(wf_ref[...].astype(jnp.bfloat16), h, (((1,), (1,)), ((), ())),
                                  preferred_element_type=jnp.float32)


def _norm_and_gate_logits(x2d, g, w_in_t, gate_row, nh, bm=512):
    m, d = x2d.shape
    assert gate_row % nh == 0 and gate_row + nh == w_in_t.shape[1]
    return pl.pallas_call(
        _norm_kernel,
        grid=(m // bm,),
        in_specs=[pl.BlockSpec((bm, d), lambda i: (i, 0)),
                  pl.BlockSpec((1, d), lambda i: (0, 0)),
                  pl.BlockSpec((None, nh, d), lambda i: (0, gate_row // nh, 0))],
        out_specs=[pl.BlockSpec((bm, d), lambda i: (i, 0)),
                   pl.BlockSpec((nh, bm), lambda i: (0, i))],
        out_shape=[jax.ShapeDtypeStruct((m, d), jnp.bfloat16),
                   jax.ShapeDtypeStruct((nh, m), jnp.float32)],
        compiler_params=_params(("arbitrary",)),
        name="attn_norm",
    )(x2d, g, w_in_t)


def _matmul_kernel(a_ref, wt_ref, o_ref):
    o_ref[...] = lax.dot_general(a_ref[...], wt_ref[...].astype(a_ref.dtype),
                                 (((1,), (1,)), ((), ())),
                                 preferred_element_type=jnp.float32).astype(o_ref.dtype)


def _matmul(a, w_t, n_out, bm=ROW_TILE, bn=512, name="matmul"):
    m, k = a.shape
    return pl.pallas_call(
        _matmul_kernel,
        grid=(m // bm, n_out // bn),
        in_specs=[pl.BlockSpec((bm, k), lambda i, j: (i, 0), pipeline_mode=pl.Buffered(1)),
                  pl.BlockSpec((None, bn, k), lambda i, j: (0, j, 0))],
        out_specs=pl.BlockSpec((bm, bn), lambda i, j: (i, j)),
        out_shape=jax.ShapeDtypeStruct((m, n_out), jnp.bfloat16),
        compiler_params=_params(("arbitrary", "arbitrary")),
        name=name,
    )(a, w_t)


def _cumsum_kernel(fl_ref, bf_ref, c_ref):
    z = fl_ref[...] + bf_ref[...]
    x = jnp.minimum(z, 0.0) - jnp.log(1.0 + jnp.exp(-jnp.abs(z)))
    n = x.shape[-1]
    pos = lax.broadcasted_iota(jnp.int32, x.shape, 1)
    shift = 1
    while shift < n:
        x = x + jnp.where(pos >= shift, pltpu.roll(x, shift, 1), 0.0)
        shift *= 2
    c_ref[...] = x


def _cum_log_forget(fl_t, b_f, seq):
    nh, m = fl_t.shape
    return pl.pallas_call(
        _cumsum_kernel,
        grid=(m // seq,),
        in_specs=[pl.BlockSpec((nh, seq), lambda b: (0, b)),
                  pl.BlockSpec((nh, 1), lambda b: (0, 0))],
        out_specs=pl.BlockSpec((nh, seq), lambda b: (0, b)),
        out_shape=jax.ShapeDtypeStruct((nh, m), jnp.float32),
        compiler_params=_params(("arbitrary",)),
        name="cum_log_forget",
    )(fl_t, b_f)


def _scores_t(k, q):
    return lax.dot_general(k, q, (((1,), (1,)), ((), ())), preferred_element_type=jnp.float32)


def _softmax_pv_update(u, blk_max, shift, v_t, m_ref, l_ref, acc_ref, qs):
    blk = u.shape[1]
    m_old = m_ref[:, pl.ds(qs, blk)]
    m_new = jnp.maximum(m_old, blk_max + shift)
    alpha = jnp.exp2(m_old - m_new)
    p = jnp.exp2(u + (shift - m_new))
    if l_ref is not None:
        l_ref[:, pl.ds(qs, blk)] = (alpha * l_ref[:, pl.ds(qs, blk)]
                                    + jnp.sum(p, axis=0, keepdims=True))
    acc_ref[:, pl.ds(qs, blk)] = alpha * acc_ref[:, pl.ds(qs, blk)] + jnp.dot(
        v_t, p.astype(v_t.dtype), preferred_element_type=jnp.float32)
    m_ref[:, pl.ds(qs, blk)] = m_new


def _pair_schedule(n_blocks, with_near):
    far, near, diag = [], [], []
    for i in range(n_blocks):
        for j in range(i + 1):
            if j == i:
                diag.append((i, j))
            elif with_near and j == i - 1:
                near.append((i, j))
            else:
                far.append((i, j))
    pairs = far + near + diag
    kinds = [FAR] * len(far) + [NEAR] * len(near) + [DIAG] * len(diag)
    return kinds, np.asarray(pairs, np.int32).T.copy()


def _run_pipeline(kinds, stage_a, stage_b):
    n = len(kinds)
    ahead_kind = list(kinds[PIPE_DEPTH:]) + [None] * PIPE_DEPTH
    for t in range(min(PIPE_DEPTH, n)):
        stage_a(t, kinds[t], t % PIPE_BUFS)
    s = 0
    while s < n:
        e = s
        while e < n and ahead_kind[e] == ahead_kind[s]:
            e += 1
        kind, n_loop = ahead_kind[s], (e - s) // PIPE_BUFS

        def one(step, buf):
            if kind is not None:
                stage_a(step + PIPE_DEPTH, kind, (buf + PIPE_DEPTH) % PIPE_BUFS)
            stage_b(step, buf)

        if n_loop:
            def body(r, carry):
                base = s + PIPE_BUFS * r
                for i in range(PIPE_BUFS):
                    one(base + i, (s + i) % PIPE_BUFS)
                return carry

            lax.fori_loop(0, n_loop, body, 0)
        for step in range(s + n_loop * PIPE_BUFS, e):
            one(step, step % PIPE_BUFS)
        s = e


def _store_transposed(src_ref, dst_ref):
    for c in range(0, src_ref.shape[0], ATTN_BLOCK):
        dst_ref[:, c:c + ATTN_BLOCK] = (
            src_ref[c:c + ATTN_BLOCK, :].astype(jnp.float32).T.astype(dst_ref.dtype))


def _t5_bucket(n):
    max_exact = NUM_BUCKETS // 2
    nf = jnp.maximum(n, 1).astype(jnp.float32)
    large = max_exact + (jnp.log(nf / max_exact) / math.log(MAX_DISTANCE / max_exact)
                         * (NUM_BUCKETS - max_exact)).astype(jnp.int32)
    large = jnp.minimum(large, NUM_BUCKETS - 1)
    return jnp.where(n < max_exact, n, large)


def _diff_attn_kernel(kinds, pair_ref, tab_ref, lq1_ref, lk1_ref, lq2_ref, lk2_ref,
                      q_ref, k_ref, v_ref, g_ref, o_ref,
                      bias_near_ref, bias_diag_ref, vt_ref, *tile_and_state_refs):
    h = pl.program_id(0)
    b = pl.program_id(1)
    blk = ATTN_BLOCK
    key = lax.broadcasted_iota(jnp.int32, (blk, blk), 0)
    qry = lax.broadcasted_iota(jnp.int32, (blk, blk), 1)
    u_refs = tile_and_state_refs[:PIPE_BUFS]
    bm_refs = tile_and_state_refs[PIPE_BUFS:2 * PIPE_BUFS]
    m_ref, l_ref, acc_ref = tile_and_state_refs[2 * PIPE_BUFS:]
    far_bias = tab_ref[NUM_BUCKETS - 1, h]

    @pl.when(b == 0)
    def _build_bias():
        for ref, offset in ((bias_diag_ref, 0), (bias_near_ref, blk)):
            bucket = _t5_bucket(jnp.maximum(qry - key + offset, 0))
            bias = jnp.zeros((blk, blk), jnp.float32)
            for t in range(NUM_BUCKETS):
                bias = jnp.where(bucket == t, tab_ref[t, h], bias)
            ref[...] = (bias - far_bias) * LOG2E

    _store_transposed(v_ref, vt_ref)
    m_ref[...] = jnp.full(m_ref.shape, NEG_INF, jnp.float32)
    l_ref[...] = jnp.zeros(l_ref.shape, jnp.float32)
    acc_ref[...] = jnp.zeros(acc_ref.shape, jnp.float32)

    def starts(t):
        return (pl.multiple_of(pair_ref[0, t] * blk, blk), pl.multiple_of(pair_ref[1, t] * blk, blk))

    def stage_a(t, kind, par):
        qs, ks = starts(t)
        for mp in range(2):
            cols = slice(mp * HEAD_DIM, (mp + 1) * HEAD_DIM)
            u = _scores_t(k_ref[pl.ds(ks, blk), cols], q_ref[pl.ds(qs, blk), cols]) * (SCALE * LOG2E)
            if kind == NEAR:
                u = u + bias_near_ref[...]
            elif kind == DIAG:
                u = jnp.where(qry >= key, u + bias_diag_ref[...], NEG_INF)
            u_refs[par][mp] = u
            bm_refs[par][mp] = jnp.max(u, axis=0, keepdims=True)

    def stage_b(t, par):
        qs, ks = starts(t)
        v_t = vt_ref[:, pl.ds(ks, blk)]
        for mp in range(2):
            _softmax_pv_update(u_refs[par][mp], bm_refs[par][mp], far_bias * LOG2E, v_t,
                               m_ref.at[mp], l_ref.at[mp], acc_ref.at[mp], qs)

    _run_pipeline(kinds, stage_a, stage_b)

    lam = (jnp.exp(jnp.sum(lq1_ref[...] * lk1_ref[...], axis=-1, keepdims=True))
           - jnp.exp(jnp.sum(lq2_ref[...] * lk2_ref[...], axis=-1, keepdims=True))
           + LAM_INIT)
    for c in range(0, o_ref.shape[0], blk):
        cols = slice(c, c + blk)
        o_t = (acc_ref[0, :, cols] * (1.0 / l_ref[0, :, cols])
               - lam * (acc_ref[1, :, cols] * (1.0 / l_ref[1, :, cols])))
        y_t = o_t * lax.rsqrt(jnp.mean(o_t * o_t, axis=0, keepdims=True) + EPS)
        y_t = (y_t * g_ref[...]) * (1.0 - LAM_INIT)
        o_ref[cols, :] = y_t.T.astype(o_ref.dtype)


def _diff_attention(qkv, tab, lq1, lk1, lq2, lk2, g_col):
    bsz, seq, _ = qkv.shape
    blk = ATTN_BLOCK
    w = 2 * HEAD_DIM
    k_off = DIFF_WIDTH // w
    v_off = 2 * DIFF_WIDTH // w
    kinds, pairs = _pair_schedule(seq // blk, with_near=True)
    smem = pl.BlockSpec(memory_space=pltpu.SMEM)
    vec = pl.BlockSpec((1, HEAD_DIM), lambda h, b: (0, 0))
    return pl.pallas_call(
        functools.partial(_diff_attn_kernel, kinds),
        grid=(N_DIFF_HEADS, bsz),
        in_specs=[smem, smem, vec, vec, vec, vec,
                  pl.BlockSpec((None, seq, w), lambda h, b: (b, 0, h)),
                  pl.BlockSpec((None, seq, w), lambda h, b: (b, 0, k_off + h)),
                  pl.BlockSpec((None, seq, w), lambda h, b: (b, 0, v_off + h)),
                  pl.BlockSpec((w, 1), lambda h, b: (0, 0))],
        out_specs=pl.BlockSpec((None, seq, w), lambda h, b: (b, 0, h)),
        out_shape=jax.ShapeDtypeStruct((bsz, seq, DIFF_WIDTH), jnp.bfloat16),
        scratch_shapes=([pltpu.VMEM((blk, blk), jnp.float32),
                         pltpu.VMEM((blk, blk), jnp.float32),
                         pltpu.VMEM((w, seq), jnp.bfloat16)]
                        + [pltpu.VMEM((2, blk, blk), jnp.float32)] * PIPE_BUFS
                        + [pltpu.VMEM((2, 1, blk), jnp.float32)] * PIPE_BUFS
                        + [pltpu.VMEM((2, 1, seq), jnp.float32),
                           pltpu.VMEM((2, 1, seq), jnp.float32),
                           pltpu.VMEM((2, w, seq), jnp.float32)]),
        compiler_params=_params(("arbitrary", "arbitrary")),
        name="diff_attention",
    )(jnp.asarray(pairs), tab, lq1, lk1, lq2, lk2, qkv, qkv, qkv, g_col)


def _fox_attn_kernel(kinds, pair_ref, q_ref, k_ref, v_ref, cum_ref, wo_ref, o_ref, wo_out_ref,
                     vt_ref, q2_ref, k2_ref, *tile_and_state_refs):
    blk = ATTN_BLOCK
    d = HEAD_DIM
    f32, bf16 = jnp.float32, jnp.bfloat16
    key = lax.broadcasted_iota(jnp.int32, (blk, blk), 0)
    qry = lax.broadcasted_iota(jnp.int32, (blk, blk), 1)
    lane = lax.broadcasted_iota(jnp.int32, (blk, LANES), 1)
    u_refs = tile_and_state_refs[:PIPE_BUFS]
    bm_refs = tile_and_state_refs[PIPE_BUFS:2 * PIPE_BUFS]
    m_ref, acc_ref = tile_and_state_refs[2 * PIPE_BUFS:]

    wo_out_ref[...] = wo_ref[...].astype(wo_out_ref.dtype)

    _store_transposed(v_ref, vt_ref.at[:d, :])
    vt_ref[d:, :] = jnp.ones((vt_ref.shape[0] - d, vt_ref.shape[1]), vt_ref.dtype)
    for c in range(0, q_ref.shape[0], blk):
        rows = slice(c, c + blk)
        x = jnp.broadcast_to(cum_ref[:, rows] * (-1.0 / SCALE), (LANES, blk)).T
        hi = x.astype(bf16).astype(f32)
        mid = (x - hi).astype(bf16).astype(f32)
        lo = ((x - hi) - mid).astype(bf16).astype(f32)
        aug = jnp.where(lane == 0, hi, jnp.where(lane == 1, mid, jnp.where(lane == 2, lo, 0.0)))
        k2_ref[rows, :d] = k_ref[rows, :]
        k2_ref[rows, d:] = aug.astype(bf16)
        q2_ref[rows, :d] = q_ref[rows, :]
        q2_ref[rows, d:] = jnp.where(lane < 3, 1.0, 0.0).astype(bf16)
    m_ref[...] = jnp.full(m_ref.shape, NEG_INF, jnp.float32)
    acc_ref[...] = jnp.zeros(acc_ref.shape, jnp.float32)

    def starts(t):
        return (pl.multiple_of(pair_ref[0, t] * blk, blk), pl.multiple_of(pair_ref[1, t] * blk, blk))

    def stage_a(t, kind, buf):
        qs, ks = starts(t)
        u = _scores_t(k2_ref[pl.ds(ks, blk), :], q2_ref[pl.ds(qs, blk), :]) * (SCALE * LOG2E)
        if kind == DIAG:
            u = jnp.where(qry >= key, u, NEG_INF)
        u_refs[buf][...] = u
        bm_refs[buf][...] = jnp.max(u, axis=0, keepdims=True)

    def stage_b(t, buf):
        qs, ks = starts(t)
        cq = cum_ref[:, pl.ds(qs, blk)] * LOG2E
        _softmax_pv_update(u_refs[buf][...], bm_refs[buf][...], cq, vt_ref[:, pl.ds(ks, blk)],
                           m_ref, None, acc_ref, qs)

    _run_pipeline(kinds, stage_a, stage_b)

    for c in range(0, o_ref.shape[0], blk):
        cols = slice(c, c + blk)
        o_t = acc_ref[:d, cols] * (1.0 / acc_ref[d:d + 1, cols])
        o_ref[cols, :] = o_t.T.astype(o_ref.dtype)


def _fox_attention(qkv, cum, w_o):
    bsz, seq, _ = qkv.shape
    blk = ATTN_BLOCK
    q_off = 3 * DIFF_WIDTH // HEAD_DIM
    k_off = q_off + N_FOX_HEADS
    v_off = k_off + N_FOX_HEADS
    kinds, pairs = _pair_schedule(seq // blk, with_near=False)
    _, wo_rows, wo_cols = w_o.shape
    slab = wo_rows // (N_FOX_HEADS * bsz)
    assert slab * N_FOX_HEADS * bsz == wo_rows and slab % 16 == 0
    return pl.pallas_call(
        functools.partial(_fox_attn_kernel, kinds),
        grid=(N_FOX_HEADS, bsz),
        in_specs=[pl.BlockSpec(memory_space=pltpu.SMEM),
                  pl.BlockSpec((None, seq, HEAD_DIM), lambda h, b: (b, 0, q_off + h)),
                  pl.BlockSpec((None, seq, HEAD_DIM), lambda h, b: (b, 0, k_off + h)),
                  pl.BlockSpec((None, seq, HEAD_DIM), lambda h, b: (b, 0, v_off + h)),
                  pl.BlockSpec((None, None, 1, seq), lambda h, b: (h, b, 0, 0)),
                  pl.BlockSpec((None, slab, wo_cols), lambda h, b: (0, h * bsz + b, 0))],
        out_specs=[pl.BlockSpec((None, seq, HEAD_DIM), lambda h, b: (b, 0, h)),
                   pl.BlockSpec((slab, wo_cols), lambda h, b: (h * bsz + b, 0))],
        out_shape=[jax.ShapeDtypeStruct((bsz, seq, FOX_WIDTH), jnp.bfloat16),
                   jax.ShapeDtypeStruct((wo_rows, wo_cols), jnp.bfloat16)],
        scratch_shapes=([pltpu.VMEM((HEAD_DIM + FOX_SUM_ROWS, seq), jnp.bfloat16),
                         pltpu.VMEM((seq, 2 * HEAD_DIM), jnp.bfloat16),
                         pltpu.VMEM((seq, 2 * HEAD_DIM), jnp.bfloat16)]
                        + [pltpu.VMEM((blk, blk), jnp.float32)] * PIPE_BUFS
                        + [pltpu.VMEM((1, blk), jnp.float32)] * PIPE_BUFS
                        + [pltpu.VMEM((1, seq), jnp.float32),
                           pltpu.VMEM((HEAD_DIM + FOX_SUM_ROWS, seq), jnp.float32)]),
        compiler_params=_params(("arbitrary", "arbitrary")),
        name="fox_attention",
    )(jnp.asarray(pairs), qkv, qkv, qkv, cum, w_o)


def _out_proj_kernel(ad_ref, af_ref, w_ref, res_ref, g_ref, x1_ref, xg_ref, ssq_ref):
    j = pl.program_id(1)
    kd = ad_ref.shape[1]
    wd = w_ref[:kd, :]
    wf = w_ref[kd:, :]
    parts = []
    for c in range(0, ad_ref.shape[0], MATMUL_ROW_CHUNK):
        rows = slice(c, c + MATMUL_ROW_CHUNK)
        y = (res_ref[rows, :] + jnp.dot(ad_ref[rows, :], wd, preferred_element_type=jnp.float32)
             + jnp.dot(af_ref[rows, :], wf, preferred_element_type=jnp.float32))
        x1_ref[rows, :] = y
        xg_ref[rows, :] = (y * g_ref[...]).astype(xg_ref.dtype)
        sq = y * y
        part = sq[:, :LANES]
        for lc in range(LANES, sq.shape[1], LANES):
            part = part + sq[:, lc:lc + LANES]
        parts.append(part)
    part = jnp.concatenate(parts, axis=0)

    @pl.when(j == 0)
    def _first():
        ssq_ref[...] = part

    @pl.when(j > 0)
    def _rest():
        ssq_ref[...] += part


def _out_proj(a_diff, a_fox, w, res, g, bm=1024, bn=512):
    m, kd = a_diff.shape
    kf = a_fox.shape[1]
    d = w.shape[-1]
    return pl.pallas_call(
        _out_proj_kernel,
        grid=(m // bm, d // bn),
        in_specs=[pl.BlockSpec((bm, kd), lambda i, j: (i, 0)),
                  pl.BlockSpec((bm, kf), lambda i, j: (i, 0)),
                  pl.BlockSpec((kd + kf, bn), lambda i, j: (0, j)),
                  pl.BlockSpec((bm, bn), lambda i, j: (i, j)),
                  pl.BlockSpec((1, bn), lambda i, j: (0, j))],
        out_specs=[pl.BlockSpec((bm, bn), lambda i, j: (i, j)),
                   pl.BlockSpec((bm, bn), lambda i, j: (i, j)),
                   pl.BlockSpec((bm, LANES), lambda i, j: (i, 0))],
        out_shape=[jax.ShapeDtypeStruct((m, d), jnp.float32),
                   jax.ShapeDtypeStruct((m, d), jnp.bfloat16),
                   jax.ShapeDtypeStruct((m, LANES), jnp.float32)],
        compiler_params=_params(("arbitrary", "arbitrary")),
        name="out_proj",
    )(a_diff, a_fox, w, res, g)


def _down_kernel(a_ref, w_ref, res_ref, g_ref, o_ref, *, n_k, k_last):
    k = pl.program_id(1)
    bk = a_ref.shape[1]

    def accumulate(k_len, base_ref):
        a = a_ref[:, :k_len]
        for c in range(0, o_ref.shape[1], PROJ_COL_CHUNK):
            cols = slice(c, c + PROJ_COL_CHUNK)
            o_ref[:, cols] = base_ref[:, cols] + jnp.dot(a, w_ref[:k_len, cols],
                                                         preferred_element_type=jnp.float32)

    pl.when(k == 0)(lambda: accumulate(bk, res_ref))
    pl.when((k > 0) & (k < n_k - 1))(lambda: accumulate(bk, o_ref))

    @pl.when(k == n_k - 1)
    def _finish():
        accumulate(k_last, o_ref)
        for r in range(0, o_ref.shape[0], PROJ_ROW_CHUNK):
            rows = slice(r, r + PROJ_ROW_CHUNK)
            x = o_ref[rows, :]
            o_ref[rows, :] = x * lax.rsqrt(jnp.mean(x * x, axis=-1, keepdims=True) + EPS) * g_ref[...]


def _down_proj_norm(a, w, res, g, bm=512, bk=DOWN_K_TILE):
    m, kdim = a.shape
    d = w.shape[1]
    n_k = pl.cdiv(kdim, bk)
    assert n_k >= 2
    k_last = kdim - (n_k - 1) * bk
    row_block = pl.BlockSpec((bm, d), lambda i, k: (i, 0))
    return pl.pallas_call(
        functools.partial(_down_kernel, n_k=n_k, k_last=k_last),
        grid=(m // bm, n_k),
        in_specs=[pl.BlockSpec((bm, bk), lambda i, k: (i, k)),
                  pl.BlockSpec((bk, d), lambda i, k: (k, 0)),
                  row_block,
                  pl.BlockSpec((1, d), lambda i, k: (0, 0))],
        out_specs=row_block,
        out_shape=jax.ShapeDtypeStruct((m, d), jnp.float32),
        compiler_params=_params(("arbitrary", "arbitrary")),
        name="ffn_down",
    )(a, w, res, g)


def _gate_up_kernel(xg_ref, ssq_ref, wg_ref, wu_ref, wd_ref, o_ref, wd_out_ref, r_ref):
    @pl.when(pl.program_id(1) == 0)
    def _row_scale():
        mean_sq = jnp.sum(ssq_ref[...], axis=-1, keepdims=True) * (1.0 / xg_ref.shape[1])
        r_ref[...] = jnp.broadcast_to(lax.rsqrt(mean_sq + EPS), r_ref.shape)

    wg = wg_ref[...].astype(xg_ref.dtype)
    wu = wu_ref[...].astype(xg_ref.dtype)
    for c in range(0, xg_ref.shape[0], MATMUL_ROW_CHUNK):
        rows = slice(c, c + MATMUL_ROW_CHUNK)
        xg = xg_ref[rows, :]
        r = jnp.concatenate([r_ref[rows, :]] * (o_ref.shape[1] // LANES), axis=1)
        gate = jnp.dot(xg, wg, preferred_element_type=jnp.float32) * r
        up = jnp.dot(xg, wu, preferred_element_type=jnp.float32) * r
        o_ref[rows, :] = (gate * jax.nn.sigmoid(gate) * up).astype(o_ref.dtype)
    wd_out_ref[...] = wd_ref[...].astype(wd_out_ref.dtype)


def _gate_up(xg, ssq, wg, wu, wd, bm=ROW_TILE, bn=FF_TILE):
    m, k = xg.shape
    _, f_rows, d = wd.shape
    n = wg.shape[2]
    n_j = n // bn
    slab = f_rows // ((m // bm) * n_j)
    assert slab * (m // bm) * n_j == f_rows and slab % 16 == 0
    return pl.pallas_call(
        _gate_up_kernel,
        grid=(m // bm, n_j),
        in_specs=[pl.BlockSpec((bm, k), lambda i, j: (i, 0), pipeline_mode=pl.Buffered(1)),
                  pl.BlockSpec((bm, LANES), lambda i, j: (i, 0)),
                  pl.BlockSpec((None, k, bn), lambda i, j: (0, 0, j)),
                  pl.BlockSpec((None, k, bn), lambda i, j: (0, 0, j)),
                  pl.BlockSpec((None, slab, d), lambda i, j: (0, i * n_j + j, 0))],
        out_specs=[pl.BlockSpec((bm, bn), lambda i, j: (i, j)),
                   pl.BlockSpec((slab, d), lambda i, j: (i * n_j + j, 0))],
        out_shape=[jax.ShapeDtypeStruct((m, n), jnp.bfloat16),
                   jax.ShapeDtypeStruct((f_rows, d), jnp.bfloat16)],
        scratch_shapes=[pltpu.VMEM((bm, LANES), jnp.float32)],
        compiler_params=_params(("arbitrary", "arbitrary")),
        name="ffn_gate_up",
    )(xg, ssq, wg, wu, wd)


def kernel(x, attn_norm_g, w_in, b_f, lambda_q1, lambda_k1, lambda_q2, lambda_k2, rel_bias_table,
           diff_subln_g, w_o, ffn_norm_g, w_gate, w_up, w_down, final_norm_g):
    bsz, seq, d = x.shape
    m = bsz * seq
    x2d = x.reshape(m, d)

    w_in_t = jnp.swapaxes(w_in, 1, 2)
    h, fl_t = _norm_and_gate_logits(x2d, attn_norm_g[0].reshape(1, d), w_in_t, QKV_COLS, N_FOX_HEADS)
    qkv = _matmul(h, w_in_t, QKV_COLS, name="in_proj").reshape(bsz, seq, QKV_COLS)

    cum = _cum_log_forget(fl_t, b_f[0].reshape(N_FOX_HEADS, 1), seq)
    cum = cum.reshape(N_FOX_HEADS, bsz, 1, seq)

    o_diff = _diff_attention(qkv, rel_bias_table,
                             lambda_q1[0].reshape(1, HEAD_DIM), lambda_k1[0].reshape(1, HEAD_DIM),
                             lambda_q2[0].reshape(1, HEAD_DIM), lambda_k2[0].reshape(1, HEAD_DIM),
                             diff_subln_g[0].reshape(2 * HEAD_DIM, 1))
    o_fox, w_o_b = _fox_attention(qkv, cum, w_o)

    x1, x1g, ssq = _out_proj(o_diff.reshape(m, DIFF_WIDTH), o_fox.reshape(m, FOX_WIDTH), w_o_b, x2d,
                             ffn_norm_g[0].reshape(1, d))
    act, w_down_b = _gate_up(x1g, ssq, w_gate, w_up, w_down)
    out = _down_proj_norm(act, w_down_b, x1, final_norm_g.reshape(1, d))
    return out.reshape(bsz, seq, d)
```

```python
import functools
import math

import numpy as np

import jax
import jax.numpy as jnp
from jax import lax
from jax.experimental import pallas as pl
from jax.experimental.pallas import tpu as pltpu

D_MODEL = 4096
HEAD_DIM = 128
N_DIFF_HEADS = D_MODEL // (4 * HEAD_DIM)
N_FOX_HEADS = D_MODEL // (2 * HEAD_DIM)
DIFF_WIDTH = N_DIFF_HEADS * 2 * HEAD_DIM
FOX_WIDTH = N_FOX_HEADS * HEAD_DIM
MIX_WIDTH = DIFF_WIDTH + FOX_WIDTH
QKV_COLS = 3 * DIFF_WIDTH + 3 * FOX_WIDTH
D_FF = ((8 * D_MODEL + 3 * 256 - 1) // (3 * 256)) * 256
NUM_BUCKETS = 32
MAX_DISTANCE = 128
EPS = 1e-6
NEG_INF = -1e30
LAM_INIT = 0.8 - 0.6 * math.exp(-0.3 * 0)
SCALE = HEAD_DIM ** -0.5
LOG2E = math.log2(math.e)

LANES = 128
ROW_TILE = 2048
FF_TILE = 256
DOWN_K_TILE = 1024
VMEM_LIMIT = 56 * 1024 * 1024

ATTN_BLOCK = 512
PIPE_DEPTH = 2
PIPE_BUFS = PIPE_DEPTH + 1
FOX_SUM_ROWS = 16
MATMUL_ROW_CHUNK = 512
PROJ_COL_CHUNK = 1024
PROJ_ROW_CHUNK = 128

FAR, NEAR, DIAG = "far", "near", "diag"


def _params(sem, vmem=VMEM_LIMIT):
    return pltpu.CompilerParams(dimension_semantics=sem, vmem_limit_bytes=vmem)


def _norm_kernel(x_ref, g_ref, wf_ref, h_ref, fl_ref):
    x = x_ref[...]
    y = x * lax.rsqrt(jnp.mean(x * x, axis=-1, keepdims=True) + EPS)
    h = (y * g_ref[...]).astype(jnp.bfloat16)
    h_ref[...] = h
    fl_ref[...] = lax.dot_general(wf_ref[...].astype(jnp.bfloat16), h, (((1,), (1,)), ((), ())),
                                  preferred_element_type=jnp.float32)


def _norm_and_gate_logits(x2d, g, w_in_t, gate_row, nh, bm=512):
    m, d = x2d.shape
    assert gate_row % nh == 0 and gate_row + nh == w_in_t.shape[1]
    return pl.pallas_call(
        _norm_kernel,
        grid=(m // bm,),
        in_specs=[pl.BlockSpec((bm, d), lambda i: (i, 0)),
                  pl.BlockSpec((1, d), lambda i: (0, 0)),
                  pl.BlockSpec((None, nh, d), lambda i: (0, gate_row // nh, 0))],
        out_specs=[pl.BlockSpec((bm, d), lambda i: (i, 0)),
                   pl.BlockSpec((nh, bm), lambda i: (0, i))],
        out_shape=[jax.ShapeDtypeStruct((m, d), jnp.bfloat16),
                   jax.ShapeDtypeStruct((nh, m), jnp.float32)],
        compiler_params=_params(("arbitrary",)),
        name="attn_norm",
    )(x2d, g, w_in_t)


def _matmul_kernel(a_ref, wt_ref, o_ref):
    o_ref[...] = lax.dot_general(a_ref[...], wt_ref[...].astype(a_ref.dtype),
                                 (((1,), (1,)), ((), ())),
                                 preferred_element_type=jnp.float32).astype(o_ref.dtype)


def _matmul(a, w_t, n_out, bm=ROW_TILE, bn=512, name="matmul"):
    m, k = a.shape
    return pl.pallas_call(
        _matmul_kernel,
        grid=(m // bm, n_out // bn),
        in_specs=[pl.BlockSpec((bm, k), lambda i, j: (i, 0), pipeline_mode=pl.Buffered(1)),
                  pl.BlockSpec((None, bn, k), lambda i, j: (0, j, 0))],
        out_specs=pl.BlockSpec((bm, bn), lambda i, j: (i, j)),
        out_shape=jax.ShapeDtypeStruct((m, n_out), jnp.bfloat16),
        compiler_params=_params(("arbitrary", "arbitrary")),
        name=name,
    )(a, w_t)


def _cumsum_kernel(fl_ref, bf_ref, c_ref):
    z = fl_ref[...] + bf_ref[...]
    x = jnp.minimum(z, 0.0) - jnp.log(1.0 + jnp.exp(-jnp.abs(z)))
    n = x.shape[-1]
    pos = lax.broadcasted_iota(jnp.int32, x.shape, 1)
    shift = 1
    while shift < n:
        x = x + jnp.where(pos >= shift, pltpu.roll(x, shift, 1), 0.0)
        shift *= 2
    c_ref[...] = x


def _cum_log_forget(fl_t, b_f, seq):
    nh, m = fl_t.shape
    return pl.pallas_call(
        _cumsum_kernel,
        grid=(m // seq,),
        in_specs=[pl.BlockSpec((nh, seq), lambda b: (0, b)),
                  pl.BlockSpec((nh, 1), lambda b: (0, 0))],
        out_specs=pl.BlockSpec((nh, seq), lambda b: (0, b)),
        out_shape=jax.ShapeDtypeStruct((nh, m), jnp.float32),
        compiler_params=_params(("arbitrary",)),
        name="cum_log_forget",
    )(fl_t, b_f)


def _scores_t(k, q):
    return lax.dot_general(k, q, (((1,), (1,)), ((), ())), preferred_element_type=jnp.float32)


def _softmax_pv_update(u, blk_max, shift, v_t, m_ref, l_ref, acc_ref, qs):
    blk = u.shape[1]
    m_old = m_ref[:, pl.ds(qs, blk)]
    m_new = jnp.maximum(m_old, blk_max + shift)
    alpha = jnp.exp2(m_old - m_new)
    p = jnp.exp2(u + (shift - m_new))
    if l_ref is not None:
        l_ref[:, pl.ds(qs, blk)] = (alpha * l_ref[:, pl.ds(qs, blk)]
                                    + jnp.sum(p, axis=0, keepdims=True))
    acc_ref[:, pl.ds(qs, blk)] = alpha * acc_ref[:, pl.ds(qs, blk)] + jnp.dot(
        v_t, p.astype(v_t.dtype), preferred_element_type=jnp.float32)
    m_ref[:, pl.ds(qs, blk)] = m_new


def _pair_schedule(n_blocks, with_near):
    far, near, diag = [], [], []
    for i in range(n_blocks):
        for j in range(i + 1):
            if j == i:
                diag.append((i, j))
            elif with_near and j == i - 1:
                near.append((i, j))
            else:
                far.append((i, j))
    pairs = far + near + diag
    kinds = [FAR] * len(far) + [NEAR] * len(near) + [DIAG] * len(diag)
    return kinds, np.asarray(pairs, np.int32).T.copy()


def _run_pipeline(kinds, stage_a, stage_b):
    n = len(kinds)
    ahead_kind = list(kinds[PIPE_DEPTH:]) + [None] * PIPE_DEPTH
    for t in range(min(PIPE_DEPTH, n)):
        stage_a(t, kinds[t], t % PIPE_BUFS)
    s = 0
    while s < n:
        e = s
        while e < n and ahead_kind[e] == ahead_kind[s]:
            e += 1
        kind, n_loop = ahead_kind[s], (e - s) // PIPE_BUFS

        def one(step, buf):
            if kind is not None:
                stage_a(step + PIPE_DEPTH, kind, (buf + PIPE_DEPTH) % PIPE_BUFS)
            stage_b(step, buf)

        if n_loop:
            def body(r, carry):
                base = s + PIPE_BUFS * r
                for i in range(PIPE_BUFS):
                    one(base + i, (s + i) % PIPE_BUFS)
                return carry

            lax.fori_loop(0, n_loop, body, 0)
        for step in range(s + n_loop * PIPE_BUFS, e):
            one(step, step % PIPE_BUFS)
        s = e


def _store_transposed(src_ref, dst_ref):
    for c in range(0, src_ref.shape[0], ATTN_BLOCK):
        dst_ref[:, c:c + ATTN_BLOCK] = (
            src_ref[c:c + ATTN_BLOCK, :].astype(jnp.float32).T.astype(dst_ref.dtype))


def _t5_bucket(n):
    max_exact = NUM_BUCKETS // 2
    nf = jnp.maximum(n, 1).astype(jnp.float32)
    large = max_exact + (jnp.log(nf / max_exact) / math.log(MAX_DISTANCE / max_exact)
                         * (NUM_BUCKETS - max_exact)).astype(jnp.int32)
    large = jnp.minimum(large, NUM_BUCKETS - 1)
    return jnp.where(n < max_exact, n, large)


def _diff_attn_kernel(kinds, pair_ref, tab_ref, lq1_ref, lk1_ref, lq2_ref, lk2_ref,
                      q_ref, k_ref, v_ref, g_ref, w_ref, o_ref, w_out_ref,
                      bias_near_ref, bias_diag_ref, vt_ref, *tile_and_state_refs):
    h = pl.program_id(0)
    b = pl.program_id(1)
    blk = ATTN_BLOCK
    w_out_ref[...] = w_ref[...].astype(w_out_ref.dtype)
    key = lax.broadcasted_iota(jnp.int32, (blk, blk), 0)
    qry = lax.broadcasted_iota(jnp.int32, (blk, blk), 1)
    u_refs = tile_and_state_refs[:PIPE_BUFS]
    bm_refs = tile_and_state_refs[PIPE_BUFS:2 * PIPE_BUFS]
    m_ref, l_ref, acc_ref = tile_and_state_refs[2 * PIPE_BUFS:]
    far_bias = tab_ref[NUM_BUCKETS - 1, h]

    @pl.when(b == 0)
    def _build_bias():
        for ref, offset in ((bias_diag_ref, 0), (bias_near_ref, blk)):
            bucket = _t5_bucket(jnp.maximum(qry - key + offset, 0))
            bias = jnp.zeros((blk, blk), jnp.float32)
            for t in range(NUM_BUCKETS):
                bias = jnp.where(bucket == t, tab_ref[t, h], bias)
            ref[...] = (bias - far_bias) * LOG2E

    _store_transposed(v_ref, vt_ref)
    m_ref[...] = jnp.full(m_ref.shape, NEG_INF, jnp.float32)
    l_ref[...] = jnp.zeros(l_ref.shape, jnp.float32)
    acc_ref[...] = jnp.zeros(acc_ref.shape, jnp.float32)

    def starts(t):
        return (pl.multiple_of(pair_ref[0, t] * blk, blk), pl.multiple_of(pair_ref[1, t] * blk, blk))

    def stage_a(t, kind, par):
        qs, ks = starts(t)
        for mp in range(2):
            cols = slice(mp * HEAD_DIM, (mp + 1) * HEAD_DIM)
            u = _scores_t(k_ref[pl.ds(ks, blk), cols], q_ref[pl.ds(qs, blk), cols]) * (SCALE * LOG2E)
            if kind == NEAR:
                u = u + bias_near_ref[...]
            elif kind == DIAG:
                u = jnp.where(qry >= key, u + bias_diag_ref[...], NEG_INF)
            u_refs[par][mp] = u
            bm_refs[par][mp] = jnp.max(u, axis=0, keepdims=True)

    def stage_b(t, par):
        qs, ks = starts(t)
        v_t = vt_ref[:, pl.ds(ks, blk)]
        for mp in range(2):
            _softmax_pv_update(u_refs[par][mp], bm_refs[par][mp], far_bias * LOG2E, v_t,
                               m_ref.at[mp], l_ref.at[mp], acc_ref.at[mp], qs)

    _run_pipeline(kinds, stage_a, stage_b)

    lam = (jnp.exp(jnp.sum(lq1_ref[...] * lk1_ref[...], axis=-1, keepdims=True))
           - jnp.exp(jnp.sum(lq2_ref[...] * lk2_ref[...], axis=-1, keepdims=True))
           + LAM_INIT)
    for c in range(0, o_ref.shape[0], blk):
        cols = slice(c, c + blk)
        o_t = (acc_ref[0, :, cols] * (1.0 / l_ref[0, :, cols])
               - lam * (acc_ref[1, :, cols] * (1.0 / l_ref[1, :, cols])))
        y_t = o_t * lax.rsqrt(jnp.mean(o_t * o_t, axis=0, keepdims=True) + EPS)
        y_t = (y_t * g_ref[...]) * (1.0 - LAM_INIT)
        o_ref[cols, :] = y_t.T.astype(o_ref.dtype)


def _slab_spec(weight, n_steps, step_of):
    rows, cols = weight.shape[-2:]
    slab = rows // n_steps
    assert slab * n_steps == rows and slab % 16 == 0
    if weight.ndim == 3:
        return pl.BlockSpec((None, slab, cols), lambda *idx: (0, step_of(*idx), 0))
    return pl.BlockSpec((slab, cols), lambda *idx: (step_of(*idx), 0))


def _bf16_like(weight):
    return jax.ShapeDtypeStruct(weight.shape[-2:], jnp.bfloat16)


def _diff_attention(qkv, tab, lq1, lk1, lq2, lk2, g_col, w_conv):
    bsz, seq, _ = qkv.shape
    blk = ATTN_BLOCK
    w = 2 * HEAD_DIM
    k_off = DIFF_WIDTH // w
    v_off = 2 * DIFF_WIDTH // w
    kinds, pairs = _pair_schedule(seq // blk, with_near=True)
    smem = pl.BlockSpec(memory_space=pltpu.SMEM)
    vec = pl.BlockSpec((1, HEAD_DIM), lambda h, b: (0, 0))
    n_steps = N_DIFF_HEADS * bsz

    def step_of(h, b):
        return h * bsz + b

    return pl.pallas_call(
        functools.partial(_diff_attn_kernel, kinds),
        grid=(N_DIFF_HEADS, bsz),
        in_specs=[smem, smem, vec, vec, vec, vec,
                  pl.BlockSpec((None, seq, w), lambda h, b: (b, 0, h)),
                  pl.BlockSpec((None, seq, w), lambda h, b: (b, 0, k_off + h)),
                  pl.BlockSpec((None, seq, w), lambda h, b: (b, 0, v_off + h)),
                  pl.BlockSpec((w, 1), lambda h, b: (0, 0)),
                  _slab_spec(w_conv, n_steps, step_of)],
        out_specs=[pl.BlockSpec((None, seq, w), lambda h, b: (b, 0, h)),
                   _slab_spec(_bf16_like(w_conv), n_steps, step_of)],
        out_shape=[jax.ShapeDtypeStruct((bsz, seq, DIFF_WIDTH), jnp.bfloat16), _bf16_like(w_conv)],
        scratch_shapes=([pltpu.VMEM((blk, blk), jnp.float32),
                         pltpu.VMEM((blk, blk), jnp.float32),
                         pltpu.VMEM((w, seq), jnp.bfloat16)]
                        + [pltpu.VMEM((2, blk, blk), jnp.float32)] * PIPE_BUFS
                        + [pltpu.VMEM((2, 1, blk), jnp.float32)] * PIPE_BUFS
                        + [pltpu.VMEM((2, 1, seq), jnp.float32),
                           pltpu.VMEM((2, 1, seq), jnp.float32),
                           pltpu.VMEM((2, w, seq), jnp.float32)]),
        compiler_params=_params(("arbitrary", "arbitrary")),
        name="diff_attention",
    )(jnp.asarray(pairs), tab, lq1, lk1, lq2, lk2, qkv, qkv, qkv, g_col, w_conv)


def _fox_attn_kernel(kinds, pair_ref, q_ref, k_ref, v_ref, cum_ref, wa_ref, wb_ref,
                     o_ref, wa_out_ref, wb_out_ref,
                     vt_ref, q2_ref, k2_ref, *tile_and_state_refs):
    blk = ATTN_BLOCK
    d = HEAD_DIM
    f32, bf16 = jnp.float32, jnp.bfloat16
    key = lax.broadcasted_iota(jnp.int32, (blk, blk), 0)
    qry = lax.broadcasted_iota(jnp.int32, (blk, blk), 1)
    lane = lax.broadcasted_iota(jnp.int32, (blk, LANES), 1)
    u_refs = tile_and_state_refs[:PIPE_BUFS]
    bm_refs = tile_and_state_refs[PIPE_BUFS:2 * PIPE_BUFS]
    m_ref, acc_ref = tile_and_state_refs[2 * PIPE_BUFS:]

    wa_out_ref[...] = wa_ref[...].astype(wa_out_ref.dtype)
    wb_out_ref[...] = wb_ref[...].astype(wb_out_ref.dtype)

    _store_transposed(v_ref, vt_ref.at[:d, :])
    vt_ref[d:, :] = jnp.ones((vt_ref.shape[0] - d, vt_ref.shape[1]), vt_ref.dtype)
    for c in range(0, q_ref.shape[0], blk):
        rows = slice(c, c + blk)
        x = jnp.broadcast_to(cum_ref[:, rows] * (-1.0 / SCALE), (LANES, blk)).T
        hi = x.astype(bf16).astype(f32)
        mid = (x - hi).astype(bf16).astype(f32)
        lo = ((x - hi) - mid).astype(bf16).astype(f32)
        aug = jnp.where(lane == 0, hi, jnp.where(lane == 1, mid, jnp.where(lane == 2, lo, 0.0)))
        k2_ref[rows, :d] = k_ref[rows, :]
        k2_ref[rows, d:] = aug.astype(bf16)
        q2_ref[rows, :d] = q_ref[rows, :]
        q2_ref[rows, d:] = jnp.where(lane < 3, 1.0, 0.0).astype(bf16)
    m_ref[...] = jnp.full(m_ref.shape, NEG_INF, jnp.float32)
    acc_ref[...] = jnp.zeros(acc_ref.shape, jnp.float32)

    def starts(t):
        return (pl.multiple_of(pair_ref[0, t] * blk, blk), pl.multiple_of(pair_ref[1, t] * blk, blk))

    def stage_a(t, kind, buf):
        qs, ks = starts(t)
        u = _scores_t(k2_ref[pl.ds(ks, blk), :], q2_ref[pl.ds(qs, blk), :]) * (SCALE * LOG2E)
        if kind == DIAG:
            u = jnp.where(qry >= key, u, NEG_INF)
        u_refs[buf][...] = u
        bm_refs[buf][...] = jnp.max(u, axis=0, keepdims=True)

    def stage_b(t, buf):
        qs, ks = starts(t)
        cq = cum_ref[:, pl.ds(qs, blk)] * LOG2E
        _softmax_pv_update(u_refs[buf][...], bm_refs[buf][...], cq, vt_ref[:, pl.ds(ks, blk)],
                           m_ref, None, acc_ref, qs)

    _run_pipeline(kinds, stage_a, stage_b)

    for c in range(0, o_ref.shape[0], blk):
        cols = slice(c, c + blk)
        o_t = acc_ref[:d, cols] * (1.0 / acc_ref[d:d + 1, cols])
        o_ref[cols, :] = o_t.T.astype(o_ref.dtype)


def _fox_attention(qkv, cum, w_conv_a, w_conv_b):
    bsz, seq, _ = qkv.shape
    blk = ATTN_BLOCK
    q_off = 3 * DIFF_WIDTH // HEAD_DIM
    k_off = q_off + N_FOX_HEADS
    v_off = k_off + N_FOX_HEADS
    kinds, pairs = _pair_schedule(seq // blk, with_near=False)
    n_steps = N_FOX_HEADS * bsz

    def step_of(h, b):
        return h * bsz + b

    return pl.pallas_call(
        functools.partial(_fox_attn_kernel, kinds),
        grid=(N_FOX_HEADS, bsz),
        in_specs=[pl.BlockSpec(memory_space=pltpu.SMEM),
                  pl.BlockSpec((None, seq, HEAD_DIM), lambda h, b: (b, 0, q_off + h)),
                  pl.BlockSpec((None, seq, HEAD_DIM), lambda h, b: (b, 0, k_off + h)),
                  pl.BlockSpec((None, seq, HEAD_DIM), lambda h, b: (b, 0, v_off + h)),
                  pl.BlockSpec((None, None, 1, seq), lambda h, b: (h, b, 0, 0)),
                  _slab_spec(w_conv_a, n_steps, step_of),
                  _slab_spec(w_conv_b, n_steps, step_of)],
        out_specs=[pl.BlockSpec((None, seq, HEAD_DIM), lambda h, b: (b, 0, h)),
                   _slab_spec(_bf16_like(w_conv_a), n_steps, step_of),
                   _slab_spec(_bf16_like(w_conv_b), n_steps, step_of)],
        out_shape=[jax.ShapeDtypeStruct((bsz, seq, FOX_WIDTH), jnp.bfloat16),
                   _bf16_like(w_conv_a), _bf16_like(w_conv_b)],
        scratch_shapes=([pltpu.VMEM((HEAD_DIM + FOX_SUM_ROWS, seq), jnp.bfloat16),
                         pltpu.VMEM((seq, 2 * HEAD_DIM), jnp.bfloat16),
                         pltpu.VMEM((seq, 2 * HEAD_DIM), jnp.bfloat16)]
                        + [pltpu.VMEM((blk, blk), jnp.float32)] * PIPE_BUFS
                        + [pltpu.VMEM((1, blk), jnp.float32)] * PIPE_BUFS
                        + [pltpu.VMEM((1, seq), jnp.float32),
                           pltpu.VMEM((HEAD_DIM + FOX_SUM_ROWS, seq), jnp.float32)]),
        compiler_params=_params(("arbitrary", "arbitrary")),
        name="fox_attention",
    )(jnp.asarray(pairs), qkv, qkv, qkv, cum, w_conv_a, w_conv_b)


def _out_proj_kernel(ad_ref, af_ref, w_ref, res_ref, g_ref, x1_ref, xg_ref, ssq_ref):
    j = pl.program_id(1)
    kd = ad_ref.shape[1]
    wd = w_ref[:kd, :]
    wf = w_ref[kd:, :]
    parts = []
    for c in range(0, ad_ref.shape[0], MATMUL_ROW_CHUNK):
        rows = slice(c, c + MATMUL_ROW_CHUNK)
        y = (res_ref[rows, :] + jnp.dot(ad_ref[rows, :], wd, preferred_element_type=jnp.float32)
             + jnp.dot(af_ref[rows, :], wf, preferred_element_type=jnp.float32))
        x1_ref[rows, :] = y
        xg_ref[rows, :] = (y * g_ref[...]).astype(xg_ref.dtype)
        sq = y * y
        part = sq[:, :LANES]
        for lc in range(LANES, sq.shape[1], LANES):
            part = part + sq[:, lc:lc + LANES]
        parts.append(part)
    part = jnp.concatenate(parts, axis=0)

    @pl.when(j == 0)
    def _first():
        ssq_ref[...] = part

    @pl.when(j > 0)
    def _rest():
        ssq_ref[...] += part


def _out_proj(a_diff, a_fox, w, res, g, bm=1024, bn=512):
    m, kd = a_diff.shape
    kf = a_fox.shape[1]
    d = w.shape[-1]
    return pl.pallas_call(
        _out_proj_kernel,
        grid=(m // bm, d // bn),
        in_specs=[pl.BlockSpec((bm, kd), lambda i, j: (i, 0)),
                  pl.BlockSpec((bm, kf), lambda i, j: (i, 0)),
                  pl.BlockSpec((kd + kf, bn), lambda i, j: (0, j)),
                  pl.BlockSpec((bm, bn), lambda i, j: (i, j)),
                  pl.BlockSpec((1, bn), lambda i, j: (0, j))],
        out_specs=[pl.BlockSpec((bm, bn), lambda i, j: (i, j)),
                   pl.BlockSpec((bm, bn), lambda i, j: (i, j)),
                   pl.BlockSpec((bm, LANES), lambda i, j: (i, 0))],
        out_shape=[jax.ShapeDtypeStruct((m, d), jnp.float32),
                   jax.ShapeDtypeStruct((m, d), jnp.bfloat16),
                   jax.ShapeDtypeStruct((m, LANES), jnp.float32)],
        compiler_params=_params(("arbitrary", "arbitrary")),
        name="out_proj",
    )(a_diff, a_fox, w, res, g)


def _down_kernel(a_ref, w_ref, res_ref, g_ref, o_ref, *, n_k, k_last):
    k = pl.program_id(1)
    bk = a_ref.shape[1]

    def accumulate(k_len, base_ref):
        a = a_ref[:, :k_len]
        for c in range(0, o_ref.shape[1], PROJ_COL_CHUNK):
            cols = slice(c, c + PROJ_COL_CHUNK)
            o_ref[:, cols] = base_ref[:, cols] + jnp.dot(a, w_ref[:k_len, cols],
                                                         preferred_element_type=jnp.float32)

    pl.when(k == 0)(lambda: accumulate(bk, res_ref))
    pl.when((k > 0) & (k < n_k - 1))(lambda: accumulate(bk, o_ref))

    @pl.when(k == n_k - 1)
    def _finish():
        accumulate(k_last, o_ref)
        for r in range(0, o_ref.shape[0], PROJ_ROW_CHUNK):
            rows = slice(r, r + PROJ_ROW_CHUNK)
            x = o_ref[rows, :]
            o_ref[rows, :] = x * lax.rsqrt(jnp.mean(x * x, axis=-1, keepdims=True) + EPS) * g_ref[...]


def _down_proj_norm(a, w, res, g, bm=512, bk=DOWN_K_TILE):
    m, kdim = a.shape
    d = w.shape[1]
    n_k = pl.cdiv(kdim, bk)
    assert n_k >= 2
    k_last = kdim - (n_k - 1) * bk
    row_block = pl.BlockSpec((bm, d), lambda i, k: (i, 0))
    return pl.pallas_call(
        functools.partial(_down_kernel, n_k=n_k, k_last=k_last),
        grid=(m // bm, n_k),
        in_specs=[pl.BlockSpec((bm, bk), lambda i, k: (i, k)),
                  pl.BlockSpec((bk, d), lambda i, k: (k, 0)),
                  row_block,
                  pl.BlockSpec((1, d), lambda i, k: (0, 0))],
        out_specs=row_block,
        out_shape=jax.ShapeDtypeStruct((m, d), jnp.float32),
        compiler_params=_params(("arbitrary", "arbitrary")),
        name="ffn_down",
    )(a, w, res, g)


def _gate_up_kernel(xg_ref, ssq_ref, wg_ref, wu_ref, wd_ref, o_ref, wd_out_ref, r_ref):
    @pl.when(pl.program_id(1) == 0)
    def _row_scale():
        mean_sq = jnp.sum(ssq_ref[...], axis=-1, keepdims=True) * (1.0 / xg_ref.shape[1])
        r_ref[...] = jnp.broadcast_to(lax.rsqrt(mean_sq + EPS), r_ref.shape)

    wg = wg_ref[...]
    wu = wu_ref[...]
    for c in range(0, xg_ref.shape[0], MATMUL_ROW_CHUNK):
        rows = slice(c, c + MATMUL_ROW_CHUNK)
        xg = xg_ref[rows, :]
        r = jnp.concatenate([r_ref[rows, :]] * (o_ref.shape[1] // LANES), axis=1)
        gate = jnp.dot(xg, wg, preferred_element_type=jnp.float32) * r
        up = jnp.dot(xg, wu, preferred_element_type=jnp.float32) * r
        o_ref[rows, :] = (gate * jax.nn.sigmoid(gate) * up).astype(o_ref.dtype)
    wd_out_ref[...] = wd_ref[...].astype(wd_out_ref.dtype)


def _gate_up(xg, ssq, wg, wu, wd, bm=ROW_TILE, bn=FF_TILE):
    m, k = xg.shape
    f_rows, d = wd.shape[-2:]
    n = wg.shape[1]
    n_j = n // bn
    n_steps = (m // bm) * n_j

    def step_of(i, j):
        return i * n_j + j

    return pl.pallas_call(
        _gate_up_kernel,
        grid=(m // bm, n_j),
        in_specs=[pl.BlockSpec((bm, k), lambda i, j: (i, 0)),
                  pl.BlockSpec((bm, LANES), lambda i, j: (i, 0)),
                  pl.BlockSpec((k, bn), lambda i, j: (0, j)),
                  pl.BlockSpec((k, bn), lambda i, j: (0, j)),
                  _slab_spec(wd, n_steps, step_of)],
        out_specs=[pl.BlockSpec((bm, bn), lambda i, j: (i, j)),
                   _slab_spec(_bf16_like(wd), n_steps, step_of)],
        out_shape=[jax.ShapeDtypeStruct((m, n), jnp.bfloat16), _bf16_like(wd)],
        scratch_shapes=[pltpu.VMEM((bm, LANES), jnp.float32)],
        compiler_params=_params(("arbitrary", "arbitrary")),
        name="ffn_gate_up",
    )(xg, ssq, wg, wu, wd)


def kernel(x, attn_norm_g, w_in, b_f, lambda_q1, lambda_k1, lambda_q2, lambda_k2, rel_bias_table,
           diff_subln_g, w_o, ffn_norm_g, w_gate, w_up, w_down, final_norm_g):
    bsz, seq, d = x.shape
    m = bsz * seq
    x2d = x.reshape(m, d)

    w_in_t = jnp.swapaxes(w_in, 1, 2)
    h, fl_t = _norm_and_gate_logits(x2d, attn_norm_g[0].reshape(1, d), w_in_t, QKV_COLS, N_FOX_HEADS)
    qkv = _matmul(h, w_in_t, QKV_COLS, name="in_proj").reshape(bsz, seq, QKV_COLS)

    cum = _cum_log_forget(fl_t, b_f[0].reshape(N_FOX_HEADS, 1), seq)
    cum = cum.reshape(N_FOX_HEADS, bsz, 1, seq)

    o_diff, w_o_b = _diff_attention(
        qkv, rel_bias_table,
        lambda_q1[0].reshape(1, HEAD_DIM), lambda_k1[0].reshape(1, HEAD_DIM),
        lambda_q2[0].reshape(1, HEAD_DIM), lambda_k2[0].reshape(1, HEAD_DIM),
        diff_subln_g[0].reshape(2 * HEAD_DIM, 1), w_o)
    o_fox, w_gate_b, w_up_b = _fox_attention(qkv, cum, w_gate, w_up)

    x1, x1g, ssq = _out_proj(o_diff.reshape(m, DIFF_WIDTH), o_fox.reshape(m, FOX_WIDTH), w_o_b, x2d,
                             ffn_norm_g[0].reshape(1, d))
    act, w_down_b = _gate_up(x1g, ssq, w_gate_b, w_up_b, w_down)
    out = _down_proj_norm(act, w_down_b, x1, final_norm_g.reshape(1, d))
    return out.reshape(bsz, seq, d)
```

```python
import functools
import math

import numpy as np

import jax
import jax.numpy as jnp
from jax import lax
from jax.experimental import pallas as pl
from jax.experimental.pallas import tpu as pltpu

D_MODEL = 4096
HEAD_DIM = 128
N_DIFF_HEADS = D_MODEL // (4 * HEAD_DIM)
N_FOX_HEADS = D_MODEL // (2 * HEAD_DIM)
DIFF_WIDTH = N_DIFF_HEADS * 2 * HEAD_DIM
FOX_WIDTH = N_FOX_HEADS * HEAD_DIM
MIX_WIDTH = DIFF_WIDTH + FOX_WIDTH
QKV_COLS = 3 * DIFF_WIDTH + 3 * FOX_WIDTH
D_FF = ((8 * D_MODEL + 3 * 256 - 1) // (3 * 256)) * 256
NUM_BUCKETS = 32
MAX_DISTANCE = 128
EPS = 1e-6
NEG_INF = -1e30
LAM_INIT = 0.8 - 0.6 * math.exp(-0.3 * 0)
SCALE = HEAD_DIM ** -0.5
LOG2E = math.log2(math.e)

LANES = 128
ROW_TILE = 2048
FF_TILE = 256
DOWN_K_TILE = 512
DOWN_RES_TILE = 256
VMEM_LIMIT = 56 * 1024 * 1024

ATTN_BLOCK = 512
PIPE_DEPTH = 2
PIPE_BUFS = PIPE_DEPTH + 1
FOX_SUM_ROWS = 16
MATMUL_ROW_CHUNK = 512
PROJ_COL_CHUNK = 1024
PROJ_ROW_CHUNK = 128

FAR, NEAR, DIAG = "far", "near", "diag"


def _params(sem, vmem=VMEM_LIMIT):
    return pltpu.CompilerParams(dimension_semantics=sem, vmem_limit_bytes=vmem)


def _norm_kernel(x_ref, g_ref, wf_ref, h_ref, fl_ref):
    x = x_ref[...]
    y = x * lax.rsqrt(jnp.mean(x * x, axis=-1, keepdims=True) + EPS)
    h = (y * g_ref[...]).astype(jnp.bfloat16)
    h_ref[...] = h
    fl_ref[...] = lax.dot_general(wf_ref[...].astype(jnp.bfloat16), h, (((1,), (1,)), ((), ())),
                                  preferred_element_type=jnp.float32)


def _norm_and_gate_logits(x2d, g, w_in_t, gate_row, nh, bm=512):
    m, d = x2d.shape
    assert gate_row % nh == 0 and gate_row + nh == w_in_t.shape[1]
    return pl.pallas_call(
        _norm_kernel,
        grid=(m // bm,),
        in_specs=[pl.BlockSpec((bm, d), lambda i: (i, 0)),
                  pl.BlockSpec((1, d), lambda i: (0, 0)),
                  pl.BlockSpec((None, nh, d), lambda i: (0, gate_row // nh, 0))],
        out_specs=[pl.BlockSpec((bm, d), lambda i: (i, 0)),
                   pl.BlockSpec((nh, bm), lambda i: (0, i))],
        out_shape=[jax.ShapeDtypeStruct((m, d), jnp.bfloat16),
                   jax.ShapeDtypeStruct((nh, m), jnp.float32)],
        compiler_params=_params(("arbitrary",)),
        name="attn_norm",
    )(x2d, g, w_in_t)


def _matmul_kernel(a_ref, wt_ref, o_ref):
    o_ref[...] = lax.dot_general(a_ref[...], wt_ref[...].astype(a_ref.dtype),
                                 (((1,), (1,)), ((), ())),
                                 preferred_element_type=jnp.float32).astype(o_ref.dtype)


def _matmul(a, w_t, n_out, bm=ROW_TILE, bn=512, name="matmul"):
    m, k = a.shape
    return pl.pallas_call(
        _matmul_kernel,
        grid=(m // bm, n_out // bn),
        in_specs=[pl.BlockSpec((bm, k), lambda i, j: (i, 0), pipeline_mode=pl.Buffered(1)),
                  pl.BlockSpec((None, bn, k), lambda i, j: (0, j, 0))],
        out_specs=pl.BlockSpec((bm, bn), lambda i, j: (i, j)),
        out_shape=jax.ShapeDtypeStruct((m, n_out), jnp.bfloat16),
        compiler_params=_params(("arbitrary", "arbitrary")),
        name=name,
    )(a, w_t)


def _cumsum_kernel(fl_ref, bf_ref, c_ref):
    z = fl_ref[...] + bf_ref[...]
    x = jnp.minimum(z, 0.0) - jnp.log(1.0 + jnp.exp(-jnp.abs(z)))
    n = x.shape[-1]
    pos = lax.broadcasted_iota(jnp.int32, x.shape, 1)
    shift = 1
    while shift < n:
        x = x + jnp.where(pos >= shift, pltpu.roll(x, shift, 1), 0.0)
        shift *= 2
    c_ref[...] = x


def _cum_log_forget(fl_t, b_f, seq):
    nh, m = fl_t.shape
    return pl.pallas_call(
        _cumsum_kernel,
        grid=(m // seq,),
        in_specs=[pl.BlockSpec((nh, seq), lambda b: (0, b)),
                  pl.BlockSpec((nh, 1), lambda b: (0, 0))],
        out_specs=pl.BlockSpec((nh, seq), lambda b: (0, b)),
        out_shape=jax.ShapeDtypeStruct((nh, m), jnp.float32),
        compiler_params=_params(("arbitrary",)),
        name="cum_log_forget",
    )(fl_t, b_f)


def _scores_t(k, q):
    return lax.dot_general(k, q, (((1,), (1,)), ((), ())), preferred_element_type=jnp.float32)


def _softmax_pv_update(u, blk_max, shift, v_t, m_ref, l_ref, acc_ref, qs):
    blk = u.shape[1]
    m_old = m_ref[:, pl.ds(qs, blk)]
    m_new = jnp.maximum(m_old, blk_max + shift)
    alpha = jnp.exp2(m_old - m_new)
    p = jnp.exp2(u + (shift - m_new))
    if l_ref is not None:
        l_ref[:, pl.ds(qs, blk)] = (alpha * l_ref[:, pl.ds(qs, blk)]
                                    + jnp.sum(p, axis=0, keepdims=True))
    acc_ref[:, pl.ds(qs, blk)] = alpha * acc_ref[:, pl.ds(qs, blk)] + jnp.dot(
        v_t, p.astype(v_t.dtype), preferred_element_type=jnp.float32)
    m_ref[:, pl.ds(qs, blk)] = m_new


def _pair_schedule(n_blocks, with_near):
    far, near, diag = [], [], []
    for i in range(n_blocks):
        for j in range(i + 1):
            if j == i:
                diag.append((i, j))
            elif with_near and j == i - 1:
                near.append((i, j))
            else:
                far.append((i, j))
    pairs = far + near + diag
    kinds = [FAR] * len(far) + [NEAR] * len(near) + [DIAG] * len(diag)
    return kinds, np.asarray(pairs, np.int32).T.copy()


def _run_pipeline(kinds, stage_a, stage_b):
    n = len(kinds)
    ahead_kind = list(kinds[PIPE_DEPTH:]) + [None] * PIPE_DEPTH
    for t in range(min(PIPE_DEPTH, n)):
        stage_a(t, kinds[t], t % PIPE_BUFS)
    s = 0
    while s < n:
        e = s
        while e < n and ahead_kind[e] == ahead_kind[s]:
            e += 1
        kind, n_loop = ahead_kind[s], (e - s) // PIPE_BUFS

        def one(step, buf):
            if kind is not None:
                stage_a(step + PIPE_DEPTH, kind, (buf + PIPE_DEPTH) % PIPE_BUFS)
            stage_b(step, buf)

        if n_loop:
            def body(r, carry):
                base = s + PIPE_BUFS * r
                for i in range(PIPE_BUFS):
                    one(base + i, (s + i) % PIPE_BUFS)
                return carry

            lax.fori_loop(0, n_loop, body, 0)
        for step in range(s + n_loop * PIPE_BUFS, e):
            one(step, step % PIPE_BUFS)
        s = e


def _store_transposed(src_ref, dst_ref):
    for c in range(0, src_ref.shape[0], ATTN_BLOCK):
        dst_ref[:, c:c + ATTN_BLOCK] = (
            src_ref[c:c + ATTN_BLOCK, :].astype(jnp.float32).T.astype(dst_ref.dtype))


def _t5_bucket(n):
    max_exact = NUM_BUCKETS // 2
    nf = jnp.maximum(n, 1).astype(jnp.float32)
    large = max_exact + (jnp.log(nf / max_exact) / math.log(MAX_DISTANCE / max_exact)
                         * (NUM_BUCKETS - max_exact)).astype(jnp.int32)
    large = jnp.minimum(large, NUM_BUCKETS - 1)
    return jnp.where(n < max_exact, n, large)


def _diff_attn_kernel(kinds, pair_ref, tab_ref, lq1_ref, lk1_ref, lq2_ref, lk2_ref,
                      q_ref, k_ref, v_ref, g_ref, w_ref, o_ref, w_out_ref,
                      bias_near_ref, bias_diag_ref, vt_ref, *tile_and_state_refs):
    h = pl.program_id(0)
    b = pl.program_id(1)
    blk = ATTN_BLOCK
    w_out_ref[...] = w_ref[...].astype(w_out_ref.dtype)
    key = lax.broadcasted_iota(jnp.int32, (blk, blk), 0)
    qry = lax.broadcasted_iota(jnp.int32, (blk, blk), 1)
    u_refs = tile_and_state_refs[:PIPE_BUFS]
    bm_refs = tile_and_state_refs[PIPE_BUFS:2 * PIPE_BUFS]
    m_ref, l_ref, acc_ref = tile_and_state_refs[2 * PIPE_BUFS:]
    far_bias = tab_ref[NUM_BUCKETS - 1, h]

    @pl.when(b == 0)
    def _build_bias():
        for ref, offset in ((bias_diag_ref, 0), (bias_near_ref, blk)):
            bucket = _t5_bucket(jnp.maximum(qry - key + offset, 0))
            bias = jnp.zeros((blk, blk), jnp.float32)
            for t in range(NUM_BUCKETS):
                bias = jnp.where(bucket == t, tab_ref[t, h], bias)
            ref[...] = (bias - far_bias) * LOG2E

    _store_transposed(v_ref, vt_ref)
    m_ref[...] = jnp.full(m_ref.shape, NEG_INF, jnp.float32)
    l_ref[...] = jnp.zeros(l_ref.shape, jnp.float32)
    acc_ref[...] = jnp.zeros(acc_ref.shape, jnp.float32)

    def starts(t):
        return (pl.multiple_of(pair_ref[0, t] * blk, blk), pl.multiple_of(pair_ref[1, t] * blk, blk))

    def stage_a(t, kind, par):
        qs, ks = starts(t)
        for mp in range(2):
            cols = slice(mp * HEAD_DIM, (mp + 1) * HEAD_DIM)
            u = _scores_t(k_ref[pl.ds(ks, blk), cols], q_ref[pl.ds(qs, blk), cols]) * (SCALE * LOG2E)
            if kind == NEAR:
                u = u + bias_near_ref[...]
            elif kind == DIAG:
                u = jnp.where(qry >= key, u + bias_diag_ref[...], NEG_INF)
            u_refs[par][mp] = u
            bm_refs[par][mp] = jnp.max(u, axis=0, keepdims=True)

    def stage_b(t, par):
        qs, ks = starts(t)
        v_t = vt_ref[:, pl.ds(ks, blk)]
        for mp in range(2):
            _softmax_pv_update(u_refs[par][mp], bm_refs[par][mp], far_bias * LOG2E, v_t,
                               m_ref.at[mp], l_ref.at[mp], acc_ref.at[mp], qs)

    _run_pipeline(kinds, stage_a, stage_b)

    lam = (jnp.exp(jnp.sum(lq1_ref[...] * lk1_ref[...], axis=-1, keepdims=True))
           - jnp.exp(jnp.sum(lq2_ref[...] * lk2_ref[...], axis=-1, keepdims=True))
           + LAM_INIT)
    for c in range(0, o_ref.shape[0], blk):
        cols = slice(c, c + blk)
        o_t = (acc_ref[0, :, cols] * (1.0 / l_ref[0, :, cols])
               - lam * (acc_ref[1, :, cols] * (1.0 / l_ref[1, :, cols])))
        y_t = o_t * lax.rsqrt(jnp.mean(o_t * o_t, axis=0, keepdims=True) + EPS)
        y_t = (y_t * g_ref[...]) * (1.0 - LAM_INIT)
        o_ref[cols, :] = y_t.T.astype(o_ref.dtype)


def _slab_spec(weight, n_steps, step_of):
    rows, cols = weight.shape[-2:]
    slab = rows // n_steps
    assert slab * n_steps == rows and slab % 16 == 0
    if weight.ndim == 3:
        return pl.BlockSpec((None, slab, cols), lambda *idx: (0, step_of(*idx), 0))
    return pl.BlockSpec((slab, cols), lambda *idx: (step_of(*idx), 0))


def _bf16_like(weight):
    return jax.ShapeDtypeStruct(weight.shape[-2:], jnp.bfloat16)


def _diff_attention(qkv, tab, lq1, lk1, lq2, lk2, g_col, w_conv):
    bsz, seq, _ = qkv.shape
    blk = ATTN_BLOCK
    w = 2 * HEAD_DIM
    k_off = DIFF_WIDTH // w
    v_off = 2 * DIFF_WIDTH // w
    kinds, pairs = _pair_schedule(seq // blk, with_near=True)
    smem = pl.BlockSpec(memory_space=pltpu.SMEM)
    vec = pl.BlockSpec((1, HEAD_DIM), lambda h, b: (0, 0))
    n_steps = N_DIFF_HEADS * bsz

    def step_of(h, b):
        return h * bsz + b

    return pl.pallas_call(
        functools.partial(_diff_attn_kernel, kinds),
        grid=(N_DIFF_HEADS, bsz),
        in_specs=[smem, smem, vec, vec, vec, vec,
                  pl.BlockSpec((None, seq, w), lambda h, b: (b, 0, h)),
                  pl.BlockSpec((None, seq, w), lambda h, b: (b, 0, k_off + h)),
                  pl.BlockSpec((None, seq, w), lambda h, b: (b, 0, v_off + h)),
                  pl.BlockSpec((w, 1), lambda h, b: (0, 0)),
                  _slab_spec(w_conv, n_steps, step_of)],
        out_specs=[pl.BlockSpec((None, seq, w), lambda h, b: (b, 0, h)),
                   _slab_spec(_bf16_like(w_conv), n_steps, step_of)],
        out_shape=[jax.ShapeDtypeStruct((bsz, seq, DIFF_WIDTH), jnp.bfloat16), _bf16_like(w_conv)],
        scratch_shapes=([pltpu.VMEM((blk, blk), jnp.float32),
                         pltpu.VMEM((blk, blk), jnp.float32),
                         pltpu.VMEM((w, seq), jnp.bfloat16)]
                        + [pltpu.VMEM((2, blk, blk), jnp.float32)] * PIPE_BUFS
                        + [pltpu.VMEM((2, 1, blk), jnp.float32)] * PIPE_BUFS
                        + [pltpu.VMEM((2, 1, seq), jnp.float32),
                           pltpu.VMEM((2, 1, seq), jnp.float32),
                           pltpu.VMEM((2, w, seq), jnp.float32)]),
        compiler_params=_params(("arbitrary", "arbitrary")),
        name="diff_attention",
    )(jnp.asarray(pairs), tab, lq1, lk1, lq2, lk2, qkv, qkv, qkv, g_col, w_conv)


def _fox_attn_kernel(kinds, pair_ref, q_ref, k_ref, v_ref, cum_ref, wa_ref, wb_ref,
                     o_ref, wa_out_ref, wb_out_ref,
                     vt_ref, q2_ref, k2_ref, *tile_and_state_refs):
    blk = ATTN_BLOCK
    d = HEAD_DIM
    f32, bf16 = jnp.float32, jnp.bfloat16
    key = lax.broadcasted_iota(jnp.int32, (blk, blk), 0)
    qry = lax.broadcasted_iota(jnp.int32, (blk, blk), 1)
    lane = lax.broadcasted_iota(jnp.int32, (blk, LANES), 1)
    u_refs = tile_and_state_refs[:PIPE_BUFS]
    bm_refs = tile_and_state_refs[PIPE_BUFS:2 * PIPE_BUFS]
    m_ref, acc_ref = tile_and_state_refs[2 * PIPE_BUFS:]

    wa_out_ref[...] = wa_ref[...].astype(wa_out_ref.dtype)
    wb_out_ref[...] = wb_ref[...].astype(wb_out_ref.dtype)

    _store_transposed(v_ref, vt_ref.at[:d, :])
    vt_ref[d:, :] = jnp.ones((vt_ref.shape[0] - d, vt_ref.shape[1]), vt_ref.dtype)
    for c in range(0, q_ref.shape[0], blk):
        rows = slice(c, c + blk)
        x = jnp.broadcast_to(cum_ref[:, rows] * (-1.0 / SCALE), (LANES, blk)).T
        hi = x.astype(bf16).astype(f32)
        mid = (x - hi).astype(bf16).astype(f32)
        lo = ((x - hi) - mid).astype(bf16).astype(f32)
        aug = jnp.where(lane == 0, hi, jnp.where(lane == 1, mid, jnp.where(lane == 2, lo, 0.0)))
        k2_ref[rows, :d] = k_ref[rows, :]
        k2_ref[rows, d:] = aug.astype(bf16)
        q2_ref[rows, :d] = q_ref[rows, :]
        q2_ref[rows, d:] = jnp.where(lane < 3, 1.0, 0.0).astype(bf16)
    m_ref[...] = jnp.full(m_ref.shape, NEG_INF, jnp.float32)
    acc_ref[...] = jnp.zeros(acc_ref.shape, jnp.float32)

    def starts(t):
        return (pl.multiple_of(pair_ref[0, t] * blk, blk), pl.multiple_of(pair_ref[1, t] * blk, blk))

    def stage_a(t, kind, buf):
        qs, ks = starts(t)
        u = _scores_t(k2_ref[pl.ds(ks, blk), :], q2_ref[pl.ds(qs, blk), :]) * (SCALE * LOG2E)
        if kind == DIAG:
            u = jnp.where(qry >= key, u, NEG_INF)
        u_refs[buf][...] = u
        bm_refs[buf][...] = jnp.max(u, axis=0, keepdims=True)

    def stage_b(t, buf):
        qs, ks = starts(t)
        cq = cum_ref[:, pl.ds(qs, blk)] * LOG2E
        _softmax_pv_update(u_refs[buf][...], bm_refs[buf][...], cq, vt_ref[:, pl.ds(ks, blk)],
                           m_ref, None, acc_ref, qs)

    _run_pipeline(kinds, stage_a, stage_b)

    for c in range(0, o_ref.shape[0], blk):
        cols = slice(c, c + blk)
        o_t = acc_ref[:d, cols] * (1.0 / acc_ref[d:d + 1, cols])
        o_ref[cols, :] = o_t.T.astype(o_ref.dtype)


def _fox_attention(qkv, cum, w_conv_a, w_conv_b):
    bsz, seq, _ = qkv.shape
    blk = ATTN_BLOCK
    q_off = 3 * DIFF_WIDTH // HEAD_DIM
    k_off = q_off + N_FOX_HEADS
    v_off = k_off + N_FOX_HEADS
    kinds, pairs = _pair_schedule(seq // blk, with_near=False)
    n_steps = N_FOX_HEADS * bsz

    def step_of(h, b):
        return h * bsz + b

    return pl.pallas_call(
        functools.partial(_fox_attn_kernel, kinds),
        grid=(N_FOX_HEADS, bsz),
        in_specs=[pl.BlockSpec(memory_space=pltpu.SMEM),
                  pl.BlockSpec((None, seq, HEAD_DIM), lambda h, b: (b, 0, q_off + h)),
                  pl.BlockSpec((None, seq, HEAD_DIM), lambda h, b: (b, 0, k_off + h)),
                  pl.BlockSpec((None, seq, HEAD_DIM), lambda h, b: (b, 0, v_off + h)),
                  pl.BlockSpec((None, None, 1, seq), lambda h, b: (h, b, 0, 0)),
                  _slab_spec(w_conv_a, n_steps, step_of),
                  _slab_spec(w_conv_b, n_steps, step_of)],
        out_specs=[pl.BlockSpec((None, seq, HEAD_DIM), lambda h, b: (b, 0, h)),
                   _slab_spec(_bf16_like(w_conv_a), n_steps, step_of),
                   _slab_spec(_bf16_like(w_conv_b), n_steps, step_of)],
        out_shape=[jax.ShapeDtypeStruct((bsz, seq, FOX_WIDTH), jnp.bfloat16),
                   _bf16_like(w_conv_a), _bf16_like(w_conv_b)],
        scratch_shapes=([pltpu.VMEM((HEAD_DIM + FOX_SUM_ROWS, seq), jnp.bfloat16),
                         pltpu.VMEM((seq, 2 * HEAD_DIM), jnp.bfloat16),
                         pltpu.VMEM((seq, 2 * HEAD_DIM), jnp.bfloat16)]
                        + [pltpu.VMEM((blk, blk), jnp.float32)] * PIPE_BUFS
                        + [pltpu.VMEM((1, blk), jnp.float32)] * PIPE_BUFS
                        + [pltpu.VMEM((1, seq), jnp.float32),
                           pltpu.VMEM((HEAD_DIM + FOX_SUM_ROWS, seq), jnp.float32)]),
        compiler_params=_params(("arbitrary", "arbitrary")),
        name="fox_attention",
    )(jnp.asarray(pairs), qkv, qkv, qkv, cum, w_conv_a, w_conv_b)


def _out_proj_kernel(ad_ref, af_ref, w_ref, res_ref, g_ref, x1_ref, xg_ref, ssq_ref):
    j = pl.program_id(1)
    kd = ad_ref.shape[1]
    wd = w_ref[:kd, :]
    wf = w_ref[kd:, :]
    parts = []
    for c in range(0, ad_ref.shape[0], MATMUL_ROW_CHUNK):
        rows = slice(c, c + MATMUL_ROW_CHUNK)
        y = (res_ref[rows, :] + jnp.dot(ad_ref[rows, :], wd, preferred_element_type=jnp.float32)
             + jnp.dot(af_ref[rows, :], wf, preferred_element_type=jnp.float32))
        x1_ref[rows, :] = y
        xg_ref[rows, :] = (y * g_ref[...]).astype(xg_ref.dtype)
        sq = y * y
        part = sq[:, :LANES]
        for lc in range(LANES, sq.shape[1], LANES):
            part = part + sq[:, lc:lc + LANES]
        parts.append(part)
    part = jnp.concatenate(parts, axis=0)

    @pl.when(j == 0)
    def _first():
        ssq_ref[...] = part

    @pl.when(j > 0)
    def _rest():
        ssq_ref[...] += part


def _out_proj(a_diff, a_fox, w, res, g, bm=1024, bn=512):
    m, kd = a_diff.shape
    kf = a_fox.shape[1]
    d = w.shape[-1]
    return pl.pallas_call(
        _out_proj_kernel,
        grid=(m // bm, d // bn),
        in_specs=[pl.BlockSpec((bm, kd), lambda i, j: (i, 0)),
                  pl.BlockSpec((bm, kf), lambda i, j: (i, 0)),
                  pl.BlockSpec((kd + kf, bn), lambda i, j: (0, j)),
                  pl.BlockSpec((bm, bn), lambda i, j: (i, j)),
                  pl.BlockSpec((1, bn), lambda i, j: (0, j))],
        out_specs=[pl.BlockSpec((bm, bn), lambda i, j: (i, j)),
                   pl.BlockSpec((bm, bn), lambda i, j: (i, j)),
                   pl.BlockSpec((bm, LANES), lambda i, j: (i, 0))],
        out_shape=[jax.ShapeDtypeStruct((m, d), jnp.float32),
                   jax.ShapeDtypeStruct((m, d), jnp.bfloat16),
                   jax.ShapeDtypeStruct((m, LANES), jnp.float32)],
        compiler_params=_params(("arbitrary", "arbitrary")),
        name="out_proj",
    )(a_diff, a_fox, w, res, g)


def _down_kernel(a_ref, w_ref, res_ref, g_ref, o_ref, *, n_k, k_last, n_res):
    k = pl.program_id(1)
    bk = a_ref.shape[1]
    rc = res_ref.shape[1]

    def accumulate(k_len, first):
        a = a_ref[:, :k_len]
        for c in range(0, o_ref.shape[1], PROJ_COL_CHUNK):
            cols = slice(c, c + PROJ_COL_CHUNK)
            prod = jnp.dot(a, w_ref[:k_len, cols], preferred_element_type=jnp.float32)
            o_ref[:, cols] = prod if first else o_ref[:, cols] + prod

    pl.when(k == 0)(lambda: accumulate(bk, True))
    pl.when((k > 0) & (k < n_k - 1))(lambda: accumulate(bk, False))

    @pl.when(k < n_res)
    def _add_residual_tile():
        cols = pl.ds(pl.multiple_of(k * rc, rc), rc)
        o_ref[:, cols] += res_ref[...]

    @pl.when(k == n_k - 1)
    def _finish():
        accumulate(k_last, False)
        for r in range(0, o_ref.shape[0], PROJ_ROW_CHUNK):
            rows = slice(r, r + PROJ_ROW_CHUNK)
            x = o_ref[rows, :]
            o_ref[rows, :] = x * lax.rsqrt(jnp.mean(x * x, axis=-1, keepdims=True) + EPS) * g_ref[...]


def _down_proj_norm(a, w, res, g, bm=1024, bk=DOWN_K_TILE, rc=DOWN_RES_TILE):
    m, kdim = a.shape
    d = w.shape[1]
    n_k = pl.cdiv(kdim, bk)
    k_last = kdim - (n_k - 1) * bk
    n_res = d // rc
    assert n_res * rc == d and n_res <= n_k - 1
    row_block = pl.BlockSpec((bm, d), lambda i, k: (i, 0))
    return pl.pallas_call(
        functools.partial(_down_kernel, n_k=n_k, k_last=k_last, n_res=n_res),
        grid=(m // bm, n_k),
        in_specs=[pl.BlockSpec((bm, bk), lambda i, k: (i, k)),
                  pl.BlockSpec((bk, d), lambda i, k: (k, 0)),
                  pl.BlockSpec((bm, rc), lambda i, k: (i, jnp.minimum(k, n_res - 1))),
                  pl.BlockSpec((1, d), lambda i, k: (0, 0))],
        out_specs=row_block,
        out_shape=jax.ShapeDtypeStruct((m, d), jnp.float32),
        compiler_params=_params(("arbitrary", "arbitrary")),
        name="ffn_down",
    )(a, w, res, g)


def _gate_up_kernel(xg_ref, ssq_ref, wg_ref, wu_ref, wd_ref, o_ref, wd_out_ref, r_ref):
    @pl.when(pl.program_id(1) == 0)
    def _row_scale():
        mean_sq = jnp.sum(ssq_ref[...], axis=-1, keepdims=True) * (1.0 / xg_ref.shape[1])
        r_ref[...] = jnp.broadcast_to(lax.rsqrt(mean_sq + EPS), r_ref.shape)

    wg = wg_ref[...]
    wu = wu_ref[...]
    for c in range(0, xg_ref.shape[0], MATMUL_ROW_CHUNK):
        rows = slice(c, c + MATMUL_ROW_CHUNK)
        xg = xg_ref[rows, :]
        r = jnp.concatenate([r_ref[rows, :]] * (o_ref.shape[1] // LANES), axis=1)
        gate = jnp.dot(xg, wg, preferred_element_type=jnp.float32) * r
        up = jnp.dot(xg, wu, preferred_element_type=jnp.float32) * r
        o_ref[rows, :] = (gate * jax.nn.sigmoid(gate) * up).astype(o_ref.dtype)
    wd_out_ref[...] = wd_ref[...].astype(wd_out_ref.dtype)


def _gate_up(xg, ssq, wg, wu, wd, bm=ROW_TILE, bn=FF_TILE):
    m, k = xg.shape
    f_rows, d = wd.shape[-2:]
    n = wg.shape[1]
    n_j = n // bn
    n_steps = (m // bm) * n_j

    def step_of(i, j):
        return i * n_j + j

    return pl.pallas_call(
        _gate_up_kernel,
        grid=(m // bm, n_j),
        in_specs=[pl.BlockSpec((bm, k), lambda i, j: (i, 0)),
                  pl.BlockSpec((bm, LANES), lambda i, j: (i, 0)),
                  pl.BlockSpec((k, bn), lambda i, j: (0, j)),
                  pl.BlockSpec((k, bn), lambda i, j: (0, j)),
                  _slab_spec(wd, n_steps, step_of)],
        out_specs=[pl.BlockSpec((bm, bn), lambda i, j: (i, j)),
                   _slab_spec(_bf16_like(wd), n_steps, step_of)],
        out_shape=[jax.ShapeDtypeStruct((m, n), jnp.bfloat16), _bf16_like(wd)],
        scratch_shapes=[pltpu.VMEM((bm, LANES), jnp.float32)],
        compiler_params=_params(("arbitrary", "arbitrary")),
        name="ffn_gate_up",
    )(xg, ssq, wg, wu, wd)


def kernel(x, attn_norm_g, w_in, b_f, lambda_q1, lambda_k1, lambda_q2, lambda_k2, rel_bias_table,
           diff_subln_g, w_o, ffn_norm_g, w_gate, w_up, w_down, final_norm_g):
    bsz, seq, d = x.shape
    m = bsz * seq
    x2d = x.reshape(m, d)

    w_in_t = jnp.swapaxes(w_in, 1, 2)
    h, fl_t = _norm_and_gate_logits(x2d, attn_norm_g[0].reshape(1, d), w_in_t, QKV_COLS, N_FOX_HEADS)
    qkv = _matmul(h, w_in_t, QKV_COLS, name="in_proj").reshape(bsz, seq, QKV_COLS)

    cum = _cum_log_forget(fl_t, b_f[0].reshape(N_FOX_HEADS, 1), seq)
    cum = cum.reshape(N_FOX_HEADS, bsz, 1, seq)

    o_diff, w_o_b = _diff_attention(
        qkv, rel_bias_table,
        lambda_q1[0].reshape(1, HEAD_DIM), lambda_k1[0].reshape(1, HEAD_DIM),
        lambda_q2[0].reshape(1, HEAD_DIM), lambda_k2[0].reshape(1, HEAD_DIM),
        diff_subln_g[0].reshape(2 * HEAD_DIM, 1), w_o)
    o_fox, w_gate_b, w_up_b = _fox_attention(qkv, cum, w_gate, w_up)

    x1, x1g, ssq = _out_proj(o_diff.reshape(m, DIFF_WIDTH), o_fox.reshape(m, FOX_WIDTH), w_o_b, x2d,
                             ffn_norm_g[0].reshape(1, d))
    act, w_down_b = _gate_up(x1g, ssq, w_gate_b, w_up_b, w_down)
    out = _down_proj_norm(act, w_down_b, x1, final_norm_g.reshape(1, d))
    return out.reshape(bsz, seq, d)
```

```python
import functools
import math

import numpy as np

import jax
import jax.numpy as jnp
from jax import lax
from jax.experimental import pallas as pl
from jax.experimental.pallas import tpu as pltpu

D_MODEL = 4096
HEAD_DIM = 128
N_DIFF_HEADS = D_MODEL // (4 * HEAD_DIM)
N_FOX_HEADS = D_MODEL // (2 * HEAD_DIM)
DIFF_WIDTH = N_DIFF_HEADS * 2 * HEAD_DIM
FOX_WIDTH = N_FOX_HEADS * HEAD_DIM
QKV_COLS = 3 * DIFF_WIDTH + 3 * FOX_WIDTH
D_FF = ((8 * D_MODEL + 3 * 256 - 1) // (3 * 256)) * 256
NUM_BUCKETS = 32
MAX_DISTANCE = 128
EPS = 1e-6
NEG_INF = -1e30
LAM_INIT = 0.8 - 0.6 * math.exp(-0.3 * 0)
SCALE = HEAD_DIM ** -0.5
LOG2E = math.log2(math.e)

LANES = 128
BF16_SUBLANES = 16
ROW_TILE = 2048
FF_TILE = 256
DOWN_K_TILE = 512
DOWN_RES_TILE = 256
VMEM_LIMIT = 56 * 1024 * 1024

ATTN_BLOCK = 512
PIPE_DEPTH = 2
PIPE_BUFS = PIPE_DEPTH + 1
FOX_SUM_ROWS = BF16_SUBLANES
MATMUL_ROW_CHUNK = 512
PROJ_COL_CHUNK = 1024
PROJ_ROW_CHUNK = 128

FAR, NEAR, DIAG = "far", "near", "diag"


def _params(sem, vmem=VMEM_LIMIT):
    return pltpu.CompilerParams(dimension_semantics=sem, vmem_limit_bytes=vmem)


def _row_chunks(n_rows):
    return [slice(s, s + MATMUL_ROW_CHUNK) for s in range(0, n_rows, MATMUL_ROW_CHUNK)]


def _norm_kernel(x_ref, g_ref, wf_ref, h_ref, fl_ref):
    x = x_ref[...]
    y = x * lax.rsqrt(jnp.mean(x * x, axis=-1, keepdims=True) + EPS)
    h = (y * g_ref[...]).astype(jnp.bfloat16)
    h_ref[...] = h
    fl_ref[...] = lax.dot_general(wf_ref[...].astype(jnp.bfloat16), h, (((1,), (1,)), ((), ())),
                                  preferred_element_type=jnp.float32)


def _norm_and_gate_logits(x2d, g, w_in_t, gate_row, nh, bm=512):
    m, d = x2d.shape
    assert gate_row % nh == 0 and gate_row + nh == w_in_t.shape[1]
    return pl.pallas_call(
        _norm_kernel,
        grid=(m // bm,),
        in_specs=[pl.BlockSpec((bm, d), lambda i: (i, 0)),
                  pl.BlockSpec((1, d), lambda i: (0, 0)),
                  pl.BlockSpec((None, nh, d), lambda i: (0, gate_row // nh, 0))],
        out_specs=[pl.BlockSpec((bm, d), lambda i: (i, 0)),
                   pl.BlockSpec((nh, bm), lambda i: (0, i))],
        out_shape=[jax.ShapeDtypeStruct((m, d), jnp.bfloat16),
                   jax.ShapeDtypeStruct((nh, m), jnp.float32)],
        compiler_params=_params(("arbitrary",)),
        name="attn_norm",
    )(x2d, g, w_in_t)


def _matmul_kernel(a_ref, wt_ref, o_ref):
    o_ref[...] = lax.dot_general(a_ref[...], wt_ref[...].astype(a_ref.dtype),
                                 (((1,), (1,)), ((), ())),
                                 preferred_element_type=jnp.float32).astype(o_ref.dtype)


def _matmul(a, w_t, n_out, bm=ROW_TILE, bn=512, name="matmul"):
    m, k = a.shape
    return pl.pallas_call(
        _matmul_kernel,
        grid=(m // bm, n_out // bn),
        in_specs=[pl.BlockSpec((bm, k), lambda i, j: (i, 0), pipeline_mode=pl.Buffered(1)),
                  pl.BlockSpec((None, bn, k), lambda i, j: (0, j, 0))],
        out_specs=pl.BlockSpec((bm, bn), lambda i, j: (i, j)),
        out_shape=jax.ShapeDtypeStruct((m, n_out), jnp.bfloat16),
        compiler_params=_params(("arbitrary", "arbitrary")),
        name=name,
    )(a, w_t)


def _cumsum_kernel(fl_ref, bf_ref, c_ref):
    z = fl_ref[...] + bf_ref[...]
    x = jnp.minimum(z, 0.0) - jnp.log(1.0 + jnp.exp(-jnp.abs(z)))
    n = x.shape[-1]
    pos = lax.broadcasted_iota(jnp.int32, x.shape, 1)
    shift = 1
    while shift < n:
        x = x + jnp.where(pos >= shift, pltpu.roll(x, shift, 1), 0.0)
        shift *= 2
    c_ref[...] = x


def _cum_log_forget(fl_t, b_f, seq):
    nh, m = fl_t.shape
    return pl.pallas_call(
        _cumsum_kernel,
        grid=(m // seq,),
        in_specs=[pl.BlockSpec((nh, seq), lambda b: (0, b)),
                  pl.BlockSpec((nh, 1), lambda b: (0, 0))],
        out_specs=pl.BlockSpec((nh, seq), lambda b: (0, b)),
        out_shape=jax.ShapeDtypeStruct((nh, m), jnp.float32),
        compiler_params=_params(("arbitrary",)),
        name="cum_log_forget",
    )(fl_t, b_f)


def _scores_t(k, q):
    return lax.dot_general(k, q, (((1,), (1,)), ((), ())), preferred_element_type=jnp.float32)


def _softmax_pv_update(u, blk_max, shift, v_t, m_ref, l_ref, acc_ref, qs):
    blk = u.shape[1]
    m_old = m_ref[:, pl.ds(qs, blk)]
    m_new = jnp.maximum(m_old, blk_max + shift)
    alpha = jnp.exp2(m_old - m_new)
    p = jnp.exp2(u + (shift - m_new))
    if l_ref is not None:
        l_ref[:, pl.ds(qs, blk)] = (alpha * l_ref[:, pl.ds(qs, blk)]
                                    + jnp.sum(p, axis=0, keepdims=True))
    acc_ref[:, pl.ds(qs, blk)] = alpha * acc_ref[:, pl.ds(qs, blk)] + jnp.dot(
        v_t, p.astype(v_t.dtype), preferred_element_type=jnp.float32)
    m_ref[:, pl.ds(qs, blk)] = m_new


def _pair_schedule(n_blocks, with_near):
    far, near, diag = [], [], []
    for i in range(n_blocks):
        for j in range(i + 1):
            if j == i:
                diag.append((i, j))
            elif with_near and j == i - 1:
                near.append((i, j))
            else:
                far.append((i, j))
    pairs = far + near + diag
    kinds = [FAR] * len(far) + [NEAR] * len(near) + [DIAG] * len(diag)
    return kinds, np.asarray(pairs, np.int32).T.copy()


def _run_pipeline(kinds, stage_a, stage_b, unroll_all=False):
    n = len(kinds)
    ahead_kind = list(kinds[PIPE_DEPTH:]) + [None] * PIPE_DEPTH
    for t in range(min(PIPE_DEPTH, n)):
        stage_a(t, kinds[t], t % PIPE_BUFS)
    s = 0
    while s < n:
        e = s
        while e < n and ahead_kind[e] == ahead_kind[s]:
            e += 1
        kind, n_loop = ahead_kind[s], 0 if unroll_all else (e - s) // PIPE_BUFS

        def one(step, buf):
            if kind is not None:
                stage_a(step + PIPE_DEPTH, kind, (buf + PIPE_DEPTH) % PIPE_BUFS)
            stage_b(step, buf)

        if n_loop:
            def body(r, carry):
                base = s + PIPE_BUFS * r
                for i in range(PIPE_BUFS):
                    one(base + i, (s + i) % PIPE_BUFS)
                return carry

            lax.fori_loop(0, n_loop, body, 0)
        for step in range(s + n_loop * PIPE_BUFS, e):
            one(step, step % PIPE_BUFS)
        s = e


def _store_transposed(src_ref, dst_ref):
    for c in range(0, src_ref.shape[0], ATTN_BLOCK):
        dst_ref[:, c:c + ATTN_BLOCK] = (
            src_ref[c:c + ATTN_BLOCK, :].astype(jnp.float32).T.astype(dst_ref.dtype))


def _t5_bucket(n):
    max_exact = NUM_BUCKETS // 2
    nf = jnp.maximum(n, 1).astype(jnp.float32)
    large = max_exact + (jnp.log(nf / max_exact) / math.log(MAX_DISTANCE / max_exact)
                         * (NUM_BUCKETS - max_exact)).astype(jnp.int32)
    large = jnp.minimum(large, NUM_BUCKETS - 1)
    return jnp.where(n < max_exact, n, large)


def _diff_attn_kernel(kinds, pair_ref, tab_ref, lq1_ref, lk1_ref, lq2_ref, lk2_ref,
                      q_ref, k_ref, v_ref, g_ref, w_ref, o_ref, w_out_ref,
                      bias_near_ref, bias_diag_ref, vt_ref, *tile_and_state_refs):
    h = pl.program_id(0)
    b = pl.program_id(1)
    blk = ATTN_BLOCK
    w_out_ref[...] = w_ref[...].astype(w_out_ref.dtype)
    key = lax.broadcasted_iota(jnp.int32, (blk, blk), 0)
    qry = lax.broadcasted_iota(jnp.int32, (blk, blk), 1)
    u_refs = tile_and_state_refs[:PIPE_BUFS]
    bm_refs = tile_and_state_refs[PIPE_BUFS:2 * PIPE_BUFS]
    m_ref, l_ref, acc_ref = tile_and_state_refs[2 * PIPE_BUFS:]
    far_bias = tab_ref[NUM_BUCKETS - 1, h]

    @pl.when(b == 0)
    def _build_bias():
        for ref, offset in ((bias_diag_ref, 0), (bias_near_ref, blk)):
            bucket = _t5_bucket(jnp.maximum(qry - key + offset, 0))
            bias = jnp.zeros((blk, blk), jnp.float32)
            for t in range(NUM_BUCKETS):
                bias = jnp.where(bucket == t, tab_ref[t, h], bias)
            ref[...] = (bias - far_bias) * LOG2E

    _store_transposed(v_ref, vt_ref)
    m_ref[...] = jnp.full(m_ref.shape, NEG_INF, jnp.float32)
    l_ref[...] = jnp.zeros(l_ref.shape, jnp.float32)
    acc_ref[...] = jnp.zeros(acc_ref.shape, jnp.float32)

    def starts(t):
        return (pl.multiple_of(pair_ref[0, t] * blk, blk), pl.multiple_of(pair_ref[1, t] * blk, blk))

    def stage_a(t, kind, par):
        qs, ks = starts(t)
        for mp in range(2):
            cols = slice(mp * HEAD_DIM, (mp + 1) * HEAD_DIM)
            u = _scores_t(k_ref[pl.ds(ks, blk), cols], q_ref[pl.ds(qs, blk), cols]) * (SCALE * LOG2E)
            if kind == NEAR:
                u = u + bias_near_ref[...]
            elif kind == DIAG:
                u = jnp.where(qry >= key, u + bias_diag_ref[...], NEG_INF)
            u_refs[par][mp] = u
            bm_refs[par][mp] = jnp.max(u, axis=0, keepdims=True)

    def stage_b(t, par):
        qs, ks = starts(t)
        v_t = vt_ref[:, pl.ds(ks, blk)]
        for mp in range(2):
            _softmax_pv_update(u_refs[par][mp], bm_refs[par][mp], far_bias * LOG2E, v_t,
                               m_ref.at[mp], l_ref.at[mp], acc_ref.at[mp], qs)

    _run_pipeline(kinds, stage_a, stage_b)

    lam = (jnp.exp(jnp.sum(lq1_ref[...] * lk1_ref[...], axis=-1, keepdims=True))
           - jnp.exp(jnp.sum(lq2_ref[...] * lk2_ref[...], axis=-1, keepdims=True))
           + LAM_INIT)
    for c in range(0, o_ref.shape[0], blk):
        cols = slice(c, c + blk)
        o_t = (acc_ref[0, :, cols] * (1.0 / l_ref[0, :, cols])
               - lam * (acc_ref[1, :, cols] * (1.0 / l_ref[1, :, cols])))
        y_t = o_t * lax.rsqrt(jnp.mean(o_t * o_t, axis=0, keepdims=True) + EPS)
        y_t = (y_t * g_ref[...]) * (1.0 - LAM_INIT)
        o_ref[cols, :] = y_t.T.astype(o_ref.dtype)


def _slab_spec(weight, n_steps, step_of):
    rows, cols = weight.shape[-2:]
    slab = rows // n_steps
    assert slab * n_steps == rows and slab % BF16_SUBLANES == 0
    if weight.ndim == 3:
        return pl.BlockSpec((None, slab, cols), lambda *idx: (0, step_of(*idx), 0))
    return pl.BlockSpec((slab, cols), lambda *idx: (step_of(*idx), 0))


def _bf16_like(weight):
    return jax.ShapeDtypeStruct(weight.shape[-2:], jnp.bfloat16)


def _diff_attention(qkv, tab, lq1, lk1, lq2, lk2, g_col, w_conv):
    bsz, seq, _ = qkv.shape
    blk = ATTN_BLOCK
    w = 2 * HEAD_DIM
    k_off = DIFF_WIDTH // w
    v_off = 2 * DIFF_WIDTH // w
    kinds, pairs = _pair_schedule(seq // blk, with_near=True)
    smem = pl.BlockSpec(memory_space=pltpu.SMEM)
    vec = pl.BlockSpec((1, HEAD_DIM), lambda h, b: (0, 0))
    n_steps = N_DIFF_HEADS * bsz

    def step_of(h, b):
        return h * bsz + b

    return pl.pallas_call(
        functools.partial(_diff_attn_kernel, kinds),
        grid=(N_DIFF_HEADS, bsz),
        in_specs=[smem, smem, vec, vec, vec, vec,
                  pl.BlockSpec((None, seq, w), lambda h, b: (b, 0, h)),
                  pl.BlockSpec((None, seq, w), lambda h, b: (b, 0, k_off + h)),
                  pl.BlockSpec((None, seq, w), lambda h, b: (b, 0, v_off + h)),
                  pl.BlockSpec((w, 1), lambda h, b: (0, 0)),
                  _slab_spec(w_conv, n_steps, step_of)],
        out_specs=[pl.BlockSpec((None, seq, w), lambda h, b: (b, 0, h)),
                   _slab_spec(_bf16_like(w_conv), n_steps, step_of)],
        out_shape=[jax.ShapeDtypeStruct((bsz, seq, DIFF_WIDTH), jnp.bfloat16), _bf16_like(w_conv)],
        scratch_shapes=([pltpu.VMEM((blk, blk), jnp.float32),
                         pltpu.VMEM((blk, blk), jnp.float32),
                         pltpu.VMEM((w, seq), jnp.bfloat16)]
                        + [pltpu.VMEM((2, blk, blk), jnp.float32)] * PIPE_BUFS
                        + [pltpu.VMEM((2, 1, blk), jnp.float32)] * PIPE_BUFS
                        + [pltpu.VMEM((2, 1, seq), jnp.float32),
                           pltpu.VMEM((2, 1, seq), jnp.float32),
                           pltpu.VMEM((2, w, seq), jnp.float32)]),
        compiler_params=_params(("arbitrary", "arbitrary")),
        name="diff_attention",
    )(jnp.asarray(pairs), tab, lq1, lk1, lq2, lk2, qkv, qkv, qkv, g_col, w_conv)


def _fox_attn_kernel(kinds, pairs, q_ref, k_ref, v_ref, cum_ref, wa_ref, wb_ref,
                     o_ref, wa_out_ref, wb_out_ref,
                     vt_ref, q2_ref, k2_ref, *tile_and_state_refs):
    blk = ATTN_BLOCK
    d = HEAD_DIM
    f32, bf16 = jnp.float32, jnp.bfloat16
    key = lax.broadcasted_iota(jnp.int32, (blk, blk), 0)
    qry = lax.broadcasted_iota(jnp.int32, (blk, blk), 1)
    lane = lax.broadcasted_iota(jnp.int32, (blk, LANES), 1)
    u_refs = tile_and_state_refs[:PIPE_BUFS]
    bm_refs = tile_and_state_refs[PIPE_BUFS:2 * PIPE_BUFS]
    m_ref, acc_ref = tile_and_state_refs[2 * PIPE_BUFS:]

    wa_out_ref[...] = wa_ref[...].astype(wa_out_ref.dtype)
    wb_out_ref[...] = wb_ref[...].astype(wb_out_ref.dtype)

    _store_transposed(v_ref, vt_ref.at[:d, :])
    vt_ref[d:, :] = jnp.ones((vt_ref.shape[0] - d, vt_ref.shape[1]), vt_ref.dtype)
    for c in range(0, q_ref.shape[0], blk):
        rows = slice(c, c + blk)
        x = jnp.broadcast_to(cum_ref[:, rows] * (-1.0 / SCALE), (LANES, blk)).T
        hi = x.astype(bf16).astype(f32)
        mid = (x - hi).astype(bf16).astype(f32)
        lo = ((x - hi) - mid).astype(bf16).astype(f32)
        aug = jnp.where(lane == 0, hi, jnp.where(lane == 1, mid, jnp.where(lane == 2, lo, 0.0)))
        k2_ref[rows, :d] = k_ref[rows, :]
        k2_ref[rows, d:] = aug.astype(bf16)
        q2_ref[rows, :d] = q_ref[rows, :]
        q2_ref[rows, d:] = jnp.where(lane < 3, 1.0, 0.0).astype(bf16)
    m_ref[...] = jnp.full(m_ref.shape, NEG_INF, jnp.float32)
    acc_ref[...] = jnp.zeros(acc_ref.shape, jnp.float32)

    def starts(t):
        return int(pairs[0][t]) * blk, int(pairs[1][t]) * blk

    def stage_a(t, kind, buf):
        qs, ks = starts(t)
        u = _scores_t(k2_ref[pl.ds(ks, blk), :], q2_ref[pl.ds(qs, blk), :]) * (SCALE * LOG2E)
        if kind == DIAG:
            u = jnp.where(qry >= key, u, NEG_INF)
        u_refs[buf][...] = u
        bm_refs[buf][...] = jnp.max(u, axis=0, keepdims=True)

    def stage_b(t, buf):
        qs, ks = starts(t)
        cq = cum_ref[:, pl.ds(qs, blk)] * LOG2E
        _softmax_pv_update(u_refs[buf][...], bm_refs[buf][...], cq, vt_ref[:, pl.ds(ks, blk)],
                           m_ref, None, acc_ref, qs)

    _run_pipeline(kinds, stage_a, stage_b, unroll_all=True)

    for c in range(0, o_ref.shape[0], blk):
        cols = slice(c, c + blk)
        o_t = acc_ref[:d, cols] * (1.0 / acc_ref[d:d + 1, cols])
        o_ref[cols, :] = o_t.T.astype(o_ref.dtype)


def _fox_attention(qkv, cum, w_conv_a, w_conv_b):
    bsz, seq, _ = qkv.shape
    blk = ATTN_BLOCK
    q_off = 3 * DIFF_WIDTH // HEAD_DIM
    k_off = q_off + N_FOX_HEADS
    v_off = k_off + N_FOX_HEADS
    kinds, pairs = _pair_schedule(seq // blk, with_near=False)
    n_steps = N_FOX_HEADS * bsz

    def step_of(h, b):
        return h * bsz + b

    return pl.pallas_call(
        functools.partial(_fox_attn_kernel, kinds, pairs),
        grid=(N_FOX_HEADS, bsz),
        in_specs=[pl.BlockSpec((None, seq, HEAD_DIM), lambda h, b: (b, 0, q_off + h)),
                  pl.BlockSpec((None, seq, HEAD_DIM), lambda h, b: (b, 0, k_off + h)),
                  pl.BlockSpec((None, seq, HEAD_DIM), lambda h, b: (b, 0, v_off + h)),
                  pl.BlockSpec((None, None, 1, seq), lambda h, b: (h, b, 0, 0)),
                  _slab_spec(w_conv_a, n_steps, step_of),
                  _slab_spec(w_conv_b, n_steps, step_of)],
        out_specs=[pl.BlockSpec((None, seq, HEAD_DIM), lambda h, b: (b, 0, h)),
                   _slab_spec(_bf16_like(w_conv_a), n_steps, step_of),
                   _slab_spec(_bf16_like(w_conv_b), n_steps, step_of)],
        out_shape=[jax.ShapeDtypeStruct((bsz, seq, FOX_WIDTH), jnp.bfloat16),
                   _bf16_like(w_conv_a), _bf16_like(w_conv_b)],
        scratch_shapes=([pltpu.VMEM((HEAD_DIM + FOX_SUM_ROWS, seq), jnp.bfloat16),
                         pltpu.VMEM((seq, 2 * HEAD_DIM), jnp.bfloat16),
                         pltpu.VMEM((seq, 2 * HEAD_DIM), jnp.bfloat16)]
                        + [pltpu.VMEM((blk, blk), jnp.float32)] * PIPE_BUFS
                        + [pltpu.VMEM((1, blk), jnp.float32)] * PIPE_BUFS
                        + [pltpu.VMEM((1, seq), jnp.float32),
                           pltpu.VMEM((HEAD_DIM + FOX_SUM_ROWS, seq), jnp.float32)]),
        compiler_params=_params(("arbitrary", "arbitrary")),
        name="fox_attention",
    )(qkv, qkv, qkv, cum, w_conv_a, w_conv_b)


def _out_proj_kernel(ad_ref, af_ref, w_ref, res_ref, g_ref, x1_ref, xg_ref, ssq_ref):
    j = pl.program_id(1)
    kd = ad_ref.shape[1]
    wd = w_ref[:kd, :]
    wf = w_ref[kd:, :]
    parts = []
    for rows in _row_chunks(ad_ref.shape[0]):
        y = (res_ref[rows, :] + jnp.dot(ad_ref[rows, :], wd, preferred_element_type=jnp.float32)
             + jnp.dot(af_ref[rows, :], wf, preferred_element_type=jnp.float32))
        x1_ref[rows, :] = y
        xg_ref[rows, :] = (y * g_ref[...]).astype(xg_ref.dtype)
        sq = y * y
        part = sq[:, :LANES]
        for lc in range(LANES, sq.shape[1], LANES):
            part = part + sq[:, lc:lc + LANES]
        parts.append(part)
    part = jnp.concatenate(parts, axis=0)

    @pl.when(j == 0)
    def _first():
        ssq_ref[...] = part

    @pl.when(j > 0)
    def _rest():
        ssq_ref[...] += part


def _out_proj(a_diff, a_fox, w, res, g, bm=1024, bn=512):
    m, kd = a_diff.shape
    kf = a_fox.shape[1]
    d = w.shape[-1]
    return pl.pallas_call(
        _out_proj_kernel,
        grid=(m // bm, d // bn),
        in_specs=[pl.BlockSpec((bm, kd), lambda i, j: (i, 0)),
                  pl.BlockSpec((bm, kf), lambda i, j: (i, 0)),
                  pl.BlockSpec((kd + kf, bn), lambda i, j: (0, j)),
                  pl.BlockSpec((bm, bn), lambda i, j: (i, j)),
                  pl.BlockSpec((1, bn), lambda i, j: (0, j))],
        out_specs=[pl.BlockSpec((bm, bn), lambda i, j: (i, j)),
                   pl.BlockSpec((bm, bn), lambda i, j: (i, j)),
                   pl.BlockSpec((bm, LANES), lambda i, j: (i, 0))],
        out_shape=[jax.ShapeDtypeStruct((m, d), jnp.float32),
                   jax.ShapeDtypeStruct((m, d), jnp.bfloat16),
                   jax.ShapeDtypeStruct((m, LANES), jnp.float32)],
        compiler_params=_params(("arbitrary", "arbitrary")),
        name="out_proj",
    )(a_diff, a_fox, w, res, g)


def _down_kernel(a_ref, w_ref, res_ref, g_ref, o_ref, *, n_k, k_last, n_res):
    k = pl.program_id(1)
    bk = a_ref.shape[1]
    rc = res_ref.shape[1]

    def accumulate(k_len, first):
        a = a_ref[:, :k_len]
        for c in range(0, o_ref.shape[1], PROJ_COL_CHUNK):
            cols = slice(c, c + PROJ_COL_CHUNK)
            prod = jnp.dot(a, w_ref[:k_len, cols], preferred_element_type=jnp.float32)
            o_ref[:, cols] = prod if first else o_ref[:, cols] + prod

    pl.when(k == 0)(lambda: accumulate(bk, True))
    pl.when((k > 0) & (k < n_k - 1))(lambda: accumulate(bk, False))

    @pl.when(k < n_res)
    def _add_residual_tile():
        cols = pl.ds(pl.multiple_of(k * rc, rc), rc)
        o_ref[:, cols] += res_ref[...]

    @pl.when(k == n_k - 1)
    def _finish():
        accumulate(k_last, False)
        for r in range(0, o_ref.shape[0], PROJ_ROW_CHUNK):
            rows = slice(r, r + PROJ_ROW_CHUNK)
            x = o_ref[rows, :]
            o_ref[rows, :] = x * lax.rsqrt(jnp.mean(x * x, axis=-1, keepdims=True) + EPS) * g_ref[...]


def _down_proj_norm(a, w, res, g, bm=1024, bk=DOWN_K_TILE, rc=DOWN_RES_TILE):
    m, kdim = a.shape
    d = w.shape[1]
    n_k = pl.cdiv(kdim, bk)
    k_last = kdim - (n_k - 1) * bk
    n_res = d // rc
    assert n_res * rc == d and n_res <= n_k - 1
    row_block = pl.BlockSpec((bm, d), lambda i, k: (i, 0))
    return pl.pallas_call(
        functools.partial(_down_kernel, n_k=n_k, k_last=k_last, n_res=n_res),
        grid=(m // bm, n_k),
        in_specs=[pl.BlockSpec((bm, bk), lambda i, k: (i, k)),
                  pl.BlockSpec((bk, d), lambda i, k: (k, 0)),
                  pl.BlockSpec((bm, rc), lambda i, k: (i, jnp.minimum(k, n_res - 1))),
                  pl.BlockSpec((1, d), lambda i, k: (0, 0))],
        out_specs=row_block,
        out_shape=jax.ShapeDtypeStruct((m, d), jnp.float32),
        compiler_params=_params(("arbitrary", "arbitrary")),
        name="ffn_down",
    )(a, w, res, g)


def _gate_up_kernel(xg_ref, ssq_ref, wg_ref, wu_ref, wd_ref, o_ref, wd_out_ref, r_ref):
    @pl.when(pl.program_id(1) == 0)
    def _row_scale():
        mean_sq = jnp.sum(ssq_ref[...], axis=-1, keepdims=True) * (1.0 / xg_ref.shape[1])
        r_ref[...] = jnp.broadcast_to(lax.rsqrt(mean_sq + EPS), r_ref.shape)

    wg = wg_ref[...]
    wu = wu_ref[...]
    for rows in _row_chunks(xg_ref.shape[0]):
        xg = xg_ref[rows, :]
        r = jnp.concatenate([r_ref[rows, :]] * (o_ref.shape[1] // LANES), axis=1)
        gate = jnp.dot(xg, wg, preferred_element_type=jnp.float32) * r
        up = jnp.dot(xg, wu, preferred_element_type=jnp.float32) * r
        o_ref[rows, :] = (gate * jax.nn.sigmoid(gate) * up).astype(o_ref.dtype)
    wd_out_ref[...] = wd_ref[...].astype(wd_out_ref.dtype)


def _gate_up(xg, ssq, wg, wu, wd, bm=ROW_TILE, bn=FF_TILE):
    m, k = xg.shape
    f_rows, d = wd.shape[-2:]
    n = wg.shape[1]
    n_j = n // bn
    n_steps = (m // bm) * n_j

    def step_of(i, j):
        return i * n_j + j

    return pl.pallas_call(
        _gate_up_kernel,
        grid=(m // bm, n_j),
        in_specs=[pl.BlockSpec((bm, k), lambda i, j: (i, 0)),
                  pl.BlockSpec((bm, LANES), lambda i, j: (i, 0)),
                  pl.BlockSpec((k, bn), lambda i, j: (0, j)),
                  pl.BlockSpec((k, bn), lambda i, j: (0, j)),
                  _slab_spec(wd, n_steps, step_of)],
        out_specs=[pl.BlockSpec((bm, bn), lambda i, j: (i, j)),
                   _slab_spec(_bf16_like(wd), n_steps, step_of)],
        out_shape=[jax.ShapeDtypeStruct((m, n), jnp.bfloat16), _bf16_like(wd)],
        scratch_shapes=[pltpu.VMEM((bm, LANES), jnp.float32)],
        compiler_params=_params(("arbitrary", "arbitrary")),
        name="ffn_gate_up",
    )(xg, ssq, wg, wu, wd)


def kernel(x, attn_norm_g, w_in, b_f, lambda_q1, lambda_k1, lambda_q2, lambda_k2, rel_bias_table,
           diff_subln_g, w_o, ffn_norm_g, w_gate, w_up, w_down, final_norm_g):
    bsz, seq, d = x.shape
    m = bsz * seq
    x2d = x.reshape(m, d)

    w_in_t = jnp.swapaxes(w_in, 1, 2)
    h, fl_t = _norm_and_gate_logits(x2d, attn_norm_g[0].reshape(1, d), w_in_t, QKV_COLS, N_FOX_HEADS)
    qkv = _matmul(h, w_in_t, QKV_COLS, name="in_proj").reshape(bsz, seq, QKV_COLS)

    cum = _cum_log_forget(fl_t, b_f[0].reshape(N_FOX_HEADS, 1), seq)
    cum = cum.reshape(N_FOX_HEADS, bsz, 1, seq)

    o_diff, w_o_b = _diff_attention(
        qkv, rel_bias_table,
        lambda_q1[0].reshape(1, HEAD_DIM), lambda_k1[0].reshape(1, HEAD_DIM),
        lambda_q2[0].reshape(1, HEAD_DIM), lambda_k2[0].reshape(1, HEAD_DIM),
        diff_subln_g[0].reshape(2 * HEAD_DIM, 1), w_o)
    o_fox, w_gate_b, w_up_b = _fox_attention(qkv, cum, w_gate, w_up)

    x1, x1g, ssq = _out_proj(o_diff.reshape(m, DIFF_WIDTH), o_fox.reshape(m, FOX_WIDTH), w_o_b, x2d,
                             ffn_norm_g[0].reshape(1, d))
    act, w_down_b = _gate_up(x1g, ssq, w_gate_b, w_up_b, w_down)
    out = _down_proj_norm(act, w_down_b, x1, final_norm_g.reshape(1, d))
    return out.reshape(bsz, seq, d)
```

```python
import functools
import math

import numpy as np

import jax
import jax.numpy as jnp
from jax import lax
from jax.experimental import pallas as pl
from jax.experimental.pallas import tpu as pltpu

D_MODEL = 4096
HEAD_DIM = 128
N_DIFF_HEADS = D_MODEL // (4 * HEAD_DIM)
N_FOX_HEADS = D_MODEL // (2 * HEAD_DIM)
DIFF_WIDTH = N_DIFF_HEADS * 2 * HEAD_DIM
FOX_WIDTH = N_FOX_HEADS * HEAD_DIM
QKV_COLS = 3 * DIFF_WIDTH + 3 * FOX_WIDTH
D_FF = ((8 * D_MODEL + 3 * 256 - 1) // (3 * 256)) * 256
NUM_BUCKETS = 32
MAX_DISTANCE = 128
EPS = 1e-6
NEG_INF = -1e30
LAM_INIT = 0.8 - 0.6 * math.exp(-0.3 * 0)
SCALE = HEAD_DIM ** -0.5
LOG2E = math.log2(math.e)

LANES = 128
BF16_SUBLANES = 16
ROW_TILE = 2048
FF_TILE = 256
DOWN_K_TILE = 512
DOWN_RES_TILE = 256
VMEM_LIMIT = 56 * 1024 * 1024
VMEM_LIMIT_HIGH = 60 * 1024 * 1024

ATTN_BLOCK = 512
PIPE_DEPTH = 2
PIPE_BUFS = PIPE_DEPTH + 1
FOX_SUM_ROWS = BF16_SUBLANES
MATMUL_ROW_CHUNK = 512
PROJ_COL_CHUNK = 1024
PROJ_ROW_CHUNK = 128

FAR, NEAR, DIAG = "far", "near", "diag"


def _params(sem, vmem=VMEM_LIMIT):
    return pltpu.CompilerParams(dimension_semantics=sem, vmem_limit_bytes=vmem)


def _row_chunks(n_rows):
    return [slice(s, s + MATMUL_ROW_CHUNK) for s in range(0, n_rows, MATMUL_ROW_CHUNK)]


def _norm_kernel(x_ref, g_ref, wf_ref, h_ref, fl_ref):
    x = x_ref[...]
    y = x * lax.rsqrt(jnp.mean(x * x, axis=-1, keepdims=True) + EPS)
    h = (y * g_ref[...]).astype(jnp.bfloat16)
    h_ref[...] = h
    fl_ref[...] = lax.dot_general(wf_ref[...].astype(jnp.bfloat16), h, (((1,), (1,)), ((), ())),
                                  preferred_element_type=jnp.float32)


def _norm_and_gate_logits(x2d, g, w_in_t, gate_row, nh, bm=512):
    m, d = x2d.shape
    assert gate_row % nh == 0 and gate_row + nh == w_in_t.shape[1]
    return pl.pallas_call(
        _norm_kernel,
        grid=(m // bm,),
        in_specs=[pl.BlockSpec((bm, d), lambda i: (i, 0)),
                  pl.BlockSpec((1, d), lambda i: (0, 0)),
                  pl.BlockSpec((None, nh, d), lambda i: (0, gate_row // nh, 0))],
        out_specs=[pl.BlockSpec((bm, d), lambda i: (i, 0)),
                   pl.BlockSpec((nh, bm), lambda i: (0, i))],
        out_shape=[jax.ShapeDtypeStruct((m, d), jnp.bfloat16),
                   jax.ShapeDtypeStruct((nh, m), jnp.float32)],
        compiler_params=_params(("arbitrary",)),
        name="attn_norm",
    )(x2d, g, w_in_t)


def _matmul_kernel(a_ref, wt_ref, o_ref):
    w_t = wt_ref[...].astype(a_ref.dtype)
    for rows in _row_chunks(a_ref.shape[0]):
        o_ref[rows, :] = lax.dot_general(a_ref[rows, :], w_t, (((1,), (1,)), ((), ())),
                                         preferred_element_type=jnp.float32).astype(o_ref.dtype)


def _matmul(a, w_t, n_out, bm=ROW_TILE, bn=512, name="matmul"):
    m, k = a.shape
    return pl.pallas_call(
        _matmul_kernel,
        grid=(m // bm, n_out // bn),
        in_specs=[pl.BlockSpec((bm, k), lambda i, j: (i, 0)),
                  pl.BlockSpec((None, bn, k), lambda i, j: (0, j, 0))],
        out_specs=pl.BlockSpec((bm, bn), lambda i, j: (i, j)),
        out_shape=jax.ShapeDtypeStruct((m, n_out), jnp.bfloat16),
        compiler_params=_params(("arbitrary", "arbitrary"), vmem=VMEM_LIMIT_HIGH),
        name=name,
    )(a, w_t)


def _cumsum_kernel(fl_ref, bf_ref, c_ref):
    z = fl_ref[...] + bf_ref[...]
    x = jnp.minimum(z, 0.0) - jnp.log(1.0 + jnp.exp(-jnp.abs(z)))
    n = x.shape[-1]
    pos = lax.broadcasted_iota(jnp.int32, x.shape, 1)
    shift = 1
    while shift < n:
        x = x + jnp.where(pos >= shift, pltpu.roll(x, shift, 1), 0.0)
        shift *= 2
    c_ref[...] = x


def _cum_log_forget(fl_t, b_f, seq):
    nh, m = fl_t.shape
    return pl.pallas_call(
        _cumsum_kernel,
        grid=(m // seq,),
        in_specs=[pl.BlockSpec((nh, seq), lambda b: (0, b)),
                  pl.BlockSpec((nh, 1), lambda b: (0, 0))],
        out_specs=pl.BlockSpec((nh, seq), lambda b: (0, b)),
        out_shape=jax.ShapeDtypeStruct((nh, m), jnp.float32),
        compiler_params=_params(("arbitrary",)),
        name="cum_log_forget",
    )(fl_t, b_f)


def _scores_t(k, q):
    return lax.dot_general(k, q, (((1,), (1,)), ((), ())), preferred_element_type=jnp.float32)


def _softmax_pv_update(u, blk_max, shift, v_t, m_ref, l_ref, acc_ref, qs):
    blk = u.shape[1]
    m_old = m_ref[:, pl.ds(qs, blk)]
    m_new = jnp.maximum(m_old, blk_max + shift)
    alpha = jnp.exp2(m_old - m_new)
    p = jnp.exp2(u + (shift - m_new))
    if l_ref is not None:
        l_ref[:, pl.ds(qs, blk)] = (alpha * l_ref[:, pl.ds(qs, blk)]
                                    + jnp.sum(p, axis=0, keepdims=True))
    acc_ref[:, pl.ds(qs, blk)] = alpha * acc_ref[:, pl.ds(qs, blk)] + jnp.dot(
        v_t, p.astype(v_t.dtype), preferred_element_type=jnp.float32)
    m_ref[:, pl.ds(qs, blk)] = m_new


def _pair_schedule(n_blocks, with_near):
    far, near, diag = [], [], []
    for i in range(n_blocks):
        for j in range(i + 1):
            if j == i:
                diag.append((i, j))
            elif with_near and j == i - 1:
                near.append((i, j))
            else:
                far.append((i, j))
    pairs = far + near + diag
    kinds = [FAR] * len(far) + [NEAR] * len(near) + [DIAG] * len(diag)
    return kinds, np.asarray(pairs, np.int32).T.copy()


def _run_pipeline(kinds, stage_a, stage_b, unroll_all=False):
    n = len(kinds)
    ahead_kind = list(kinds[PIPE_DEPTH:]) + [None] * PIPE_DEPTH
    for t in range(min(PIPE_DEPTH, n)):
        stage_a(t, kinds[t], t % PIPE_BUFS)
    s = 0
    while s < n:
        e = s
        while e < n and ahead_kind[e] == ahead_kind[s]:
            e += 1
        kind, n_loop = ahead_kind[s], 0 if unroll_all else (e - s) // PIPE_BUFS

        def one(step, buf):
            if kind is not None:
                stage_a(step + PIPE_DEPTH, kind, (buf + PIPE_DEPTH) % PIPE_BUFS)
            stage_b(step, buf)

        if n_loop:
            def body(r, carry):
                base = s + PIPE_BUFS * r
                for i in range(PIPE_BUFS):
                    one(base + i, (s + i) % PIPE_BUFS)
                return carry

            lax.fori_loop(0, n_loop, body, 0)
        for step in range(s + n_loop * PIPE_BUFS, e):
            one(step, step % PIPE_BUFS)
        s = e


def _store_transposed(src_ref, dst_ref):
    for c in range(0, src_ref.shape[0], ATTN_BLOCK):
        dst_ref[:, c:c + ATTN_BLOCK] = (
            src_ref[c:c + ATTN_BLOCK, :].astype(jnp.float32).T.astype(dst_ref.dtype))


def _t5_bucket(n):
    max_exact = NUM_BUCKETS // 2
    nf = jnp.maximum(n, 1).astype(jnp.float32)
    large = max_exact + (jnp.log(nf / max_exact) / math.log(MAX_DISTANCE / max_exact)
                         * (NUM_BUCKETS - max_exact)).astype(jnp.int32)
    large = jnp.minimum(large, NUM_BUCKETS - 1)
    return jnp.where(n < max_exact, n, large)


def _diff_attn_kernel(kinds, pair_ref, tab_ref, lq1_ref, lk1_ref, lq2_ref, lk2_ref,
                      q_ref, k_ref, v_ref, g_ref, w_ref, o_ref, w_out_ref,
                      bias_near_ref, bias_diag_ref, vt_ref, *tile_and_state_refs):
    h = pl.program_id(0)
    b = pl.program_id(1)
    blk = ATTN_BLOCK
    w_out_ref[...] = w_ref[...].astype(w_out_ref.dtype)
    key = lax.broadcasted_iota(jnp.int32, (blk, blk), 0)
    qry = lax.broadcasted_iota(jnp.int32, (blk, blk), 1)
    u_refs = tile_and_state_refs[:PIPE_BUFS]
    bm_refs = tile_and_state_refs[PIPE_BUFS:2 * PIPE_BUFS]
    m_ref, l_ref, acc_ref = tile_and_state_refs[2 * PIPE_BUFS:]
    far_bias = tab_ref[NUM_BUCKETS - 1, h]

    @pl.when(b == 0)
    def _build_bias():
        for ref, offset in ((bias_diag_ref, 0), (bias_near_ref, blk)):
            bucket = _t5_bucket(jnp.maximum(qry - key + offset, 0))
            bias = jnp.zeros((blk, blk), jnp.float32)
            for t in range(NUM_BUCKETS):
                bias = jnp.where(bucket == t, tab_ref[t, h], bias)
            ref[...] = (bias - far_bias) * LOG2E

    _store_transposed(v_ref, vt_ref)
    m_ref[...] = jnp.full(m_ref.shape, NEG_INF, jnp.float32)
    l_ref[...] = jnp.zeros(l_ref.shape, jnp.float32)
    acc_ref[...] = jnp.zeros(acc_ref.shape, jnp.float32)

    def starts(t):
        return (pl.multiple_of(pair_ref[0, t] * blk, blk), pl.multiple_of(pair_ref[1, t] * blk, blk))

    def stage_a(t, kind, par):
        qs, ks = starts(t)
        for mp in range(2):
            cols = slice(mp * HEAD_DIM, (mp + 1) * HEAD_DIM)
            u = _scores_t(k_ref[pl.ds(ks, blk), cols], q_ref[pl.ds(qs, blk), cols]) * (SCALE * LOG2E)
            if kind == NEAR:
                u = u + bias_near_ref[...]
            elif kind == DIAG:
                u = jnp.where(qry >= key, u + bias_diag_ref[...], NEG_INF)
            u_refs[par][mp] = u
            bm_refs[par][mp] = jnp.max(u, axis=0, keepdims=True)

    def stage_b(t, par):
        qs, ks = starts(t)
        v_t = vt_ref[:, pl.ds(ks, blk)]
        for mp in range(2):
            _softmax_pv_update(u_refs[par][mp], bm_refs[par][mp], far_bias * LOG2E, v_t,
                               m_ref.at[mp], l_ref.at[mp], acc_ref.at[mp], qs)

    _run_pipeline(kinds, stage_a, stage_b)

    lam = (jnp.exp(jnp.sum(lq1_ref[...] * lk1_ref[...], axis=-1, keepdims=True))
           - jnp.exp(jnp.sum(lq2_ref[...] * lk2_ref[...], axis=-1, keepdims=True))
           + LAM_INIT)
    for c in range(0, o_ref.shape[0], blk):
        cols = slice(c, c + blk)
        o_t = (acc_ref[0, :, cols] * (1.0 / l_ref[0, :, cols])
               - lam * (acc_ref[1, :, cols] * (1.0 / l_ref[1, :, cols])))
        y_t = o_t * lax.rsqrt(jnp.mean(o_t * o_t, axis=0, keepdims=True) + EPS)
        y_t = (y_t * g_ref[...]) * (1.0 - LAM_INIT)
        o_ref[cols, :] = y_t.T.astype(o_ref.dtype)


def _slab_spec(weight, n_steps, step_of):
    rows, cols = weight.shape[-2:]
    slab = rows // n_steps
    assert slab * n_steps == rows and slab % BF16_SUBLANES == 0
    if weight.ndim == 3:
        return pl.BlockSpec((None, slab, cols), lambda *idx: (0, step_of(*idx), 0))
    return pl.BlockSpec((slab, cols), lambda *idx: (step_of(*idx), 0))


def _bf16_like(weight):
    return jax.ShapeDtypeStruct(weight.shape[-2:], jnp.bfloat16)


def _diff_attention(qkv, tab, lq1, lk1, lq2, lk2, g_col, w_conv):
    bsz, seq, _ = qkv.shape
    blk = ATTN_BLOCK
    w = 2 * HEAD_DIM
    k_off = DIFF_WIDTH // w
    v_off = 2 * DIFF_WIDTH // w
    kinds, pairs = _pair_schedule(seq // blk, with_near=True)
    smem = pl.BlockSpec(memory_space=pltpu.SMEM)
    vec = pl.BlockSpec((1, HEAD_DIM), lambda h, b: (0, 0))
    n_steps = N_DIFF_HEADS * bsz

    def step_of(h, b):
        return h * bsz + b

    return pl.pallas_call(
        functools.partial(_diff_attn_kernel, kinds),
        grid=(N_DIFF_HEADS, bsz),
        in_specs=[smem, smem, vec, vec, vec, vec,
                  pl.BlockSpec((None, seq, w), lambda h, b: (b, 0, h)),
                  pl.BlockSpec((None, seq, w), lambda h, b: (b, 0, k_off + h)),
                  pl.BlockSpec((None, seq, w), lambda h, b: (b, 0, v_off + h)),
                  pl.BlockSpec((w, 1), lambda h, b: (0, 0)),
                  _slab_spec(w_conv, n_steps, step_of)],
        out_specs=[pl.BlockSpec((None, seq, w), lambda h, b: (b, 0, h)),
                   _slab_spec(_bf16_like(w_conv), n_steps, step_of)],
        out_shape=[jax.ShapeDtypeStruct((bsz, seq, DIFF_WIDTH), jnp.bfloat16), _bf16_like(w_conv)],
        scratch_shapes=([pltpu.VMEM((blk, blk), jnp.float32),
                         pltpu.VMEM((blk, blk), jnp.float32),
                         pltpu.VMEM((w, seq), jnp.bfloat16)]
                        + [pltpu.VMEM((2, blk, blk), jnp.float32)] * PIPE_BUFS
                        + [pltpu.VMEM((2, 1, blk), jnp.float32)] * PIPE_BUFS
                        + [pltpu.VMEM((2, 1, seq), jnp.float32),
                           pltpu.VMEM((2, 1, seq), jnp.float32),
                           pltpu.VMEM((2, w, seq), jnp.float32)]),
        compiler_params=_params(("arbitrary", "arbitrary")),
        name="diff_attention",
    )(jnp.asarray(pairs), tab, lq1, lk1, lq2, lk2, qkv, qkv, qkv, g_col, w_conv)


def _fox_attn_kernel(kinds, pairs, q_ref, k_ref, v_ref, cum_ref, wa_ref, wb_ref,
                     o_ref, wa_out_ref, wb_out_ref,
                     vt_ref, q2_ref, k2_ref, *tile_and_state_refs):
    blk = ATTN_BLOCK
    d = HEAD_DIM
    f32, bf16 = jnp.float32, jnp.bfloat16
    key = lax.broadcasted_iota(jnp.int32, (blk, blk), 0)
    qry = lax.broadcasted_iota(jnp.int32, (blk, blk), 1)
    lane = lax.broadcasted_iota(jnp.int32, (blk, LANES), 1)
    u_refs = tile_and_state_refs[:PIPE_BUFS]
    bm_refs = tile_and_state_refs[PIPE_BUFS:2 * PIPE_BUFS]
    m_ref, acc_ref = tile_and_state_refs[2 * PIPE_BUFS:]

    wa_out_ref[...] = wa_ref[...].astype(wa_out_ref.dtype)
    wb_out_ref[...] = wb_ref[...].astype(wb_out_ref.dtype)

    _store_transposed(v_ref, vt_ref.at[:d, :])
    vt_ref[d:, :] = jnp.ones((vt_ref.shape[0] - d, vt_ref.shape[1]), vt_ref.dtype)
    for c in range(0, q_ref.shape[0], blk):
        rows = slice(c, c + blk)
        x = jnp.broadcast_to(cum_ref[:, rows] * (-1.0 / SCALE), (LANES, blk)).T
        hi = x.astype(bf16).astype(f32)
        mid = (x - hi).astype(bf16).astype(f32)
        lo = ((x - hi) - mid).astype(bf16).astype(f32)
        aug = jnp.where(lane == 0, hi, jnp.where(lane == 1, mid, jnp.where(lane == 2, lo, 0.0)))
        k2_ref[rows, :d] = k_ref[rows, :]
        k2_ref[rows, d:] = aug.astype(bf16)
        q2_ref[rows, :d] = q_ref[rows, :]
        q2_ref[rows, d:] = jnp.where(lane < 3, 1.0, 0.0).astype(bf16)
    m_ref[...] = jnp.full(m_ref.shape, NEG_INF, jnp.float32)
    acc_ref[...] = jnp.zeros(acc_ref.shape, jnp.float32)

    def starts(t):
        return int(pairs[0][t]) * blk, int(pairs[1][t]) * blk

    def stage_a(t, kind, buf):
        qs, ks = starts(t)
        u = _scores_t(k2_ref[pl.ds(ks, blk), :], q2_ref[pl.ds(qs, blk), :]) * (SCALE * LOG2E)
        if kind == DIAG:
            u = jnp.where(qry >= key, u, NEG_INF)
        u_refs[buf][...] = u
        bm_refs[buf][...] = jnp.max(u, axis=0, keepdims=True)

    def stage_b(t, buf):
        qs, ks = starts(t)
        cq = cum_ref[:, pl.ds(qs, blk)] * LOG2E
        _softmax_pv_update(u_refs[buf][...], bm_refs[buf][...], cq, vt_ref[:, pl.ds(ks, blk)],
                           m_ref, None, acc_ref, qs)

    _run_pipeline(kinds, stage_a, stage_b, unroll_all=True)

    for c in range(0, o_ref.shape[0], blk):
        cols = slice(c, c + blk)
        o_t = acc_ref[:d, cols] * (1.0 / acc_ref[d:d + 1, cols])
        o_ref[cols, :] = o_t.T.astype(o_ref.dtype)


def _fox_attention(qkv, cum, w_conv_a, w_conv_b):
    bsz, seq, _ = qkv.shape
    blk = ATTN_BLOCK
    q_off = 3 * DIFF_WIDTH // HEAD_DIM
    k_off = q_off + N_FOX_HEADS
    v_off = k_off + N_FOX_HEADS
    kinds, pairs = _pair_schedule(seq // blk, with_near=False)
    n_steps = N_FOX_HEADS * bsz

    def step_of(h, b):
        return h * bsz + b

    return pl.pallas_call(
        functools.partial(_fox_attn_kernel, kinds, pairs),
        grid=(N_FOX_HEADS, bsz),
        in_specs=[pl.BlockSpec((None, seq, HEAD_DIM), lambda h, b: (b, 0, q_off + h)),
                  pl.BlockSpec((None, seq, HEAD_DIM), lambda h, b: (b, 0, k_off + h)),
                  pl.BlockSpec((None, seq, HEAD_DIM), lambda h, b: (b, 0, v_off + h)),
                  pl.BlockSpec((None, None, 1, seq), lambda h, b: (h, b, 0, 0)),
                  _slab_spec(w_conv_a, n_steps, step_of),
                  _slab_spec(w_conv_b, n_steps, step_of)],
        out_specs=[pl.BlockSpec((None, seq, HEAD_DIM), lambda h, b: (b, 0, h)),
                   _slab_spec(_bf16_like(w_conv_a), n_steps, step_of),
                   _slab_spec(_bf16_like(w_conv_b), n_steps, step_of)],
        out_shape=[jax.ShapeDtypeStruct((bsz, seq, FOX_WIDTH), jnp.bfloat16),
                   _bf16_like(w_conv_a), _bf16_like(w_conv_b)],
        scratch_shapes=([pltpu.VMEM((HEAD_DIM + FOX_SUM_ROWS, seq), jnp.bfloat16),
                         pltpu.VMEM((seq, 2 * HEAD_DIM), jnp.bfloat16),
                         pltpu.VMEM((seq, 2 * HEAD_DIM), jnp.bfloat16)]
                        + [pltpu.VMEM((blk, blk), jnp.float32)] * PIPE_BUFS
                        + [pltpu.VMEM((1, blk), jnp.float32)] * PIPE_BUFS
                        + [pltpu.VMEM((1, seq), jnp.float32),
                           pltpu.VMEM((HEAD_DIM + FOX_SUM_ROWS, seq), jnp.float32)]),
        compiler_params=_params(("arbitrary", "arbitrary")),
        name="fox_attention",
    )(qkv, qkv, qkv, cum, w_conv_a, w_conv_b)


def _out_proj_kernel(ad_ref, af_ref, w_ref, res_ref, g_ref, x1_ref, xg_ref, ssq_ref):
    j = pl.program_id(1)
    kd = ad_ref.shape[1]
    wd = w_ref[:kd, :]
    wf = w_ref[kd:, :]
    parts = []
    for rows in _row_chunks(ad_ref.shape[0]):
        y = (res_ref[rows, :] + jnp.dot(ad_ref[rows, :], wd, preferred_element_type=jnp.float32)
             + jnp.dot(af_ref[rows, :], wf, preferred_element_type=jnp.float32))
        x1_ref[rows, :] = y
        xg_ref[rows, :] = (y * g_ref[...]).astype(xg_ref.dtype)
        sq = y * y
        part = sq[:, :LANES]
        for lc in range(LANES, sq.shape[1], LANES):
            part = part + sq[:, lc:lc + LANES]
        parts.append(part)
    part = jnp.concatenate(parts, axis=0)

    @pl.when(j == 0)
    def _first():
        ssq_ref[...] = part

    @pl.when(j > 0)
    def _rest():
        ssq_ref[...] += part


def _out_proj(a_diff, a_fox, w, res, g, bm=1024, bn=512):
    m, kd = a_diff.shape
    kf = a_fox.shape[1]
    d = w.shape[-1]
    return pl.pallas_call(
        _out_proj_kernel,
        grid=(m // bm, d // bn),
        in_specs=[pl.BlockSpec((bm, kd), lambda i, j: (i, 0)),
                  pl.BlockSpec((bm, kf), lambda i, j: (i, 0)),
                  pl.BlockSpec((kd + kf, bn), lambda i, j: (0, j)),
                  pl.BlockSpec((bm, bn), lambda i, j: (i, j)),
                  pl.BlockSpec((1, bn), lambda i, j: (0, j))],
        out_specs=[pl.BlockSpec((bm, bn), lambda i, j: (i, j)),
                   pl.BlockSpec((bm, bn), lambda i, j: (i, j)),
                   pl.BlockSpec((bm, LANES), lambda i, j: (i, 0))],
        out_shape=[jax.ShapeDtypeStruct((m, d), jnp.float32),
                   jax.ShapeDtypeStruct((m, d), jnp.bfloat16),
                   jax.ShapeDtypeStruct((m, LANES), jnp.float32)],
        compiler_params=_params(("arbitrary", "arbitrary")),
        name="out_proj",
    )(a_diff, a_fox, w, res, g)


def _down_kernel(a_ref, w_ref, res_ref, g_ref, o_ref, *, n_k, k_last, n_res):
    k = pl.program_id(1)
    bk = a_ref.shape[1]
    rc = res_ref.shape[1]

    def accumulate(k_len, first):
        a = a_ref[:, :k_len]
        for c in range(0, o_ref.shape[1], PROJ_COL_CHUNK):
            cols = slice(c, c + PROJ_COL_CHUNK)
            prod = jnp.dot(a, w_ref[:k_len, cols], preferred_element_type=jnp.float32)
            o_ref[:, cols] = prod if first else o_ref[:, cols] + prod

    pl.when(k == 0)(lambda: accumulate(bk, True))
    pl.when((k > 0) & (k < n_k - 1))(lambda: accumulate(bk, False))

    @pl.when(k < n_res)
    def _add_residual_tile():
        cols = pl.ds(pl.multiple_of(k * rc, rc), rc)
        o_ref[:, cols] += res_ref[...]

    @pl.when(k == n_k - 1)
    def _finish():
        accumulate(k_last, False)
        for r in range(0, o_ref.shape[0], PROJ_ROW_CHUNK):
            rows = slice(r, r + PROJ_ROW_CHUNK)
            x = o_ref[rows, :]
            o_ref[rows, :] = x * lax.rsqrt(jnp.mean(x * x, axis=-1, keepdims=True) + EPS) * g_ref[...]


def _down_proj_norm(a, w, res, g, bm=1024, bk=DOWN_K_TILE, rc=DOWN_RES_TILE):
    m, kdim = a.shape
    d = w.shape[1]
    n_k = pl.cdiv(kdim, bk)
    k_last = kdim - (n_k - 1) * bk
    n_res = d // rc
    assert n_res * rc == d and n_res <= n_k - 1
    row_block = pl.BlockSpec((bm, d), lambda i, k: (i, 0))
    return pl.pallas_call(
        functools.partial(_down_kernel, n_k=n_k, k_last=k_last, n_res=n_res),
        grid=(m // bm, n_k),
        in_specs=[pl.BlockSpec((bm, bk), lambda i, k: (i, k)),
                  pl.BlockSpec((bk, d), lambda i, k: (k, 0)),
                  pl.BlockSpec((bm, rc), lambda i, k: (i, jnp.minimum(k, n_res - 1))),
                  pl.BlockSpec((1, d), lambda i, k: (0, 0))],
        out_specs=row_block,
        out_shape=jax.ShapeDtypeStruct((m, d), jnp.float32),
        compiler_params=_params(("arbitrary", "arbitrary")),
        name="ffn_down",
    )(a, w, res, g)


def _gate_up_kernel(xg_ref, ssq_ref, wg_ref, wu_ref, wd_ref, o_ref, wd_out_ref, r_ref):
    @pl.when(pl.program_id(1) == 0)
    def _row_scale():
        mean_sq = jnp.sum(ssq_ref[...], axis=-1, keepdims=True) * (1.0 / xg_ref.shape[1])
        r_ref[...] = jnp.broadcast_to(lax.rsqrt(mean_sq + EPS), r_ref.shape)

    wg = wg_ref[...]
    wu = wu_ref[...]
    for rows in _row_chunks(xg_ref.shape[0]):
        xg = xg_ref[rows, :]
        r = jnp.concatenate([r_ref[rows, :]] * (o_ref.shape[1] // LANES), axis=1)
        gate = jnp.dot(xg, wg, preferred_element_type=jnp.float32) * r
        up = jnp.dot(xg, wu, preferred_element_type=jnp.float32) * r
        o_ref[rows, :] = (gate * jax.nn.sigmoid(gate) * up).astype(o_ref.dtype)
    wd_out_ref[...] = wd_ref[...].astype(wd_out_ref.dtype)


def _gate_up(xg, ssq, wg, wu, wd, bm=ROW_TILE, bn=FF_TILE):
    m, k = xg.shape
    f_rows, d = wd.shape[-2:]
    n = wg.shape[1]
    n_j = n // bn
    n_steps = (m // bm) * n_j

    def step_of(i, j):
        return i * n_j + j

    return pl.pallas_call(
        _gate_up_kernel,
        grid=(m // bm, n_j),
        in_specs=[pl.BlockSpec((bm, k), lambda i, j: (i, 0)),
                  pl.BlockSpec((bm, LANES), lambda i, j: (i, 0)),
                  pl.BlockSpec((k, bn), lambda i, j: (0, j)),
                  pl.BlockSpec((k, bn), lambda i, j: (0, j)),
                  _slab_spec(wd, n_steps, step_of)],
        out_specs=[pl.BlockSpec((bm, bn), lambda i, j: (i, j)),
                   _slab_spec(_bf16_like(wd), n_steps, step_of)],
        out_shape=[jax.ShapeDtypeStruct((m, n), jnp.bfloat16), _bf16_like(wd)],
        scratch_shapes=[pltpu.VMEM((bm, LANES), jnp.float32)],
        compiler_params=_params(("arbitrary", "arbitrary")),
        name="ffn_gate_up",
    )(xg, ssq, wg, wu, wd)


def kernel(x, attn_norm_g, w_in, b_f, lambda_q1, lambda_k1, lambda_q2, lambda_k2, rel_bias_table,
           diff_subln_g, w_o, ffn_norm_g, w_gate, w_up, w_down, final_norm_g):
    bsz, seq, d = x.shape
    m = bsz * seq
    x2d = x.reshape(m, d)

    w_in_t = jnp.swapaxes(w_in, 1, 2)
    h, fl_t = _norm_and_gate_logits(x2d, attn_norm_g[0].reshape(1, d), w_in_t, QKV_COLS, N_FOX_HEADS)
    qkv = _matmul(h, w_in_t, QKV_COLS, name="in_proj").reshape(bsz, seq, QKV_COLS)

    cum = _cum_log_forget(fl_t, b_f[0].reshape(N_FOX_HEADS, 1), seq)
    cum = cum.reshape(N_FOX_HEADS, bsz, 1, seq)

    o_diff, w_o_b = _diff_attention(
        qkv, rel_bias_table,
        lambda_q1[0].reshape(1, HEAD_DIM), lambda_k1[0].reshape(1, HEAD_DIM),
        lambda_q2[0].reshape(1, HEAD_DIM), lambda_k2[0].reshape(1, HEAD_DIM),
        diff_subln_g[0].reshape(2 * HEAD_DIM, 1), w_o)
    o_fox, w_gate_b, w_up_b = _fox_attention(qkv, cum, w_gate, w_up)

    x1, x1g, ssq = _out_proj(o_diff.reshape(m, DIFF_WIDTH), o_fox.reshape(m, FOX_WIDTH), w_o_b, x2d,
                             ffn_norm_g[0].reshape(1, d))
    act, w_down_b = _gate_up(x1g, ssq, w_gate_b, w_up_b, w_down)
    out = _down_proj_norm(act, w_down_b, x1, final_norm_g.reshape(1, d))
    return out.reshape(bsz, seq, d)
```

```python
import functools
import math

import numpy as np

import jax
import jax.numpy as jnp
from jax import lax
from jax.experimental import pallas as pl
from jax.experimental.pallas import tpu as pltpu

D_MODEL = 4096
HEAD_DIM = 128
N_DIFF_HEADS = D_MODEL // (4 * HEAD_DIM)
N_FOX_HEADS = D_MODEL // (2 * HEAD_DIM)
DIFF_WIDTH = N_DIFF_HEADS * 2 * HEAD_DIM
FOX_WIDTH = N_FOX_HEADS * HEAD_DIM
QKV_COLS = 3 * DIFF_WIDTH + 3 * FOX_WIDTH
D_FF = ((8 * D_MODEL + 3 * 256 - 1) // (3 * 256)) * 256
NUM_BUCKETS = 32
MAX_DISTANCE = 128
EPS = 1e-6
NEG_INF = -1e30
LAM_INIT = 0.8 - 0.6 * math.exp(-0.3 * 0)
SCALE = HEAD_DIM ** -0.5
LOG2E = math.log2(math.e)

LANES = 128
BF16_SUBLANES = 16
ROW_TILE = 2048
FF_TILE = 256
DOWN_K_TILE = 512
DOWN_RES_TILE = 256
VMEM_LIMIT = 56 * 1024 * 1024
VMEM_LIMIT_HIGH = 60 * 1024 * 1024

ATTN_BLOCK = 512
PIPE_DEPTH = 2
PIPE_BUFS = PIPE_DEPTH + 1
FOX_SUM_ROWS = BF16_SUBLANES
MATMUL_ROW_CHUNK = 512
PROJ_COL_CHUNK = 1024
PROJ_ROW_CHUNK = 128

FAR, NEAR, DIAG = "far", "near", "diag"


def _params(sem, vmem=VMEM_LIMIT):
    return pltpu.CompilerParams(dimension_semantics=sem, vmem_limit_bytes=vmem)


def _row_chunks(n_rows):
    return [slice(s, s + MATMUL_ROW_CHUNK) for s in range(0, n_rows, MATMUL_ROW_CHUNK)]


def _norm_kernel(x_ref, g_ref, wf_ref, h_ref, fl_ref):
    x = x_ref[...]
    y = x * lax.rsqrt(jnp.mean(x * x, axis=-1, keepdims=True) + EPS)
    h = (y * g_ref[...]).astype(jnp.bfloat16)
    h_ref[...] = h
    fl_ref[...] = lax.dot_general(wf_ref[...].astype(jnp.bfloat16), h, (((1,), (1,)), ((), ())),
                                  preferred_element_type=jnp.float32)


def _norm_and_gate_logits(x2d, g, w_in_t, gate_row, nh, bm=512):
    m, d = x2d.shape
    assert gate_row % nh == 0 and gate_row + nh == w_in_t.shape[1]
    return pl.pallas_call(
        _norm_kernel,
        grid=(m // bm,),
        in_specs=[pl.BlockSpec((bm, d), lambda i: (i, 0)),
                  pl.BlockSpec((1, d), lambda i: (0, 0)),
                  pl.BlockSpec((None, nh, d), lambda i: (0, gate_row // nh, 0))],
        out_specs=[pl.BlockSpec((bm, d), lambda i: (i, 0)),
                   pl.BlockSpec((nh, bm), lambda i: (0, i))],
        out_shape=[jax.ShapeDtypeStruct((m, d), jnp.bfloat16),
                   jax.ShapeDtypeStruct((nh, m), jnp.float32)],
        compiler_params=_params(("arbitrary",)),
        name="attn_norm",
    )(x2d, g, w_in_t)


def _matmul_kernel(a_ref, wt_ref, o_ref):
    w_t = wt_ref[...].astype(a_ref.dtype)
    for rows in _row_chunks(a_ref.shape[0]):
        o_ref[rows, :] = lax.dot_general(a_ref[rows, :], w_t, (((1,), (1,)), ((), ())),
                                         preferred_element_type=jnp.float32).astype(o_ref.dtype)


def _matmul(a, w_t, n_out, bm=ROW_TILE, bn=512, name="matmul"):
    m, k = a.shape
    return pl.pallas_call(
        _matmul_kernel,
        grid=(m // bm, n_out // bn),
        in_specs=[pl.BlockSpec((bm, k), lambda i, j: (i, 0)),
                  pl.BlockSpec((None, bn, k), lambda i, j: (0, j, 0))],
        out_specs=pl.BlockSpec((bm, bn), lambda i, j: (i, j)),
        out_shape=jax.ShapeDtypeStruct((m, n_out), jnp.bfloat16),
        compiler_params=_params(("arbitrary", "arbitrary"), vmem=VMEM_LIMIT_HIGH),
        name=name,
    )(a, w_t)


def _cumsum_kernel(fl_ref, bf_ref, c_ref):
    z = fl_ref[...] + bf_ref[...]
    x = jnp.minimum(z, 0.0) - jnp.log(1.0 + jnp.exp(-jnp.abs(z)))
    n = x.shape[-1]
    pos = lax.broadcasted_iota(jnp.int32, x.shape, 1)
    shift = 1
    while shift < n:
        x = x + jnp.where(pos >= shift, pltpu.roll(x, shift, 1), 0.0)
        shift *= 2
    c_ref[...] = x


def _cum_log_forget(fl_t, b_f, seq):
    nh, m = fl_t.shape
    return pl.pallas_call(
        _cumsum_kernel,
        grid=(m // seq,),
        in_specs=[pl.BlockSpec((nh, seq), lambda b: (0, b)),
                  pl.BlockSpec((nh, 1), lambda b: (0, 0))],
        out_specs=pl.BlockSpec((nh, seq), lambda b: (0, b)),
        out_shape=jax.ShapeDtypeStruct((nh, m), jnp.float32),
        compiler_params=_params(("arbitrary",)),
        name="cum_log_forget",
    )(fl_t, b_f)


def _scores_t(k, q):
    return lax.dot_general(k, q, (((1,), (1,)), ((), ())), preferred_element_type=jnp.float32)


def _softmax_pv_update(u, blk_max, shift, v_t, m_ref, l_ref, acc_ref, qs):
    blk = u.shape[1]
    m_old = m_ref[:, pl.ds(qs, blk)]
    m_new = jnp.maximum(m_old, blk_max + shift)
    alpha = jnp.exp2(m_old - m_new)
    p = jnp.exp2(u + (shift - m_new))
    if l_ref is not None:
        l_ref[:, pl.ds(qs, blk)] = (alpha * l_ref[:, pl.ds(qs, blk)]
                                    + jnp.sum(p, axis=0, keepdims=True))
    acc_ref[:, pl.ds(qs, blk)] = alpha * acc_ref[:, pl.ds(qs, blk)] + jnp.dot(
        v_t, p.astype(v_t.dtype), preferred_element_type=jnp.float32)
    m_ref[:, pl.ds(qs, blk)] = m_new


def _pair_schedule(n_blocks, with_near):
    far, near, diag = [], [], []
    for i in range(n_blocks):
        for j in range(i + 1):
            if j == i:
                diag.append((i, j))
            elif with_near and j == i - 1:
                near.append((i, j))
            else:
                far.append((i, j))
    pairs = far + near + diag
    kinds = [FAR] * len(far) + [NEAR] * len(near) + [DIAG] * len(diag)
    return kinds, np.asarray(pairs, np.int32).T.copy()


def _run_pipeline(kinds, stage_a, stage_b, unroll_all=False):
    n = len(kinds)
    ahead_kind = list(kinds[PIPE_DEPTH:]) + [None] * PIPE_DEPTH
    for t in range(min(PIPE_DEPTH, n)):
        stage_a(t, kinds[t], t % PIPE_BUFS)
    s = 0
    while s < n:
        e = s
        while e < n and ahead_kind[e] == ahead_kind[s]:
            e += 1
        kind, n_loop = ahead_kind[s], 0 if unroll_all else (e - s) // PIPE_BUFS

        def one(step, buf):
            if kind is not None:
                stage_a(step + PIPE_DEPTH, kind, (buf + PIPE_DEPTH) % PIPE_BUFS)
            stage_b(step, buf)

        if n_loop:
            def body(r, carry):
                base = s + PIPE_BUFS * r
                for i in range(PIPE_BUFS):
                    one(base + i, (s + i) % PIPE_BUFS)
                return carry

            lax.fori_loop(0, n_loop, body, 0)
        for step in range(s + n_loop * PIPE_BUFS, e):
            one(step, step % PIPE_BUFS)
        s = e


def _store_transposed(src_ref, dst_ref):
    for c in range(0, src_ref.shape[0], ATTN_BLOCK):
        dst_ref[:, c:c + ATTN_BLOCK] = (
            src_ref[c:c + ATTN_BLOCK, :].astype(jnp.float32).T.astype(dst_ref.dtype))


def _t5_bucket(n):
    max_exact = NUM_BUCKETS // 2
    nf = jnp.maximum(n, 1).astype(jnp.float32)
    large = max_exact + (jnp.log(nf / max_exact) / math.log(MAX_DISTANCE / max_exact)
                         * (NUM_BUCKETS - max_exact)).astype(jnp.int32)
    large = jnp.minimum(large, NUM_BUCKETS - 1)
    return jnp.where(n < max_exact, n, large)


def _diff_attn_kernel(kinds, pair_ref, tab_ref, lq1_ref, lk1_ref, lq2_ref, lk2_ref,
                      q_ref, k_ref, v_ref, g_ref, w_ref, o_ref, w_out_ref,
                      bias_near_ref, bias_diag_ref, vt_ref, *tile_and_state_refs):
    h = pl.program_id(0)
    b = pl.program_id(1)
    blk = ATTN_BLOCK
    w_out_ref[...] = w_ref[...].astype(w_out_ref.dtype)
    key = lax.broadcasted_iota(jnp.int32, (blk, blk), 0)
    qry = lax.broadcasted_iota(jnp.int32, (blk, blk), 1)
    u_refs = tile_and_state_refs[:PIPE_BUFS]
    bm_refs = tile_and_state_refs[PIPE_BUFS:2 * PIPE_BUFS]
    m_ref, l_ref, acc_ref = tile_and_state_refs[2 * PIPE_BUFS:]
    far_bias = tab_ref[NUM_BUCKETS - 1, h]

    @pl.when(b == 0)
    def _build_bias():
        for ref, offset in ((bias_diag_ref, 0), (bias_near_ref, blk)):
            bucket = _t5_bucket(jnp.maximum(qry - key + offset, 0))
            bias = jnp.zeros((blk, blk), jnp.float32)
            for t in range(NUM_BUCKETS):
                bias = jnp.where(bucket == t, tab_ref[t, h], bias)
            ref[...] = (bias - far_bias) * LOG2E

    _store_transposed(v_ref, vt_ref)
    m_ref[...] = jnp.full(m_ref.shape, NEG_INF, jnp.float32)
    l_ref[...] = jnp.zeros(l_ref.shape, jnp.float32)
    acc_ref[...] = jnp.zeros(acc_ref.shape, jnp.float32)

    def starts(t):
        return (pl.multiple_of(pair_ref[0, t] * blk, blk), pl.multiple_of(pair_ref[1, t] * blk, blk))

    def stage_a(t, kind, par):
        qs, ks = starts(t)
        for mp in range(2):
            cols = slice(mp * HEAD_DIM, (mp + 1) * HEAD_DIM)
            u = _scores_t(k_ref[pl.ds(ks, blk), cols], q_ref[pl.ds(qs, blk), cols]) * (SCALE * LOG2E)
            if kind == NEAR:
                u = u + bias_near_ref[...]
            elif kind == DIAG:
                u = jnp.where(qry >= key, u + bias_diag_ref[...], NEG_INF)
            u_refs[par][mp] = u
            bm_refs[par][mp] = jnp.max(u, axis=0, keepdims=True)

    def stage_b(t, par):
        qs, ks = starts(t)
        v_t = vt_ref[:, pl.ds(ks, blk)]
        for mp in range(2):
            _softmax_pv_update(u_refs[par][mp], bm_refs[par][mp], far_bias * LOG2E, v_t,
                               m_ref.at[mp], l_ref.at[mp], acc_ref.at[mp], qs)

    _run_pipeline(kinds, stage_a, stage_b)

    lam = (jnp.exp(jnp.sum(lq1_ref[...] * lk1_ref[...], axis=-1, keepdims=True))
           - jnp.exp(jnp.sum(lq2_ref[...] * lk2_ref[...], axis=-1, keepdims=True))
           + LAM_INIT)
    for c in range(0, o_ref.shape[0], blk):
        cols = slice(c, c + blk)
        o_t = (acc_ref[0, :, cols] * (1.0 / l_ref[0, :, cols])
               - lam * (acc_ref[1, :, cols] * (1.0 / l_ref[1, :, cols])))
        y_t = o_t * lax.rsqrt(jnp.mean(o_t * o_t, axis=0, keepdims=True) + EPS)
        y_t = (y_t * g_ref[...]) * (1.0 - LAM_INIT)
        o_ref[cols, :] = y_t.T.astype(o_ref.dtype)


def _slab_spec(weight, n_steps, step_of):
    rows, cols = weight.shape[-2:]
    slab = rows // n_steps
    assert slab * n_steps == rows and slab % BF16_SUBLANES == 0
    if weight.ndim == 3:
        return pl.BlockSpec((None, slab, cols), lambda *idx: (0, step_of(*idx), 0))
    return pl.BlockSpec((slab, cols), lambda *idx: (step_of(*idx), 0))


def _bf16_like(weight):
    return jax.ShapeDtypeStruct(weight.shape[-2:], jnp.bfloat16)


def _diff_attention(qkv, tab, lq1, lk1, lq2, lk2, g_col, w_conv):
    bsz, seq, _ = qkv.shape
    blk = ATTN_BLOCK
    w = 2 * HEAD_DIM
    k_off = DIFF_WIDTH // w
    v_off = 2 * DIFF_WIDTH // w
    kinds, pairs = _pair_schedule(seq // blk, with_near=True)
    smem = pl.BlockSpec(memory_space=pltpu.SMEM)
    vec = pl.BlockSpec((1, HEAD_DIM), lambda h, b: (0, 0))
    n_steps = N_DIFF_HEADS * bsz

    def step_of(h, b):
        return h * bsz + b

    return pl.pallas_call(
        functools.partial(_diff_attn_kernel, kinds),
        grid=(N_DIFF_HEADS, bsz),
        in_specs=[smem, smem, vec, vec, vec, vec,
                  pl.BlockSpec((None, seq, w), lambda h, b: (b, 0, h)),
                  pl.BlockSpec((None, seq, w), lambda h, b: (b, 0, k_off + h)),
                  pl.BlockSpec((None, seq, w), lambda h, b: (b, 0, v_off + h)),
                  pl.BlockSpec((w, 1), lambda h, b: (0, 0)),
                  _slab_spec(w_conv, n_steps, step_of)],
        out_specs=[pl.BlockSpec((None, seq, w), lambda h, b: (b, 0, h)),
                   _slab_spec(_bf16_like(w_conv), n_steps, step_of)],
        out_shape=[jax.ShapeDtypeStruct((bsz, seq, DIFF_WIDTH), jnp.bfloat16), _bf16_like(w_conv)],
        scratch_shapes=([pltpu.VMEM((blk, blk), jnp.float32),
                         pltpu.VMEM((blk, blk), jnp.float32),
                         pltpu.VMEM((w, seq), jnp.bfloat16)]
                        + [pltpu.VMEM((2, blk, blk), jnp.float32)] * PIPE_BUFS
                        + [pltpu.VMEM((2, 1, blk), jnp.float32)] * PIPE_BUFS
                        + [pltpu.VMEM((2, 1, seq), jnp.float32),
                           pltpu.VMEM((2, 1, seq), jnp.float32),
                           pltpu.VMEM((2, w, seq), jnp.float32)]),
        compiler_params=_params(("arbitrary", "arbitrary")),
        name="diff_attention",
    )(jnp.asarray(pairs), tab, lq1, lk1, lq2, lk2, qkv, qkv, qkv, g_col, w_conv)


def _fox_attn_kernel(kinds, pairs, q_ref, k_ref, v_ref, cum_ref, wa_ref, wb_ref,
                     o_ref, wa_out_ref, wb_out_ref,
                     vt_ref, q2t_ref, k2_ref, *tile_and_state_refs):
    blk = ATTN_BLOCK
    d = HEAD_DIM
    f32, bf16 = jnp.float32, jnp.bfloat16
    key = lax.broadcasted_iota(jnp.int32, (blk, blk), 0)
    qry = lax.broadcasted_iota(jnp.int32, (blk, blk), 1)
    lane = lax.broadcasted_iota(jnp.int32, (blk, LANES), 1)
    u_refs = tile_and_state_refs[:PIPE_BUFS]
    bm_refs = tile_and_state_refs[PIPE_BUFS:2 * PIPE_BUFS]
    m_ref, acc_ref = tile_and_state_refs[2 * PIPE_BUFS:]

    wa_out_ref[...] = wa_ref[...].astype(wa_out_ref.dtype)
    wb_out_ref[...] = wb_ref[...].astype(wb_out_ref.dtype)

    _store_transposed(v_ref, vt_ref.at[:d, :])
    vt_ref[d:, :] = jnp.ones((vt_ref.shape[0] - d, vt_ref.shape[1]), vt_ref.dtype)
    for c in range(0, q_ref.shape[0], blk):
        rows = slice(c, c + blk)
        x = jnp.broadcast_to(cum_ref[:, rows] * (-1.0 / SCALE), (LANES, blk)).T
        hi = x.astype(bf16).astype(f32)
        mid = (x - hi).astype(bf16).astype(f32)
        lo = ((x - hi) - mid).astype(bf16).astype(f32)
        aug = jnp.where(lane == 0, hi, jnp.where(lane == 1, mid, jnp.where(lane == 2, lo, 0.0)))
        k2_ref[rows, :d] = k_ref[rows, :]
        k2_ref[rows, d:] = aug.astype(bf16)
    _store_transposed(q_ref, q2t_ref.at[:d, :])
    ones_row = lax.broadcasted_iota(jnp.int32, (d, q2t_ref.shape[1]), 0) < 3
    q2t_ref[d:, :] = jnp.where(ones_row, 1.0, 0.0).astype(bf16)
    m_ref[...] = jnp.full(m_ref.shape, NEG_INF, jnp.float32)
    acc_ref[...] = jnp.zeros(acc_ref.shape, jnp.float32)

    def starts(t):
        return int(pairs[0][t]) * blk, int(pairs[1][t]) * blk

    def stage_a(t, kind, buf):
        qs, ks = starts(t)
        u = jnp.dot(k2_ref[pl.ds(ks, blk), :], q2t_ref[:, pl.ds(qs, blk)],
                    preferred_element_type=jnp.float32) * (SCALE * LOG2E)
        if kind == DIAG:
            u = jnp.where(qry >= key, u, NEG_INF)
        u_refs[buf][...] = u
        bm_refs[buf][...] = jnp.max(u, axis=0, keepdims=True)

    def stage_b(t, buf):
        qs, ks = starts(t)
        cq = cum_ref[:, pl.ds(qs, blk)] * LOG2E
        _softmax_pv_update(u_refs[buf][...], bm_refs[buf][...], cq, vt_ref[:, pl.ds(ks, blk)],
                           m_ref, None, acc_ref, qs)

    _run_pipeline(kinds, stage_a, stage_b, unroll_all=True)

    for c in range(0, o_ref.shape[0], blk):
        cols = slice(c, c + blk)
        o_t = acc_ref[:d, cols] * (1.0 / acc_ref[d:d + 1, cols])
        o_ref[cols, :] = o_t.T.astype(o_ref.dtype)


def _fox_attention(qkv, cum, w_conv_a, w_conv_b):
    bsz, seq, _ = qkv.shape
    blk = ATTN_BLOCK
    q_off = 3 * DIFF_WIDTH // HEAD_DIM
    k_off = q_off + N_FOX_HEADS
    v_off = k_off + N_FOX_HEADS
    kinds, pairs = _pair_schedule(seq // blk, with_near=False)
    n_steps = N_FOX_HEADS * bsz

    def step_of(h, b):
        return h * bsz + b

    return pl.pallas_call(
        functools.partial(_fox_attn_kernel, kinds, pairs),
        grid=(N_FOX_HEADS, bsz),
        in_specs=[pl.BlockSpec((None, seq, HEAD_DIM), lambda h, b: (b, 0, q_off + h)),
                  pl.BlockSpec((None, seq, HEAD_DIM), lambda h, b: (b, 0, k_off + h)),
                  pl.BlockSpec((None, seq, HEAD_DIM), lambda h, b: (b, 0, v_off + h)),
                  pl.BlockSpec((None, None, 1, seq), lambda h, b: (h, b, 0, 0)),
                  _slab_spec(w_conv_a, n_steps, step_of),
                  _slab_spec(w_conv_b, n_steps, step_of)],
        out_specs=[pl.BlockSpec((None, seq, HEAD_DIM), lambda h, b: (b, 0, h)),
                   _slab_spec(_bf16_like(w_conv_a), n_steps, step_of),
                   _slab_spec(_bf16_like(w_conv_b), n_steps, step_of)],
        out_shape=[jax.ShapeDtypeStruct((bsz, seq, FOX_WIDTH), jnp.bfloat16),
                   _bf16_like(w_conv_a), _bf16_like(w_conv_b)],
        scratch_shapes=([pltpu.VMEM((HEAD_DIM + FOX_SUM_ROWS, seq), jnp.bfloat16),
                         pltpu.VMEM((2 * HEAD_DIM, seq), jnp.bfloat16),
                         pltpu.VMEM((seq, 2 * HEAD_DIM), jnp.bfloat16)]
                        + [pltpu.VMEM((blk, blk), jnp.float32)] * PIPE_BUFS
                        + [pltpu.VMEM((1, blk), jnp.float32)] * PIPE_BUFS
                        + [pltpu.VMEM((1, seq), jnp.float32),
                           pltpu.VMEM((HEAD_DIM + FOX_SUM_ROWS, seq), jnp.float32)]),
        compiler_params=_params(("arbitrary", "arbitrary")),
        name="fox_attention",
    )(qkv, qkv, qkv, cum, w_conv_a, w_conv_b)


def _out_proj_kernel(ad_ref, af_ref, w_ref, res_ref, g_ref, x1_ref, xg_ref, ssq_ref):
    j = pl.program_id(1)
    kd = ad_ref.shape[1]
    wd = w_ref[:kd, :]
    wf = w_ref[kd:, :]
    parts = []
    for rows in _row_chunks(ad_ref.shape[0]):
        y = (res_ref[rows, :] + jnp.dot(ad_ref[rows, :], wd, preferred_element_type=jnp.float32)
             + jnp.dot(af_ref[rows, :], wf, preferred_element_type=jnp.float32))
        x1_ref[rows, :] = y
        xg_ref[rows, :] = (y * g_ref[...]).astype(xg_ref.dtype)
        sq = y * y
        part = sq[:, :LANES]
        for lc in range(LANES, sq.shape[1], LANES):
            part = part + sq[:, lc:lc + LANES]
        parts.append(part)
    part = jnp.concatenate(parts, axis=0)

    @pl.when(j == 0)
    def _first():
        ssq_ref[...] = part

    @pl.when(j > 0)
    def _rest():
        ssq_ref[...] += part


def _out_proj(a_diff, a_fox, w, res, g, bm=1024, bn=512):
    m, kd = a_diff.shape
    kf = a_fox.shape[1]
    d = w.shape[-1]
    return pl.pallas_call(
        _out_proj_kernel,
        grid=(m // bm, d // bn),
        in_specs=[pl.BlockSpec((bm, kd), lambda i, j: (i, 0)),
                  pl.BlockSpec((bm, kf), lambda i, j: (i, 0)),
                  pl.BlockSpec((kd + kf, bn), lambda i, j: (0, j)),
                  pl.BlockSpec((bm, bn), lambda i, j: (i, j)),
                  pl.BlockSpec((1, bn), lambda i, j: (0, j))],
        out_specs=[pl.BlockSpec((bm, bn), lambda i, j: (i, j)),
                   pl.BlockSpec((bm, bn), lambda i, j: (i, j)),
                   pl.BlockSpec((bm, LANES), lambda i, j: (i, 0))],
        out_shape=[jax.ShapeDtypeStruct((m, d), jnp.float32),
                   jax.ShapeDtypeStruct((m, d), jnp.bfloat16),
                   jax.ShapeDtypeStruct((m, LANES), jnp.float32)],
        compiler_params=_params(("arbitrary", "arbitrary")),
        name="out_proj",
    )(a_diff, a_fox, w, res, g)


def _down_kernel(a_ref, w_ref, res_ref, g_ref, o_ref, *, n_k, k_last, n_res):
    k = pl.program_id(1)
    bk = a_ref.shape[1]
    rc = res_ref.shape[1]

    def accumulate(k_len, first):
        a = a_ref[:, :k_len]
        for c in range(0, o_ref.shape[1], PROJ_COL_CHUNK):
            cols = slice(c, c + PROJ_COL_CHUNK)
            prod = jnp.dot(a, w_ref[:k_len, cols], preferred_element_type=jnp.float32)
            o_ref[:, cols] = prod if first else o_ref[:, cols] + prod

    pl.when(k == 0)(lambda: accumulate(bk, True))
    pl.when((k > 0) & (k < n_k - 1))(lambda: accumulate(bk, False))

    @pl.when(k < n_res)
    def _add_residual_tile():
        cols = pl.ds(pl.multiple_of(k * rc, rc), rc)
        o_ref[:, cols] += res_ref[...]

    @pl.when(k == n_k - 1)
    def _finish():
        accumulate(k_last, False)
        for r in range(0, o_ref.shape[0], PROJ_ROW_CHUNK):
            rows = slice(r, r + PROJ_ROW_CHUNK)
            x = o_ref[rows, :]
            o_ref[rows, :] = x * lax.rsqrt(jnp.mean(x * x, axis=-1, keepdims=True) + EPS) * g_ref[...]


def _down_proj_norm(a, w, res, g, bm=1024, bk=DOWN_K_TILE, rc=DOWN_RES_TILE):
    m, kdim = a.shape
    d = w.shape[1]
    n_k = pl.cdiv(kdim, bk)
    k_last = kdim - (n_k - 1) * bk
    n_res = d // rc
    assert n_res * rc == d and n_res <= n_k - 1
    row_block = pl.BlockSpec((bm, d), lambda i, k: (i, 0))
    return pl.pallas_call(
        functools.partial(_down_kernel, n_k=n_k, k_last=k_last, n_res=n_res),
        grid=(m // bm, n_k),
        in_specs=[pl.BlockSpec((bm, bk), lambda i, k: (i, k)),
                  pl.BlockSpec((bk, d), lambda i, k: (k, 0)),
                  pl.BlockSpec((bm, rc), lambda i, k: (i, jnp.minimum(k, n_res - 1))),
                  pl.BlockSpec((1, d), lambda i, k: (0, 0))],
        out_specs=row_block,
        out_shape=jax.ShapeDtypeStruct((m, d), jnp.float32),
        compiler_params=_params(("arbitrary", "arbitrary")),
        name="ffn_down",
    )(a, w, res, g)


def _gate_up_kernel(xg_ref, ssq_ref, wg_ref, wu_ref, wd_ref, o_ref, wd_out_ref, r_ref):
    @pl.when(pl.program_id(1) == 0)
    def _row_scale():
        mean_sq = jnp.sum(ssq_ref[...], axis=-1, keepdims=True) * (1.0 / xg_ref.shape[1])
        r_ref[...] = jnp.broadcast_to(lax.rsqrt(mean_sq + EPS), r_ref.shape)

    wg = wg_ref[...]
    wu = wu_ref[...]
    for rows in _row_chunks(xg_ref.shape[0]):
        xg = xg_ref[rows, :]
        r = jnp.concatenate([r_ref[rows, :]] * (o_ref.shape[1] // LANES), axis=1)
        gate = jnp.dot(xg, wg, preferred_element_type=jnp.float32) * r
        up = jnp.dot(xg, wu, preferred_element_type=jnp.float32) * r
        o_ref[rows, :] = (gate * jax.nn.sigmoid(gate) * up).astype(o_ref.dtype)
    wd_out_ref[...] = wd_ref[...].astype(wd_out_ref.dtype)


def _gate_up(xg, ssq, wg, wu, wd, bm=ROW_TILE, bn=FF_TILE):
    m, k = xg.shape
    f_rows, d = wd.shape[-2:]
    n = wg.shape[1]
    n_j = n // bn
    n_steps = (m // bm) * n_j

    def step_of(i, j):
        return i * n_j + j

    return pl.pallas_call(
        _gate_up_kernel,
        grid=(m // bm, n_j),
        in_specs=[pl.BlockSpec((bm, k), lambda i, j: (i, 0)),
                  pl.BlockSpec((bm, LANES), lambda i, j: (i, 0)),
                  pl.BlockSpec((k, bn), lambda i, j: (0, j)),
                  pl.BlockSpec((k, bn), lambda i, j: (0, j)),
                  _slab_spec(wd, n_steps, step_of)],
        out_specs=[pl.BlockSpec((bm, bn), lambda i, j: (i, j)),
                   _slab_spec(_bf16_like(wd), n_steps, step_of)],
        out_shape=[jax.ShapeDtypeStruct((m, n), jnp.bfloat16), _bf16_like(wd)],
        scratch_shapes=[pltpu.VMEM((bm, LANES), jnp.float32)],
        compiler_params=_params(("arbitrary", "arbitrary")),
        name="ffn_gate_up",
    )(xg, ssq, wg, wu, wd)


def kernel(x, attn_norm_g, w_in, b_f, lambda_q1, lambda_k1, lambda_q2, lambda_k2, rel_bias_table,
           diff_subln_g, w_o, ffn_norm_g, w_gate, w_up, w_down, final_norm_g):
    bsz, seq, d = x.shape
    m = bsz * seq
    x2d = x.reshape(m, d)

    w_in_t = jnp.swapaxes(w_in, 1, 2)
    h, fl_t = _norm_and_gate_logits(x2d, attn_norm_g[0].reshape(1, d), w_in_t, QKV_COLS, N_FOX_HEADS)
    qkv = _matmul(h, w_in_t, QKV_COLS, name="in_proj").reshape(bsz, seq, QKV_COLS)

    cum = _cum_log_forget(fl_t, b_f[0].reshape(N_FOX_HEADS, 1), seq)
    cum = cum.reshape(N_FOX_HEADS, bsz, 1, seq)

    o_diff, w_o_b = _diff_attention(
        qkv, rel_bias_table,
        lambda_q1[0].reshape(1, HEAD_DIM), lambda_k1[0].reshape(1, HEAD_DIM),
        lambda_q2[0].reshape(1, HEAD_DIM), lambda_k2[0].reshape(1, HEAD_DIM),
        diff_subln_g[0].reshape(2 * HEAD_DIM, 1), w_o)
    o_fox, w_gate_b, w_up_b = _fox_attention(qkv, cum, w_gate, w_up)

    x1, x1g, ssq = _out_proj(o_diff.reshape(m, DIFF_WIDTH), o_fox.reshape(m, FOX_WIDTH), w_o_b, x2d,
                             ffn_norm_g[0].reshape(1, d))
    act, w_down_b = _gate_up(x1g, ssq, w_gate_b, w_up_b, w_down)
    out = _down_proj_norm(act, w_down_b, x1, final_norm_g.reshape(1, d))
    return out.reshape(bsz, seq, d)
```

```python
import functools
import math

import numpy as np

import jax
import jax.numpy as jnp
from jax import lax
from jax.experimental import pallas as pl
from jax.experimental.pallas import tpu as pltpu

D_MODEL = 4096
HEAD_DIM = 128
N_DIFF_HEADS = D_MODEL // (4 * HEAD_DIM)
N_FOX_HEADS = D_MODEL // (2 * HEAD_DIM)
DIFF_WIDTH = N_DIFF_HEADS * 2 * HEAD_DIM
FOX_WIDTH = N_FOX_HEADS * HEAD_DIM
QKV_COLS = 3 * DIFF_WIDTH + 3 * FOX_WIDTH
D_FF = ((8 * D_MODEL + 3 * 256 - 1) // (3 * 256)) * 256
NUM_BUCKETS = 32
MAX_DISTANCE = 128
EPS = 1e-6
NEG_INF = -1e30
LAM_INIT = 0.8 - 0.6 * math.exp(-0.3 * 0)
SCALE = HEAD_DIM ** -0.5
LOG2E = math.log2(math.e)

LANES = 128
BF16_SUBLANES = 16
ROW_TILE = 2048
FF_TILE = 256
DOWN_K_TILE = 512
DOWN_RES_TILE = 256
VMEM_LIMIT = 56 * 1024 * 1024
VMEM_LIMIT_HIGH = 60 * 1024 * 1024

ATTN_BLOCK = 512
PIPE_DEPTH = 2
PIPE_BUFS = PIPE_DEPTH + 1
FOX_SUM_ROWS = BF16_SUBLANES
MATMUL_ROW_CHUNK = 512
PROJ_COL_CHUNK = 1024
PROJ_ROW_CHUNK = 128

FAR, NEAR, DIAG = "far", "near", "diag"


def _params(sem, vmem=VMEM_LIMIT):
    return pltpu.CompilerParams(dimension_semantics=sem, vmem_limit_bytes=vmem)


def _row_chunks(n_rows):
    return [slice(s, s + MATMUL_ROW_CHUNK) for s in range(0, n_rows, MATMUL_ROW_CHUNK)]


def _norm_kernel(x_ref, g_ref, wf_ref, h_ref, fl_ref):
    x = x_ref[...]
    y = x * lax.rsqrt(jnp.mean(x * x, axis=-1, keepdims=True) + EPS)
    h = (y * g_ref[...]).astype(jnp.bfloat16)
    h_ref[...] = h
    fl_ref[...] = lax.dot_general(wf_ref[...].astype(jnp.bfloat16), h, (((1,), (1,)), ((), ())),
                                  preferred_element_type=jnp.float32)


def _norm_and_gate_logits(x2d, g, w_in_t, gate_row, nh, bm=512):
    m, d = x2d.shape
    assert gate_row % nh == 0 and gate_row + nh == w_in_t.shape[1]
    return pl.pallas_call(
        _norm_kernel,
        grid=(m // bm,),
        in_specs=[pl.BlockSpec((bm, d), lambda i: (i, 0)),
                  pl.BlockSpec((1, d), lambda i: (0, 0)),
                  pl.BlockSpec((None, nh, d), lambda i: (0, gate_row // nh, 0))],
        out_specs=[pl.BlockSpec((bm, d), lambda i: (i, 0)),
                   pl.BlockSpec((nh, bm), lambda i: (0, i))],
        out_shape=[jax.ShapeDtypeStruct((m, d), jnp.bfloat16),
                   jax.ShapeDtypeStruct((nh, m), jnp.float32)],
        compiler_params=_params(("arbitrary",)),
        name="attn_norm",
    )(x2d, g, w_in_t)


def _matmul_kernel(a_ref, wt_ref, o_ref):
    w_t = wt_ref[...].astype(a_ref.dtype)
    for rows in _row_chunks(a_ref.shape[0]):
        o_ref[rows, :] = lax.dot_general(a_ref[rows, :], w_t, (((1,), (1,)), ((), ())),
                                         preferred_element_type=jnp.float32).astype(o_ref.dtype)


def _matmul(a, w_t, n_out, bm=ROW_TILE, bn=512, name="matmul"):
    m, k = a.shape
    return pl.pallas_call(
        _matmul_kernel,
        grid=(m // bm, n_out // bn),
        in_specs=[pl.BlockSpec((bm, k), lambda i, j: (i, 0)),
                  pl.BlockSpec((None, bn, k), lambda i, j: (0, j, 0))],
        out_specs=pl.BlockSpec((bm, bn), lambda i, j: (i, j)),
        out_shape=jax.ShapeDtypeStruct((m, n_out), jnp.bfloat16),
        compiler_params=_params(("arbitrary", "arbitrary"), vmem=VMEM_LIMIT_HIGH),
        name=name,
    )(a, w_t)


def _cumsum_kernel(fl_ref, bf_ref, c_ref):
    z = fl_ref[...] + bf_ref[...]
    x = jnp.minimum(z, 0.0) - jnp.log(1.0 + jnp.exp(-jnp.abs(z)))
    n = x.shape[-1]
    pos = lax.broadcasted_iota(jnp.int32, x.shape, 1)
    shift = 1
    while shift < n:
        x = x + jnp.where(pos >= shift, pltpu.roll(x, shift, 1), 0.0)
        shift *= 2
    c_ref[...] = x


def _cum_log_forget(fl_t, b_f, seq):
    nh, m = fl_t.shape
    return pl.pallas_call(
        _cumsum_kernel,
        grid=(m // seq,),
        in_specs=[pl.BlockSpec((nh, seq), lambda b: (0, b)),
                  pl.BlockSpec((nh, 1), lambda b: (0, 0))],
        out_specs=pl.BlockSpec((nh, seq), lambda b: (0, b)),
        out_shape=jax.ShapeDtypeStruct((nh, m), jnp.float32),
        compiler_params=_params(("arbitrary",)),
        name="cum_log_forget",
    )(fl_t, b_f)


def _softmax_pv_update(u, blk_max, shift, v_t, m_ref, l_ref, acc_ref, qs):
    blk = u.shape[1]
    m_old = m_ref[:, pl.ds(qs, blk)]
    m_new = jnp.maximum(m_old, blk_max + shift)
    alpha = jnp.exp2(m_old - m_new)
    p = jnp.exp2(u + (shift - m_new))
    if l_ref is not None:
        l_ref[:, pl.ds(qs, blk)] = (alpha * l_ref[:, pl.ds(qs, blk)]
                                    + jnp.sum(p, axis=0, keepdims=True))
    acc_ref[:, pl.ds(qs, blk)] = alpha * acc_ref[:, pl.ds(qs, blk)] + jnp.dot(
        v_t, p.astype(v_t.dtype), preferred_element_type=jnp.float32)
    m_ref[:, pl.ds(qs, blk)] = m_new


def _pair_schedule(n_blocks, with_near):
    far, near, diag = [], [], []
    for i in range(n_blocks):
        for j in range(i + 1):
            if j == i:
                diag.append((i, j))
            elif with_near and j == i - 1:
                near.append((i, j))
            else:
                far.append((i, j))
    pairs = far + near + diag
    kinds = [FAR] * len(far) + [NEAR] * len(near) + [DIAG] * len(diag)
    return kinds, np.asarray(pairs, np.int32).T.copy()


def _run_pipeline(kinds, stage_a, stage_b, unroll_all=False):
    n = len(kinds)
    ahead_kind = list(kinds[PIPE_DEPTH:]) + [None] * PIPE_DEPTH
    for t in range(min(PIPE_DEPTH, n)):
        stage_a(t, kinds[t], t % PIPE_BUFS)
    s = 0
    while s < n:
        e = s
        while e < n and ahead_kind[e] == ahead_kind[s]:
            e += 1
        kind, n_loop = ahead_kind[s], 0 if unroll_all else (e - s) // PIPE_BUFS

        def one(step, buf):
            if kind is not None:
                stage_a(step + PIPE_DEPTH, kind, (buf + PIPE_DEPTH) % PIPE_BUFS)
            stage_b(step, buf)

        if n_loop:
            def body(r, carry):
                base = s + PIPE_BUFS * r
                for i in range(PIPE_BUFS):
                    one(base + i, (s + i) % PIPE_BUFS)
                return carry

            lax.fori_loop(0, n_loop, body, 0)
        for step in range(s + n_loop * PIPE_BUFS, e):
            one(step, step % PIPE_BUFS)
        s = e


def _store_transposed(src_ref, dst_ref):
    for c in range(0, src_ref.shape[0], ATTN_BLOCK):
        dst_ref[:, c:c + ATTN_BLOCK] = (
            src_ref[c:c + ATTN_BLOCK, :].astype(jnp.float32).T.astype(dst_ref.dtype))


def _t5_bucket(n):
    max_exact = NUM_BUCKETS // 2
    nf = jnp.maximum(n, 1).astype(jnp.float32)
    large = max_exact + (jnp.log(nf / max_exact) / math.log(MAX_DISTANCE / max_exact)
                         * (NUM_BUCKETS - max_exact)).astype(jnp.int32)
    large = jnp.minimum(large, NUM_BUCKETS - 1)
    return jnp.where(n < max_exact, n, large)


def _diff_attn_kernel(kinds, pair_ref, tab_ref, lq1_ref, lk1_ref, lq2_ref, lk2_ref,
                      q_ref, k_ref, v_ref, g_ref, w_ref, o_ref, w_out_ref,
                      bias_near_ref, bias_diag_ref, vt_ref, qt_ref, *tile_and_state_refs):
    h = pl.program_id(0)
    b = pl.program_id(1)
    blk = ATTN_BLOCK
    w_out_ref[...] = w_ref[...].astype(w_out_ref.dtype)
    key = lax.broadcasted_iota(jnp.int32, (blk, blk), 0)
    qry = lax.broadcasted_iota(jnp.int32, (blk, blk), 1)
    u_refs = tile_and_state_refs[:PIPE_BUFS]
    bm_refs = tile_and_state_refs[PIPE_BUFS:2 * PIPE_BUFS]
    m_ref, l_ref, acc_ref = tile_and_state_refs[2 * PIPE_BUFS:]
    far_bias = tab_ref[NUM_BUCKETS - 1, h]

    @pl.when(b == 0)
    def _build_bias():
        for ref, offset in ((bias_diag_ref, 0), (bias_near_ref, blk)):
            bucket = _t5_bucket(jnp.maximum(qry - key + offset, 0))
            bias = jnp.zeros((blk, blk), jnp.float32)
            for t in range(NUM_BUCKETS):
                bias = jnp.where(bucket == t, tab_ref[t, h], bias)
            ref[...] = (bias - far_bias) * LOG2E

    _store_transposed(v_ref, vt_ref)
    _store_transposed(q_ref, qt_ref)
    m_ref[...] = jnp.full(m_ref.shape, NEG_INF, jnp.float32)
    l_ref[...] = jnp.zeros(l_ref.shape, jnp.float32)
    acc_ref[...] = jnp.zeros(acc_ref.shape, jnp.float32)

    def starts(t):
        return (pl.multiple_of(pair_ref[0, t] * blk, blk), pl.multiple_of(pair_ref[1, t] * blk, blk))

    def stage_a(t, kind, par):
        qs, ks = starts(t)
        for mp in range(2):
            cols = slice(mp * HEAD_DIM, (mp + 1) * HEAD_DIM)
            u = jnp.dot(k_ref[pl.ds(ks, blk), cols], qt_ref[cols, pl.ds(qs, blk)],
                        preferred_element_type=jnp.float32) * (SCALE * LOG2E)
            if kind == NEAR:
                u = u + bias_near_ref[...]
            elif kind == DIAG:
                u = jnp.where(qry >= key, u + bias_diag_ref[...], NEG_INF)
            u_refs[par][mp] = u
            bm_refs[par][mp] = jnp.max(u, axis=0, keepdims=True)

    def stage_b(t, par):
        qs, ks = starts(t)
        v_t = vt_ref[:, pl.ds(ks, blk)]
        for mp in range(2):
            _softmax_pv_update(u_refs[par][mp], bm_refs[par][mp], far_bias * LOG2E, v_t,
                               m_ref.at[mp], l_ref.at[mp], acc_ref.at[mp], qs)

    _run_pipeline(kinds, stage_a, stage_b)

    lam = (jnp.exp(jnp.sum(lq1_ref[...] * lk1_ref[...], axis=-1, keepdims=True))
           - jnp.exp(jnp.sum(lq2_ref[...] * lk2_ref[...], axis=-1, keepdims=True))
           + LAM_INIT)
    for c in range(0, o_ref.shape[0], blk):
        cols = slice(c, c + blk)
        o_t = (acc_ref[0, :, cols] * (1.0 / l_ref[0, :, cols])
               - lam * (acc_ref[1, :, cols] * (1.0 / l_ref[1, :, cols])))
        y_t = o_t * lax.rsqrt(jnp.mean(o_t * o_t, axis=0, keepdims=True) + EPS)
        y_t = (y_t * g_ref[...]) * (1.0 - LAM_INIT)
        o_ref[cols, :] = y_t.T.astype(o_ref.dtype)


def _slab_spec(weight, n_steps, step_of):
    rows, cols = weight.shape[-2:]
    slab = rows // n_steps
    assert slab * n_steps == rows and slab % BF16_SUBLANES == 0
    if weight.ndim == 3:
        return pl.BlockSpec((None, slab, cols), lambda *idx: (0, step_of(*idx), 0))
    return pl.BlockSpec((slab, cols), lambda *idx: (step_of(*idx), 0))


def _bf16_like(weight):
    return jax.ShapeDtypeStruct(weight.shape[-2:], jnp.bfloat16)


def _diff_attention(qkv, tab, lq1, lk1, lq2, lk2, g_col, w_conv):
    bsz, seq, _ = qkv.shape
    blk = ATTN_BLOCK
    w = 2 * HEAD_DIM
    k_off = DIFF_WIDTH // w
    v_off = 2 * DIFF_WIDTH // w
    kinds, pairs = _pair_schedule(seq // blk, with_near=True)
    smem = pl.BlockSpec(memory_space=pltpu.SMEM)
    vec = pl.BlockSpec((1, HEAD_DIM), lambda h, b: (0, 0))
    n_steps = N_DIFF_HEADS * bsz

    def step_of(h, b):
        return h * bsz + b

    return pl.pallas_call(
        functools.partial(_diff_attn_kernel, kinds),
        grid=(N_DIFF_HEADS, bsz),
        in_specs=[smem, smem, vec, vec, vec, vec,
                  pl.BlockSpec((None, seq, w), lambda h, b: (b, 0, h)),
                  pl.BlockSpec((None, seq, w), lambda h, b: (b, 0, k_off + h)),
                  pl.BlockSpec((None, seq, w), lambda h, b: (b, 0, v_off + h)),
                  pl.BlockSpec((w, 1), lambda h, b: (0, 0)),
                  _slab_spec(w_conv, n_steps, step_of)],
        out_specs=[pl.BlockSpec((None, seq, w), lambda h, b: (b, 0, h)),
                   _slab_spec(_bf16_like(w_conv), n_steps, step_of)],
        out_shape=[jax.ShapeDtypeStruct((bsz, seq, DIFF_WIDTH), jnp.bfloat16), _bf16_like(w_conv)],
        scratch_shapes=([pltpu.VMEM((blk, blk), jnp.float32),
                         pltpu.VMEM((blk, blk), jnp.float32),
                         pltpu.VMEM((w, seq), jnp.bfloat16),
                         pltpu.VMEM((w, seq), jnp.bfloat16)]
                        + [pltpu.VMEM((2, blk, blk), jnp.float32)] * PIPE_BUFS
                        + [pltpu.VMEM((2, 1, blk), jnp.float32)] * PIPE_BUFS
                        + [pltpu.VMEM((2, 1, seq), jnp.float32),
                           pltpu.VMEM((2, 1, seq), jnp.float32),
                           pltpu.VMEM((2, w, seq), jnp.float32)]),
        compiler_params=_params(("arbitrary", "arbitrary")),
        name="diff_attention",
    )(jnp.asarray(pairs), tab, lq1, lk1, lq2, lk2, qkv, qkv, qkv, g_col, w_conv)


def _fox_attn_kernel(kinds, pairs, q_ref, k_ref, v_ref, cum_ref, wa_ref, wb_ref,
                     o_ref, wa_out_ref, wb_out_ref,
                     vt_ref, q2t_ref, k2_ref, *tile_and_state_refs):
    blk = ATTN_BLOCK
    d = HEAD_DIM
    f32, bf16 = jnp.float32, jnp.bfloat16
    key = lax.broadcasted_iota(jnp.int32, (blk, blk), 0)
    qry = lax.broadcasted_iota(jnp.int32, (blk, blk), 1)
    lane = lax.broadcasted_iota(jnp.int32, (blk, LANES), 1)
    u_refs = tile_and_state_refs[:PIPE_BUFS]
    bm_refs = tile_and_state_refs[PIPE_BUFS:2 * PIPE_BUFS]
    m_ref, acc_ref = tile_and_state_refs[2 * PIPE_BUFS:]

    wa_out_ref[...] = wa_ref[...].astype(wa_out_ref.dtype)
    wb_out_ref[...] = wb_ref[...].astype(wb_out_ref.dtype)

    _store_transposed(v_ref, vt_ref.at[:d, :])
    vt_ref[d:, :] = jnp.ones((vt_ref.shape[0] - d, vt_ref.shape[1]), vt_ref.dtype)
    for c in range(0, q_ref.shape[0], blk):
        rows = slice(c, c + blk)
        x = jnp.broadcast_to(cum_ref[:, rows] * (-1.0 / SCALE), (LANES, blk)).T
        hi = x.astype(bf16).astype(f32)
        mid = (x - hi).astype(bf16).astype(f32)
        lo = ((x - hi) - mid).astype(bf16).astype(f32)
        aug = jnp.where(lane == 0, hi, jnp.where(lane == 1, mid, jnp.where(lane == 2, lo, 0.0)))
        k2_ref[rows, :d] = k_ref[rows, :]
        k2_ref[rows, d:] = aug.astype(bf16)
    _store_transposed(q_ref, q2t_ref.at[:d, :])
    ones_row = lax.broadcasted_iota(jnp.int32, (d, q2t_ref.shape[1]), 0) < 3
    q2t_ref[d:, :] = jnp.where(ones_row, 1.0, 0.0).astype(bf16)
    m_ref[...] = jnp.full(m_ref.shape, NEG_INF, jnp.float32)
    acc_ref[...] = jnp.zeros(acc_ref.shape, jnp.float32)

    def starts(t):
        return int(pairs[0][t]) * blk, int(pairs[1][t]) * blk

    def stage_a(t, kind, buf):
        qs, ks = starts(t)
        u = jnp.dot(k2_ref[pl.ds(ks, blk), :], q2t_ref[:, pl.ds(qs, blk)],
                    preferred_element_type=jnp.float32) * (SCALE * LOG2E)
        if kind == DIAG:
            u = jnp.where(qry >= key, u, NEG_INF)
        u_refs[buf][...] = u
        bm_refs[buf][...] = jnp.max(u, axis=0, keepdims=True)

    def stage_b(t, buf):
        qs, ks = starts(t)
        cq = cum_ref[:, pl.ds(qs, blk)] * LOG2E
        _softmax_pv_update(u_refs[buf][...], bm_refs[buf][...], cq, vt_ref[:, pl.ds(ks, blk)],
                           m_ref, None, acc_ref, qs)

    _run_pipeline(kinds, stage_a, stage_b, unroll_all=True)

    for c in range(0, o_ref.shape[0], blk):
        cols = slice(c, c + blk)
        o_t = acc_ref[:d, cols] * (1.0 / acc_ref[d:d + 1, cols])
        o_ref[cols, :] = o_t.T.astype(o_ref.dtype)


def _fox_attention(qkv, cum, w_conv_a, w_conv_b):
    bsz, seq, _ = qkv.shape
    blk = ATTN_BLOCK
    q_off = 3 * DIFF_WIDTH // HEAD_DIM
    k_off = q_off + N_FOX_HEADS
    v_off = k_off + N_FOX_HEADS
    kinds, pairs = _pair_schedule(seq // blk, with_near=False)
    n_steps = N_FOX_HEADS * bsz

    def step_of(h, b):
        return h * bsz + b

    return pl.pallas_call(
        functools.partial(_fox_attn_kernel, kinds, pairs),
        grid=(N_FOX_HEADS, bsz),
        in_specs=[pl.BlockSpec((None, seq, HEAD_DIM), lambda h, b: (b, 0, q_off + h)),
                  pl.BlockSpec((None, seq, HEAD_DIM), lambda h, b: (b, 0, k_off + h)),
                  pl.BlockSpec((None, seq, HEAD_DIM), lambda h, b: (b, 0, v_off + h)),
                  pl.BlockSpec((None, None, 1, seq), lambda h, b: (h, b, 0, 0)),
                  _slab_spec(w_conv_a, n_steps, step_of),
                  _slab_spec(w_conv_b, n_steps, step_of)],
        out_specs=[pl.BlockSpec((None, seq, HEAD_DIM), lambda h, b: (b, 0, h)),
                   _slab_spec(_bf16_like(w_conv_a), n_steps, step_of),
                   _slab_spec(_bf16_like(w_conv_b), n_steps, step_of)],
        out_shape=[jax.ShapeDtypeStruct((bsz, seq, FOX_WIDTH), jnp.bfloat16),
                   _bf16_like(w_conv_a), _bf16_like(w_conv_b)],
        scratch_shapes=([pltpu.VMEM((HEAD_DIM + FOX_SUM_ROWS, seq), jnp.bfloat16),
                         pltpu.VMEM((2 * HEAD_DIM, seq), jnp.bfloat16),
                         pltpu.VMEM((seq, 2 * HEAD_DIM), jnp.bfloat16)]
                        + [pltpu.VMEM((blk, blk), jnp.float32)] * PIPE_BUFS
                        + [pltpu.VMEM((1, blk), jnp.float32)] * PIPE_BUFS
                        + [pltpu.VMEM((1, seq), jnp.float32),
                           pltpu.VMEM((HEAD_DIM + FOX_SUM_ROWS, seq), jnp.float32)]),
        compiler_params=_params(("arbitrary", "arbitrary")),
        name="fox_attention",
    )(qkv, qkv, qkv, cum, w_conv_a, w_conv_b)


def _out_proj_kernel(ad_ref, af_ref, w_ref, res_ref, g_ref, x1_ref, xg_ref, ssq_ref):
    j = pl.program_id(1)
    kd = ad_ref.shape[1]
    wd = w_ref[:kd, :]
    wf = w_ref[kd:, :]
    parts = []
    for rows in _row_chunks(ad_ref.shape[0]):
        y = (res_ref[rows, :] + jnp.dot(ad_ref[rows, :], wd, preferred_element_type=jnp.float32)
             + jnp.dot(af_ref[rows, :], wf, preferred_element_type=jnp.float32))
        x1_ref[rows, :] = y
        xg_ref[rows, :] = (y * g_ref[...]).astype(xg_ref.dtype)
        sq = y * y
        part = sq[:, :LANES]
        for lc in range(LANES, sq.shape[1], LANES):
            part = part + sq[:, lc:lc + LANES]
        parts.append(part)
    part = jnp.concatenate(parts, axis=0)

    @pl.when(j == 0)
    def _first():
        ssq_ref[...] = part

    @pl.when(j > 0)
    def _rest():
        ssq_ref[...] += part


def _out_proj(a_diff, a_fox, w, res, g, bm=1024, bn=512):
    m, kd = a_diff.shape
    kf = a_fox.shape[1]
    d = w.shape[-1]
    return pl.pallas_call(
        _out_proj_kernel,
        grid=(m // bm, d // bn),
        in_specs=[pl.BlockSpec((bm, kd), lambda i, j: (i, 0)),
                  pl.BlockSpec((bm, kf), lambda i, j: (i, 0)),
                  pl.BlockSpec((kd + kf, bn), lambda i, j: (0, j)),
                  pl.BlockSpec((bm, bn), lambda i, j: (i, j)),
                  pl.BlockSpec((1, bn), lambda i, j: (0, j))],
        out_specs=[pl.BlockSpec((bm, bn), lambda i, j: (i, j)),
                   pl.BlockSpec((bm, bn), lambda i, j: (i, j)),
                   pl.BlockSpec((bm, LANES), lambda i, j: (i, 0))],
        out_shape=[jax.ShapeDtypeStruct((m, d), jnp.float32),
                   jax.ShapeDtypeStruct((m, d), jnp.bfloat16),
                   jax.ShapeDtypeStruct((m, LANES), jnp.float32)],
        compiler_params=_params(("arbitrary", "arbitrary")),
        name="out_proj",
    )(a_diff, a_fox, w, res, g)


def _down_kernel(a_ref, w_ref, res_ref, g_ref, o_ref, *, n_k, k_last, n_res):
    k = pl.program_id(1)
    bk = a_ref.shape[1]
    rc = res_ref.shape[1]

    def accumulate(k_len, first):
        a = a_ref[:, :k_len]
        for c in range(0, o_ref.shape[1], PROJ_COL_CHUNK):
            cols = slice(c, c + PROJ_COL_CHUNK)
            prod = jnp.dot(a, w_ref[:k_len, cols], preferred_element_type=jnp.float32)
            o_ref[:, cols] = prod if first else o_ref[:, cols] + prod

    pl.when(k == 0)(lambda: accumulate(bk, True))
    pl.when((k > 0) & (k < n_k - 1))(lambda: accumulate(bk, False))

    @pl.when(k < n_res)
    def _add_residual_tile():
        cols = pl.ds(pl.multiple_of(k * rc, rc), rc)
        o_ref[:, cols] += res_ref[...]

    @pl.when(k == n_k - 1)
    def _finish():
        accumulate(k_last, False)
        for r in range(0, o_ref.shape[0], PROJ_ROW_CHUNK):
            rows = slice(r, r + PROJ_ROW_CHUNK)
            x = o_ref[rows, :]
            o_ref[rows, :] = x * lax.rsqrt(jnp.mean(x * x, axis=-1, keepdims=True) + EPS) * g_ref[...]


def _down_proj_norm(a, w, res, g, bm=1024, bk=DOWN_K_TILE, rc=DOWN_RES_TILE):
    m, kdim = a.shape
    d = w.shape[1]
    n_k = pl.cdiv(kdim, bk)
    k_last = kdim - (n_k - 1) * bk
    n_res = d // rc
    assert n_res * rc == d and n_res <= n_k - 1
    row_block = pl.BlockSpec((bm, d), lambda i, k: (i, 0))
    return pl.pallas_call(
        functools.partial(_down_kernel, n_k=n_k, k_last=k_last, n_res=n_res),
        grid=(m // bm, n_k),
        in_specs=[pl.BlockSpec((bm, bk), lambda i, k: (i, k)),
                  pl.BlockSpec((bk, d), lambda i, k: (k, 0)),
                  pl.BlockSpec((bm, rc), lambda i, k: (i, jnp.minimum(k, n_res - 1))),
                  pl.BlockSpec((1, d), lambda i, k: (0, 0))],
        out_specs=row_block,
        out_shape=jax.ShapeDtypeStruct((m, d), jnp.float32),
        compiler_params=_params(("arbitrary", "arbitrary")),
        name="ffn_down",
    )(a, w, res, g)


def _gate_up_kernel(xg_ref, ssq_ref, wg_ref, wu_ref, wd_ref, o_ref, wd_out_ref, r_ref):
    @pl.when(pl.program_id(1) == 0)
    def _row_scale():
        mean_sq = jnp.sum(ssq_ref[...], axis=-1, keepdims=True) * (1.0 / xg_ref.shape[1])
        r_ref[...] = jnp.broadcast_to(lax.rsqrt(mean_sq + EPS), r_ref.shape)

    wg = wg_ref[...]
    wu = wu_ref[...]
    for rows in _row_chunks(xg_ref.shape[0]):
        xg = xg_ref[rows, :]
        r = jnp.concatenate([r_ref[rows, :]] * (o_ref.shape[1] // LANES), axis=1)
        gate = jnp.dot(xg, wg, preferred_element_type=jnp.float32) * r
        up = jnp.dot(xg, wu, preferred_element_type=jnp.float32) * r
        o_ref[rows, :] = (gate * jax.nn.sigmoid(gate) * up).astype(o_ref.dtype)
    wd_out_ref[...] = wd_ref[...].astype(wd_out_ref.dtype)


def _gate_up(xg, ssq, wg, wu, wd, bm=ROW_TILE, bn=FF_TILE):
    m, k = xg.shape
    f_rows, d = wd.shape[-2:]
    n = wg.shape[1]
    n_j = n // bn
    n_steps = (m // bm) * n_j

    def step_of(i, j):
        return i * n_j + j

    return pl.pallas_call(
        _gate_up_kernel,
        grid=(m // bm, n_j),
        in_specs=[pl.BlockSpec((bm, k), lambda i, j: (i, 0)),
                  pl.BlockSpec((bm, LANES), lambda i, j: (i, 0)),
                  pl.BlockSpec((k, bn), lambda i, j: (0, j)),
                  pl.BlockSpec((k, bn), lambda i, j: (0, j)),
                  _slab_spec(wd, n_steps, step_of)],
        out_specs=[pl.BlockSpec((bm, bn), lambda i, j: (i, j)),
                   _slab_spec(_bf16_like(wd), n_steps, step_of)],
        out_shape=[jax.ShapeDtypeStruct((m, n), jnp.bfloat16), _bf16_like(wd)],
        scratch_shapes=[pltpu.VMEM((bm, LANES), jnp.float32)],
        compiler_params=_params(("arbitrary", "arbitrary")),
        name="ffn_gate_up",
    )(xg, ssq, wg, wu, wd)


def kernel(x, attn_norm_g, w_in, b_f, lambda_q1, lambda_k1, lambda_q2, lambda_k2, rel_bias_table,
           diff_subln_g, w_o, ffn_norm_g, w_gate, w_up, w_down, final_norm_g):
    bsz, seq, d = x.shape
    m = bsz * seq
    x2d = x.reshape(m, d)

    w_in_t = jnp.swapaxes(w_in, 1, 2)
    h, fl_t = _norm_and_gate_logits(x2d, attn_norm_g[0].reshape(1, d), w_in_t, QKV_COLS, N_FOX_HEADS)
    qkv = _matmul(h, w_in_t, QKV_COLS, name="in_proj").reshape(bsz, seq, QKV_COLS)

    cum = _cum_log_forget(fl_t, b_f[0].reshape(N_FOX_HEADS, 1), seq)
    cum = cum.reshape(N_FOX_HEADS, bsz, 1, seq)

    o_diff, w_o_b = _diff_attention(
        qkv, rel_bias_table,
        lambda_q1[0].reshape(1, HEAD_DIM), lambda_k1[0].reshape(1, HEAD_DIM),
        lambda_q2[0].reshape(1, HEAD_DIM), lambda_k2[0].reshape(1, HEAD_DIM),
        diff_subln_g[0].reshape(2 * HEAD_DIM, 1), w_o)
    o_fox, w_gate_b, w_up_b = _fox_attention(qkv, cum, w_gate, w_up)

    x1, x1g, ssq = _out_proj(o_diff.reshape(m, DIFF_WIDTH), o_fox.reshape(m, FOX_WIDTH), w_o_b, x2d,
                             ffn_norm_g[0].reshape(1, d))
    act, w_down_b = _gate_up(x1g, ssq, w_gate_b, w_up_b, w_down)
    out = _down_proj_norm(act, w_down_b, x1, final_norm_g.reshape(1, d))
    return out.reshape(bsz, seq, d)
```

```python
import functools
import math

import numpy as np

import jax
import jax.numpy as jnp
from jax import lax
from jax.experimental import pallas as pl
from jax.experimental.pallas import tpu as pltpu

D_MODEL = 4096
HEAD_DIM = 128
N_DIFF_HEADS = D_MODEL // (4 * HEAD_DIM)
N_FOX_HEADS = D_MODEL // (2 * HEAD_DIM)
DIFF_WIDTH = N_DIFF_HEADS * 2 * HEAD_DIM
FOX_WIDTH = N_FOX_HEADS * HEAD_DIM
QKV_COLS = 3 * DIFF_WIDTH + 3 * FOX_WIDTH
D_FF = ((8 * D_MODEL + 3 * 256 - 1) // (3 * 256)) * 256
NUM_BUCKETS = 32
MAX_DISTANCE = 128
EPS = 1e-6
NEG_INF = -1e30
LAM_INIT = 0.8 - 0.6 * math.exp(-0.3 * 0)
SCALE = HEAD_DIM ** -0.5
LOG2E = math.log2(math.e)

LANES = 128
BF16_SUBLANES = 16
ROW_TILE = 2048
FF_TILE = 256
DOWN_K_TILE = 512
DOWN_RES_TILE = 256
VMEM_LIMIT = 56 * 1024 * 1024
VMEM_LIMIT_HIGH = 60 * 1024 * 1024

ATTN_BLOCK = 512
PIPE_DEPTH = 2
PIPE_BUFS = PIPE_DEPTH + 1
FOX_SUM_ROWS = BF16_SUBLANES
MATMUL_ROW_CHUNK = 512
PROJ_COL_CHUNK = 1024
PROJ_ROW_CHUNK = 128

FAR, NEAR, DIAG = "far", "near", "diag"


def _params(sem, vmem=VMEM_LIMIT):
    return pltpu.CompilerParams(dimension_semantics=sem, vmem_limit_bytes=vmem)


def _row_chunks(n_rows):
    return [slice(s, s + MATMUL_ROW_CHUNK) for s in range(0, n_rows, MATMUL_ROW_CHUNK)]


def _norm_kernel(x_ref, g_ref, wf_ref, h_ref, fl_ref):
    x = x_ref[...]
    y = x * lax.rsqrt(jnp.mean(x * x, axis=-1, keepdims=True) + EPS)
    h = (y * g_ref[...]).astype(jnp.bfloat16)
    h_ref[...] = h
    fl_ref[...] = lax.dot_general(wf_ref[...].astype(jnp.bfloat16), h, (((1,), (1,)), ((), ())),
                                  preferred_element_type=jnp.float32)


def _norm_and_gate_logits(x2d, g, w_in_t, gate_row, nh, bm=512):
    m, d = x2d.shape
    assert gate_row % nh == 0 and gate_row + nh == w_in_t.shape[1]
    return pl.pallas_call(
        _norm_kernel,
        grid=(m // bm,),
        in_specs=[pl.BlockSpec((bm, d), lambda i: (i, 0)),
                  pl.BlockSpec((1, d), lambda i: (0, 0)),
                  pl.BlockSpec((None, nh, d), lambda i: (0, gate_row // nh, 0))],
        out_specs=[pl.BlockSpec((bm, d), lambda i: (i, 0)),
                   pl.BlockSpec((nh, bm), lambda i: (0, i))],
        out_shape=[jax.ShapeDtypeStruct((m, d), jnp.bfloat16),
                   jax.ShapeDtypeStruct((nh, m), jnp.float32)],
        compiler_params=_params(("arbitrary",)),
        name="attn_norm",
    )(x2d, g, w_in_t)


def _matmul_kernel(a_ref, wt_ref, o_ref):
    w_t = wt_ref[...].astype(a_ref.dtype)
    for rows in _row_chunks(a_ref.shape[0]):
        o_ref[rows, :] = lax.dot_general(a_ref[rows, :], w_t, (((1,), (1,)), ((), ())),
                                         preferred_element_type=jnp.float32).astype(o_ref.dtype)


def _matmul(a, w_t, n_out, bm=ROW_TILE, bn=512, name="matmul"):
    m, k = a.shape
    return pl.pallas_call(
        _matmul_kernel,
        grid=(m // bm, n_out // bn),
        in_specs=[pl.BlockSpec((bm, k), lambda i, j: (i, 0)),
                  pl.BlockSpec((None, bn, k), lambda i, j: (0, j, 0))],
        out_specs=pl.BlockSpec((bm, bn), lambda i, j: (i, j)),
        out_shape=jax.ShapeDtypeStruct((m, n_out), jnp.bfloat16),
        compiler_params=_params(("arbitrary", "arbitrary"), vmem=VMEM_LIMIT_HIGH),
        name=name,
    )(a, w_t)


def _cumsum_kernel(fl_ref, bf_ref, c_ref):
    z = fl_ref[...] + bf_ref[...]
    x = jnp.minimum(z, 0.0) - jnp.log(1.0 + jnp.exp(-jnp.abs(z)))
    n = x.shape[-1]
    pos = lax.broadcasted_iota(jnp.int32, x.shape, 1)
    shift = 1
    while shift < n:
        x = x + jnp.where(pos >= shift, pltpu.roll(x, shift, 1), 0.0)
        shift *= 2
    c_ref[...] = x


def _cum_log_forget(fl_t, b_f, seq):
    nh, m = fl_t.shape
    return pl.pallas_call(
        _cumsum_kernel,
        grid=(m // seq,),
        in_specs=[pl.BlockSpec((nh, seq), lambda b: (0, b)),
                  pl.BlockSpec((nh, 1), lambda b: (0, 0))],
        out_specs=pl.BlockSpec((nh, seq), lambda b: (0, b)),
        out_shape=jax.ShapeDtypeStruct((nh, m), jnp.float32),
        compiler_params=_params(("arbitrary",)),
        name="cum_log_forget",
    )(fl_t, b_f)


def _scores_t(k, q):
    return lax.dot_general(k, q, (((1,), (1,)), ((), ())), preferred_element_type=jnp.float32)


def _softmax_pv_update(u, blk_max, shift, v_t, m_ref, l_ref, acc_ref, qs):
    blk = u.shape[1]
    m_old = m_ref[:, pl.ds(qs, blk)]
    m_new = jnp.maximum(m_old, blk_max + shift)
    alpha = jnp.exp2(m_old - m_new)
    p = jnp.exp2(u + (shift - m_new))
    if l_ref is not None:
        l_ref[:, pl.ds(qs, blk)] = (alpha * l_ref[:, pl.ds(qs, blk)]
                                    + jnp.sum(p, axis=0, keepdims=True))
    acc_ref[:, pl.ds(qs, blk)] = alpha * acc_ref[:, pl.ds(qs, blk)] + jnp.dot(
        v_t, p.astype(v_t.dtype), preferred_element_type=jnp.float32)
    m_ref[:, pl.ds(qs, blk)] = m_new


def _pair_schedule(n_blocks, with_near):
    far, near, diag = [], [], []
    for i in range(n_blocks):
        for j in range(i + 1):
            if j == i:
                diag.append((i, j))
            elif with_near and j == i - 1:
                near.append((i, j))
            else:
                far.append((i, j))
    pairs = far + near + diag
    kinds = [FAR] * len(far) + [NEAR] * len(near) + [DIAG] * len(diag)
    return kinds, np.asarray(pairs, np.int32).T.copy()


def _run_pipeline(kinds, stage_a, stage_b, unroll_all=False):
    n = len(kinds)
    ahead_kind = list(kinds[PIPE_DEPTH:]) + [None] * PIPE_DEPTH
    for t in range(min(PIPE_DEPTH, n)):
        stage_a(t, kinds[t], t % PIPE_BUFS)
    s = 0
    while s < n:
        e = s
        while e < n and ahead_kind[e] == ahead_kind[s]:
            e += 1
        kind, n_loop = ahead_kind[s], 0 if unroll_all else (e - s) // PIPE_BUFS

        def one(step, buf):
            if kind is not None:
                stage_a(step + PIPE_DEPTH, kind, (buf + PIPE_DEPTH) % PIPE_BUFS)
            stage_b(step, buf)

        if n_loop:
            def body(r, carry):
                base = s + PIPE_BUFS * r
                for i in range(PIPE_BUFS):
                    one(base + i, (s + i) % PIPE_BUFS)
                return carry

            lax.fori_loop(0, n_loop, body, 0)
        for step in range(s + n_loop * PIPE_BUFS, e):
            one(step, step % PIPE_BUFS)
        s = e


def _store_transposed(src_ref, dst_ref):
    for c in range(0, src_ref.shape[0], ATTN_BLOCK):
        dst_ref[:, c:c + ATTN_BLOCK] = src_ref[c:c + ATTN_BLOCK, :].T.astype(dst_ref.dtype)


def _t5_bucket(n):
    max_exact = NUM_BUCKETS // 2
    nf = jnp.maximum(n, 1).astype(jnp.float32)
    large = max_exact + (jnp.log(nf / max_exact) / math.log(MAX_DISTANCE / max_exact)
                         * (NUM_BUCKETS - max_exact)).astype(jnp.int32)
    large = jnp.minimum(large, NUM_BUCKETS - 1)
    return jnp.where(n < max_exact, n, large)


def _diff_attn_kernel(kinds, pair_ref, tab_ref, lq1_ref, lk1_ref, lq2_ref, lk2_ref,
                      q_ref, k_ref, v_ref, g_ref, w_ref, o_ref, w_out_ref,
                      bias_near_ref, bias_diag_ref, vt_ref, *tile_and_state_refs):
    h = pl.program_id(0)
    b = pl.program_id(1)
    blk = ATTN_BLOCK
    w_out_ref[...] = w_ref[...].astype(w_out_ref.dtype)
    key = lax.broadcasted_iota(jnp.int32, (blk, blk), 0)
    qry = lax.broadcasted_iota(jnp.int32, (blk, blk), 1)
    u_refs = tile_and_state_refs[:PIPE_BUFS]
    bm_refs = tile_and_state_refs[PIPE_BUFS:2 * PIPE_BUFS]
    m_ref, l_ref, acc_ref = tile_and_state_refs[2 * PIPE_BUFS:]
    far_bias = tab_ref[NUM_BUCKETS - 1, h]

    @pl.when(b == 0)
    def _build_bias():
        for ref, offset in ((bias_diag_ref, 0), (bias_near_ref, blk)):
            bucket = _t5_bucket(jnp.maximum(qry - key + offset, 0))
            bias = jnp.zeros((blk, blk), jnp.float32)
            for t in range(NUM_BUCKETS):
                bias = jnp.where(bucket == t, tab_ref[t, h], bias)
            ref[...] = (bias - far_bias) * LOG2E

    _store_transposed(v_ref, vt_ref)
    m_ref[...] = jnp.full(m_ref.shape, NEG_INF, jnp.float32)
    l_ref[...] = jnp.zeros(l_ref.shape, jnp.float32)
    acc_ref[...] = jnp.zeros(acc_ref.shape, jnp.float32)

    def starts(t):
        return (pl.multiple_of(pair_ref[0, t] * blk, blk), pl.multiple_of(pair_ref[1, t] * blk, blk))

    def stage_a(t, kind, par):
        qs, ks = starts(t)
        for mp in range(2):
            cols = slice(mp * HEAD_DIM, (mp + 1) * HEAD_DIM)
            u = _scores_t(k_ref[pl.ds(ks, blk), cols], q_ref[pl.ds(qs, blk), cols]) * (SCALE * LOG2E)
            if kind == NEAR:
                u = u + bias_near_ref[...]
            elif kind == DIAG:
                u = jnp.where(qry >= key, u + bias_diag_ref[...], NEG_INF)
            u_refs[par][mp] = u
            bm_refs[par][mp] = jnp.max(u, axis=0, keepdims=True)

    def stage_b(t, par):
        qs, ks = starts(t)
        v_t = vt_ref[:, pl.ds(ks, blk)]
        for mp in range(2):
            _softmax_pv_update(u_refs[par][mp], bm_refs[par][mp], far_bias * LOG2E, v_t,
                               m_ref.at[mp], l_ref.at[mp], acc_ref.at[mp], qs)

    _run_pipeline(kinds, stage_a, stage_b)

    lam = (jnp.exp(jnp.sum(lq1_ref[...] * lk1_ref[...], axis=-1, keepdims=True))
           - jnp.exp(jnp.sum(lq2_ref[...] * lk2_ref[...], axis=-1, keepdims=True))
           + LAM_INIT)
    for c in range(0, o_ref.shape[0], blk):
        cols = slice(c, c + blk)
        o_t = (acc_ref[0, :, cols] * (1.0 / l_ref[0, :, cols])
               - lam * (acc_ref[1, :, cols] * (1.0 / l_ref[1, :, cols])))
        y_t = o_t * lax.rsqrt(jnp.mean(o_t * o_t, axis=0, keepdims=True) + EPS)
        y_t = (y_t * g_ref[...]) * (1.0 - LAM_INIT)
        o_ref[cols, :] = y_t.astype(o_ref.dtype).T


def _slab_spec(weight, n_steps, step_of):
    rows, cols = weight.shape[-2:]
    slab = rows // n_steps
    assert slab * n_steps == rows and slab % BF16_SUBLANES == 0
    if weight.ndim == 3:
        return pl.BlockSpec((None, slab, cols), lambda *idx: (0, step_of(*idx), 0))
    return pl.BlockSpec((slab, cols), lambda *idx: (step_of(*idx), 0))


def _bf16_like(weight):
    return jax.ShapeDtypeStruct(weight.shape[-2:], jnp.bfloat16)


def _diff_attention(qkv, tab, lq1, lk1, lq2, lk2, g_col, w_conv):
    bsz, seq, _ = qkv.shape
    blk = ATTN_BLOCK
    w = 2 * HEAD_DIM
    k_off = DIFF_WIDTH // w
    v_off = 2 * DIFF_WIDTH // w
    kinds, pairs = _pair_schedule(seq // blk, with_near=True)
    smem = pl.BlockSpec(memory_space=pltpu.SMEM)
    vec = pl.BlockSpec((1, HEAD_DIM), lambda h, b: (0, 0))
    n_steps = N_DIFF_HEADS * bsz

    def step_of(h, b):
        return h * bsz + b

    return pl.pallas_call(
        functools.partial(_diff_attn_kernel, kinds),
        grid=(N_DIFF_HEADS, bsz),
        in_specs=[smem, smem, vec, vec, vec, vec,
                  pl.BlockSpec((None, seq, w), lambda h, b: (b, 0, h)),
                  pl.BlockSpec((None, seq, w), lambda h, b: (b, 0, k_off + h)),
                  pl.BlockSpec((None, seq, w), lambda h, b: (b, 0, v_off + h)),
                  pl.BlockSpec((w, 1), lambda h, b: (0, 0)),
                  _slab_spec(w_conv, n_steps, step_of)],
        out_specs=[pl.BlockSpec((None, seq, w), lambda h, b: (b, 0, h)),
                   _slab_spec(_bf16_like(w_conv), n_steps, step_of)],
        out_shape=[jax.ShapeDtypeStruct((bsz, seq, DIFF_WIDTH), jnp.bfloat16), _bf16_like(w_conv)],
        scratch_shapes=([pltpu.VMEM((blk, blk), jnp.float32),
                         pltpu.VMEM((blk, blk), jnp.float32),
                         pltpu.VMEM((w, seq), jnp.bfloat16)]
                        + [pltpu.VMEM((2, blk, blk), jnp.float32)] * PIPE_BUFS
                        + [pltpu.VMEM((2, 1, blk), jnp.float32)] * PIPE_BUFS
                        + [pltpu.VMEM((2, 1, seq), jnp.float32),
                           pltpu.VMEM((2, 1, seq), jnp.float32),
                           pltpu.VMEM((2, w, seq), jnp.float32)]),
        compiler_params=_params(("arbitrary", "arbitrary")),
        name="diff_attention",
    )(jnp.asarray(pairs), tab, lq1, lk1, lq2, lk2, qkv, qkv, qkv, g_col, w_conv)


def _fox_attn_kernel(kinds, pairs, q_ref, k_ref, v_ref, cum_ref, wa_ref, wb_ref,
                     o_ref, wa_out_ref, wb_out_ref,
                     vt_ref, q2t_ref, k2_ref, *tile_and_state_refs):
    blk = ATTN_BLOCK
    d = HEAD_DIM
    f32, bf16 = jnp.float32, jnp.bfloat16
    key = lax.broadcasted_iota(jnp.int32, (blk, blk), 0)
    qry = lax.broadcasted_iota(jnp.int32, (blk, blk), 1)
    lane = lax.broadcasted_iota(jnp.int32, (blk, LANES), 1)
    u_refs = tile_and_state_refs[:PIPE_BUFS]
    bm_refs = tile_and_state_refs[PIPE_BUFS:2 * PIPE_BUFS]
    m_ref, acc_ref = tile_and_state_refs[2 * PIPE_BUFS:]

    wa_out_ref[...] = wa_ref[...].astype(wa_out_ref.dtype)
    wb_out_ref[...] = wb_ref[...].astype(wb_out_ref.dtype)

    _store_transposed(v_ref, vt_ref.at[:d, :])
    vt_ref[d:, :] = jnp.ones((vt_ref.shape[0] - d, vt_ref.shape[1]), vt_ref.dtype)
    for c in range(0, q_ref.shape[0], blk):
        rows = slice(c, c + blk)
        x = jnp.broadcast_to(cum_ref[:, rows] * (-1.0 / SCALE), (LANES, blk)).T
        hi = x.astype(bf16).astype(f32)
        mid = (x - hi).astype(bf16).astype(f32)
        lo = ((x - hi) - mid).astype(bf16).astype(f32)
        aug = jnp.where(lane == 0, hi, jnp.where(lane == 1, mid, jnp.where(lane == 2, lo, 0.0)))
        k2_ref[rows, :d] = k_ref[rows, :]
        k2_ref[rows, d:] = aug.astype(bf16)
    _store_transposed(q_ref, q2t_ref.at[:d, :])
    ones_row = lax.broadcasted_iota(jnp.int32, (d, q2t_ref.shape[1]), 0) < 3
    q2t_ref[d:, :] = jnp.where(ones_row, 1.0, 0.0).astype(bf16)
    m_ref[...] = jnp.full(m_ref.shape, NEG_INF, jnp.float32)
    acc_ref[...] = jnp.zeros(acc_ref.shape, jnp.float32)

    def starts(t):
        return int(pairs[0][t]) * blk, int(pairs[1][t]) * blk

    def stage_a(t, kind, buf):
        qs, ks = starts(t)
        u = jnp.dot(k2_ref[pl.ds(ks, blk), :], q2t_ref[:, pl.ds(qs, blk)],
                    preferred_element_type=jnp.float32) * (SCALE * LOG2E)
        if kind == DIAG:
            u = jnp.where(qry >= key, u, NEG_INF)
        u_refs[buf][...] = u
        bm_refs[buf][...] = jnp.max(u, axis=0, keepdims=True)

    def stage_b(t, buf):
        qs, ks = starts(t)
        cq = cum_ref[:, pl.ds(qs, blk)] * LOG2E
        _softmax_pv_update(u_refs[buf][...], bm_refs[buf][...], cq, vt_ref[:, pl.ds(ks, blk)],
                           m_ref, None, acc_ref, qs)

    _run_pipeline(kinds, stage_a, stage_b, unroll_all=True)

    for c in range(0, o_ref.shape[0], blk):
        cols = slice(c, c + blk)
        o_t = acc_ref[:d, cols] * (1.0 / acc_ref[d:d + 1, cols])
        o_ref[cols, :] = o_t.astype(o_ref.dtype).T


def _fox_attention(qkv, cum, w_conv_a, w_conv_b):
    bsz, seq, _ = qkv.shape
    blk = ATTN_BLOCK
    q_off = 3 * DIFF_WIDTH // HEAD_DIM
    k_off = q_off + N_FOX_HEADS
    v_off = k_off + N_FOX_HEADS
    kinds, pairs = _pair_schedule(seq // blk, with_near=False)
    n_steps = N_FOX_HEADS * bsz

    def step_of(h, b):
        return h * bsz + b

    return pl.pallas_call(
        functools.partial(_fox_attn_kernel, kinds, pairs),
        grid=(N_FOX_HEADS, bsz),
        in_specs=[pl.BlockSpec((None, seq, HEAD_DIM), lambda h, b: (b, 0, q_off + h)),
                  pl.BlockSpec((None, seq, HEAD_DIM), lambda h, b: (b, 0, k_off + h)),
                  pl.BlockSpec((None, seq, HEAD_DIM), lambda h, b: (b, 0, v_off + h)),
                  pl.BlockSpec((None, None, 1, seq), lambda h, b: (h, b, 0, 0)),
                  _slab_spec(w_conv_a, n_steps, step_of),
                  _slab_spec(w_conv_b, n_steps, step_of)],
        out_specs=[pl.BlockSpec((None, seq, HEAD_DIM), lambda h, b: (b, 0, h)),
                   _slab_spec(_bf16_like(w_conv_a), n_steps, step_of),
                   _slab_spec(_bf16_like(w_conv_b), n_steps, step_of)],
        out_shape=[jax.ShapeDtypeStruct((bsz, seq, FOX_WIDTH), jnp.bfloat16),
                   _bf16_like(w_conv_a), _bf16_like(w_conv_b)],
        scratch_shapes=([pltpu.VMEM((HEAD_DIM + FOX_SUM_ROWS, seq), jnp.bfloat16),
                         pltpu.VMEM((2 * HEAD_DIM, seq), jnp.bfloat16),
                         pltpu.VMEM((seq, 2 * HEAD_DIM), jnp.bfloat16)]
                        + [pltpu.VMEM((blk, blk), jnp.float32)] * PIPE_BUFS
                        + [pltpu.VMEM((1, blk), jnp.float32)] * PIPE_BUFS
                        + [pltpu.VMEM((1, seq), jnp.float32),
                           pltpu.VMEM((HEAD_DIM + FOX_SUM_ROWS, seq), jnp.float32)]),
        compiler_params=_params(("arbitrary", "arbitrary")),
        name="fox_attention",
    )(qkv, qkv, qkv, cum, w_conv_a, w_conv_b)


def _out_proj_kernel(ad_ref, af_ref, w_ref, res_ref, g_ref, x1_ref, xg_ref, ssq_ref):
    j = pl.program_id(1)
    kd = ad_ref.shape[1]
    wd = w_ref[:kd, :]
    wf = w_ref[kd:, :]
    parts = []
    for rows in _row_chunks(ad_ref.shape[0]):
        y = (res_ref[rows, :] + jnp.dot(ad_ref[rows, :], wd, preferred_element_type=jnp.float32)
             + jnp.dot(af_ref[rows, :], wf, preferred_element_type=jnp.float32))
        x1_ref[rows, :] = y
        xg_ref[rows, :] = (y * g_ref[...]).astype(xg_ref.dtype)
        sq = y * y
        part = sq[:, :LANES]
        for lc in range(LANES, sq.shape[1], LANES):
            part = part + sq[:, lc:lc + LANES]
        parts.append(part)
    part = jnp.concatenate(parts, axis=0)

    @pl.when(j == 0)
    def _first():
        ssq_ref[...] = part

    @pl.when(j > 0)
    def _rest():
        ssq_ref[...] += part


def _out_proj(a_diff, a_fox, w, res, g, bm=1024, bn=512):
    m, kd = a_diff.shape
    kf = a_fox.shape[1]
    d = w.shape[-1]
    return pl.pallas_call(
        _out_proj_kernel,
        grid=(m // bm, d // bn),
        in_specs=[pl.BlockSpec((bm, kd), lambda i, j: (i, 0)),
                  pl.BlockSpec((bm, kf), lambda i, j: (i, 0)),
                  pl.BlockSpec((kd + kf, bn), lambda i, j: (0, j)),
                  pl.BlockSpec((bm, bn), lambda i, j: (i, j)),
                  pl.BlockSpec((1, bn), lambda i, j: (0, j))],
        out_specs=[pl.BlockSpec((bm, bn), lambda i, j: (i, j)),
                   pl.BlockSpec((bm, bn), lambda i, j: (i, j)),
                   pl.BlockSpec((bm, LANES), lambda i, j: (i, 0))],
        out_shape=[jax.ShapeDtypeStruct((m, d), jnp.float32),
                   jax.ShapeDtypeStruct((m, d), jnp.bfloat16),
                   jax.ShapeDtypeStruct((m, LANES), jnp.float32)],
        compiler_params=_params(("arbitrary", "arbitrary")),
        name="out_proj",
    )(a_diff, a_fox, w, res, g)


def _down_kernel(a_ref, w_ref, res_ref, g_ref, o_ref, *, n_k, k_last, n_res):
    k = pl.program_id(1)
    bk = a_ref.shape[1]
    rc = res_ref.shape[1]

    def accumulate(k_len, first):
        a = a_ref[:, :k_len]
        for c in range(0, o_ref.shape[1], PROJ_COL_CHUNK):
            cols = slice(c, c + PROJ_COL_CHUNK)
            prod = jnp.dot(a, w_ref[:k_len, cols], preferred_element_type=jnp.float32)
            o_ref[:, cols] = prod if first else o_ref[:, cols] + prod

    pl.when(k == 0)(lambda: accumulate(bk, True))
    pl.when((k > 0) & (k < n_k - 1))(lambda: accumulate(bk, False))

    @pl.when(k < n_res)
    def _add_residual_tile():
        cols = pl.ds(pl.multiple_of(k * rc, rc), rc)
        o_ref[:, cols] += res_ref[...]

    @pl.when(k == n_k - 1)
    def _finish():
        accumulate(k_last, False)
        for r in range(0, o_ref.shape[0], PROJ_ROW_CHUNK):
            rows = slice(r, r + PROJ_ROW_CHUNK)
            x = o_ref[rows, :]
            o_ref[rows, :] = x * lax.rsqrt(jnp.mean(x * x, axis=-1, keepdims=True) + EPS) * g_ref[...]


def _down_proj_norm(a, w, res, g, bm=1024, bk=DOWN_K_TILE, rc=DOWN_RES_TILE):
    m, kdim = a.shape
    d = w.shape[1]
    n_k = pl.cdiv(kdim, bk)
    k_last = kdim - (n_k - 1) * bk
    n_res = d // rc
    assert n_res * rc == d and n_res <= n_k - 1
    row_block = pl.BlockSpec((bm, d), lambda i, k: (i, 0))
    return pl.pallas_call(
        functools.partial(_down_kernel, n_k=n_k, k_last=k_last, n_res=n_res),
        grid=(m // bm, n_k),
        in_specs=[pl.BlockSpec((bm, bk), lambda i, k: (i, k)),
                  pl.BlockSpec((bk, d), lambda i, k: (k, 0)),
                  pl.BlockSpec((bm, rc), lambda i, k: (i, jnp.minimum(k, n_res - 1))),
                  pl.BlockSpec((1, d), lambda i, k: (0, 0))],
        out_specs=row_block,
        out_shape=jax.ShapeDtypeStruct((m, d), jnp.float32),
        compiler_params=_params(("arbitrary", "arbitrary")),
        name="ffn_down",
    )(a, w, res, g)


def _gate_up_kernel(xg_ref, ssq_ref, wg_ref, wu_ref, wd_ref, o_ref, wd_out_ref, r_ref):
    @pl.when(pl.program_id(1) == 0)
    def _row_scale():
        mean_sq = jnp.sum(ssq_ref[...], axis=-1, keepdims=True) * (1.0 / xg_ref.shape[1])
        r_ref[...] = jnp.broadcast_to(lax.rsqrt(mean_sq + EPS), r_ref.shape)

    wg = wg_ref[...]
    wu = wu_ref[...]
    for rows in _row_chunks(xg_ref.shape[0]):
        xg = xg_ref[rows, :]
        r = jnp.concatenate([r_ref[rows, :]] * (o_ref.shape[1] // LANES), axis=1)
        gate = jnp.dot(xg, wg, preferred_element_type=jnp.float32) * r
        up = jnp.dot(xg, wu, preferred_element_type=jnp.float32) * r
        o_ref[rows, :] = (gate * jax.nn.sigmoid(gate) * up).astype(o_ref.dtype)
    wd_out_ref[...] = wd_ref[...].astype(wd_out_ref.dtype)


def _gate_up(xg, ssq, wg, wu, wd, bm=ROW_TILE, bn=FF_TILE):
    m, k = xg.shape
    f_rows, d = wd.shape[-2:]
    n = wg.shape[1]
    n_j = n // bn
    n_steps = (m // bm) * n_j

    def step_of(i, j):
        return i * n_j + j

    return pl.pallas_call(
        _gate_up_kernel,
        grid=(m // bm, n_j),
        in_specs=[pl.BlockSpec((bm, k), lambda i, j: (i, 0)),
                  pl.BlockSpec((bm, LANES), lambda i, j: (i, 0)),
                  pl.BlockSpec((k, bn), lambda i, j: (0, j)),
                  pl.BlockSpec((k, bn), lambda i, j: (0, j)),
                  _slab_spec(wd, n_steps, step_of)],
        out_specs=[pl.BlockSpec((bm, bn), lambda i, j: (i, j)),
                   _slab_spec(_bf16_like(wd), n_steps, step_of)],
        out_shape=[jax.ShapeDtypeStruct((m, n), jnp.bfloat16), _bf16_like(wd)],
        scratch_shapes=[pltpu.VMEM((bm, LANES), jnp.float32)],
        compiler_params=_params(("arbitrary", "arbitrary")),
        name="ffn_gate_up",
    )(xg, ssq, wg, wu, wd)


def kernel(x, attn_norm_g, w_in, b_f, lambda_q1, lambda_k1, lambda_q2, lambda_k2, rel_bias_table,
           diff_subln_g, w_o, ffn_norm_g, w_gate, w_up, w_down, final_norm_g):
    bsz, seq, d = x.shape
    m = bsz * seq
    x2d = x.reshape(m, d)

    w_in_t = jnp.swapaxes(w_in, 1, 2)
    h, fl_t = _norm_and_gate_logits(x2d, attn_norm_g[0].reshape(1, d), w_in_t, QKV_COLS, N_FOX_HEADS)
    qkv = _matmul(h, w_in_t, QKV_COLS, name="in_proj").reshape(bsz, seq, QKV_COLS)

    cum = _cum_log_forget(fl_t, b_f[0].reshape(N_FOX_HEADS, 1), seq)
    cum = cum.reshape(N_FOX_HEADS, bsz, 1, seq)

    o_diff, w_o_b = _diff_attention(
        qkv, rel_bias_table,
        lambda_q1[0].reshape(1, HEAD_DIM), lambda_k1[0].reshape(1, HEAD_DIM),
        lambda_q2[0].reshape(1, HEAD_DIM), lambda_k2[0].reshape(1, HEAD_DIM),
        diff_subln_g[0].reshape(2 * HEAD_DIM, 1), w_o)
    o_fox, w_gate_b, w_up_b = _fox_attention(qkv, cum, w_gate, w_up)

    x1, x1g, ssq = _out_proj(o_diff.reshape(m, DIFF_WIDTH), o_fox.reshape(m, FOX_WIDTH), w_o_b, x2d,
                             ffn_norm_g[0].reshape(1, d))
    act, w_down_b = _gate_up(x1g, ssq, w_gate_b, w_up_b, w_down)
    out = _down_proj_norm(act, w_down_b, x1, final_norm_g.reshape(1, d))
    return out.reshape(bsz, seq, d)
```

```python
import functools
import math

import numpy as np

import jax
import jax.numpy as jnp
from jax import lax
from jax.experimental import pallas as pl
from jax.experimental.pallas import tpu as pltpu

D_MODEL = 4096
HEAD_DIM = 128
N_DIFF_HEADS = D_MODEL // (4 * HEAD_DIM)
N_FOX_HEADS = D_MODEL // (2 * HEAD_DIM)
DIFF_WIDTH = N_DIFF_HEADS * 2 * HEAD_DIM
FOX_WIDTH = N_FOX_HEADS * HEAD_DIM
QKV_COLS = 3 * DIFF_WIDTH + 3 * FOX_WIDTH
D_FF = ((8 * D_MODEL + 3 * 256 - 1) // (3 * 256)) * 256
NUM_BUCKETS = 32
MAX_DISTANCE = 128
EPS = 1e-6
NEG_INF = -1e30
LAM_INIT = 0.8 - 0.6 * math.exp(-0.3 * 0)
SCALE = HEAD_DIM ** -0.5
LOG2E = math.log2(math.e)

LANES = 128
BF16_SUBLANES = 16
ROW_TILE = 2048
FF_TILE = 256
DOWN_K_TILE = 512
DOWN_RES_TILE = 256
VMEM_LIMIT = 56 * 1024 * 1024
VMEM_LIMIT_HIGH = 60 * 1024 * 1024

ATTN_BLOCK = 512
PIPE_DEPTH = 2
PIPE_BUFS = PIPE_DEPTH + 1
FOX_SUM_ROWS = BF16_SUBLANES
MATMUL_ROW_CHUNK = 512
PROJ_COL_CHUNK = 1024
PROJ_ROW_CHUNK = 128

FAR, NEAR, DIAG = "far", "near", "diag"


def _params(sem, vmem=VMEM_LIMIT):
    return pltpu.CompilerParams(dimension_semantics=sem, vmem_limit_bytes=vmem)


def _row_chunks(n_rows):
    return [slice(s, s + MATMUL_ROW_CHUNK) for s in range(0, n_rows, MATMUL_ROW_CHUNK)]


def _norm_kernel(x_ref, g_ref, wf_ref, h_ref, fl_ref):
    x = x_ref[...]
    y = x * lax.rsqrt(jnp.mean(x * x, axis=-1, keepdims=True) + EPS)
    h = (y * g_ref[...]).astype(jnp.bfloat16)
    h_ref[...] = h
    fl_ref[...] = lax.dot_general(wf_ref[...].astype(jnp.bfloat16), h, (((1,), (1,)), ((), ())),
                                  preferred_element_type=jnp.float32)


def _norm_and_gate_logits(x2d, g, w_in_t, gate_row, nh, bm=512):
    m, d = x2d.shape
    assert gate_row % nh == 0 and gate_row + nh == w_in_t.shape[1]
    return pl.pallas_call(
        _norm_kernel,
        grid=(m // bm,),
        in_specs=[pl.BlockSpec((bm, d), lambda i: (i, 0)),
                  pl.BlockSpec((1, d), lambda i: (0, 0)),
                  pl.BlockSpec((None, nh, d), lambda i: (0, gate_row // nh, 0))],
        out_specs=[pl.BlockSpec((bm, d), lambda i: (i, 0)),
                   pl.BlockSpec((nh, bm), lambda i: (0, i))],
        out_shape=[jax.ShapeDtypeStruct((m, d), jnp.bfloat16),
                   jax.ShapeDtypeStruct((nh, m), jnp.float32)],
        compiler_params=_params(("arbitrary",)),
        name="attn_norm",
    )(x2d, g, w_in_t)


def _matmul_kernel(a_ref, wt_ref, o_ref):
    w_t = wt_ref[...].astype(a_ref.dtype)
    for rows in _row_chunks(a_ref.shape[0]):
        o_ref[rows, :] = lax.dot_general(a_ref[rows, :], w_t, (((1,), (1,)), ((), ())),
                                         preferred_element_type=jnp.float32).astype(o_ref.dtype)


def _matmul(a, w_t, n_out, bm=ROW_TILE, bn=512, name="matmul"):
    m, k = a.shape
    return pl.pallas_call(
        _matmul_kernel,
        grid=(m // bm, n_out // bn),
        in_specs=[pl.BlockSpec((bm, k), lambda i, j: (i, 0)),
                  pl.BlockSpec((None, bn, k), lambda i, j: (0, j, 0))],
        out_specs=pl.BlockSpec((bm, bn), lambda i, j: (i, j)),
        out_shape=jax.ShapeDtypeStruct((m, n_out), jnp.bfloat16),
        compiler_params=_params(("arbitrary", "arbitrary"), vmem=VMEM_LIMIT_HIGH),
        name=name,
    )(a, w_t)


def _cumsum_kernel(fl_ref, bf_ref, c_ref):
    z = fl_ref[...] + bf_ref[...]
    x = jnp.minimum(z, 0.0) - jnp.log(1.0 + jnp.exp(-jnp.abs(z)))
    n = x.shape[-1]
    pos = lax.broadcasted_iota(jnp.int32, x.shape, 1)
    shift = 1
    while shift < n:
        x = x + jnp.where(pos >= shift, pltpu.roll(x, shift, 1), 0.0)
        shift *= 2
    c_ref[...] = x


def _cum_log_forget(fl_t, b_f, seq):
    nh, m = fl_t.shape
    return pl.pallas_call(
        _cumsum_kernel,
        grid=(m // seq,),
        in_specs=[pl.BlockSpec((nh, seq), lambda b: (0, b)),
                  pl.BlockSpec((nh, 1), lambda b: (0, 0))],
        out_specs=pl.BlockSpec((nh, seq), lambda b: (0, b)),
        out_shape=jax.ShapeDtypeStruct((nh, m), jnp.float32),
        compiler_params=_params(("arbitrary",)),
        name="cum_log_forget",
    )(fl_t, b_f)


def _scores_t(k, q):
    return lax.dot_general(k, q, (((1,), (1,)), ((), ())), preferred_element_type=jnp.float32)


def _softmax_pv_update(u, blk_max, shift, v_t, m_ref, l_ref, acc_ref, qs):
    blk = u.shape[1]
    m_old = m_ref[:, pl.ds(qs, blk)]
    m_new = jnp.maximum(m_old, blk_max + shift)
    alpha = jnp.exp2(m_old - m_new)
    p = jnp.exp2(u + (shift - m_new))
    if l_ref is not None:
        l_ref[:, pl.ds(qs, blk)] = (alpha * l_ref[:, pl.ds(qs, blk)]
                                    + jnp.sum(p, axis=0, keepdims=True))
    acc_ref[:, pl.ds(qs, blk)] = alpha * acc_ref[:, pl.ds(qs, blk)] + jnp.dot(
        v_t, p.astype(v_t.dtype), preferred_element_type=jnp.float32)
    m_ref[:, pl.ds(qs, blk)] = m_new


def _pair_schedule(n_blocks, with_near):
    far, near, diag = [], [], []
    for i in range(n_blocks):
        for j in range(i + 1):
            if j == i:
                diag.append((i, j))
            elif with_near and j == i - 1:
                near.append((i, j))
            else:
                far.append((i, j))
    pairs = far + near + diag
    kinds = [FAR] * len(far) + [NEAR] * len(near) + [DIAG] * len(diag)
    return kinds, np.asarray(pairs, np.int32).T.copy()


def _run_pipeline(kinds, stage_a, stage_b, unroll_all=False):
    n = len(kinds)
    ahead_kind = list(kinds[PIPE_DEPTH:]) + [None] * PIPE_DEPTH
    for t in range(min(PIPE_DEPTH, n)):
        stage_a(t, kinds[t], t % PIPE_BUFS)
    s = 0
    while s < n:
        e = s
        while e < n and ahead_kind[e] == ahead_kind[s]:
            e += 1
        kind, n_loop = ahead_kind[s], 0 if unroll_all else (e - s) // PIPE_BUFS

        def one(step, buf):
            if kind is not None:
                stage_a(step + PIPE_DEPTH, kind, (buf + PIPE_DEPTH) % PIPE_BUFS)
            stage_b(step, buf)

        if n_loop:
            def body(r, carry):
                base = s + PIPE_BUFS * r
                for i in range(PIPE_BUFS):
                    one(base + i, (s + i) % PIPE_BUFS)
                return carry

            lax.fori_loop(0, n_loop, body, 0)
        for step in range(s + n_loop * PIPE_BUFS, e):
            one(step, step % PIPE_BUFS)
        s = e


def _store_transposed(src_ref, dst_ref):
    for c in range(0, src_ref.shape[0], ATTN_BLOCK):
        dst_ref[:, c:c + ATTN_BLOCK] = src_ref[c:c + ATTN_BLOCK, :].T.astype(dst_ref.dtype)


def _t5_bucket(n):
    max_exact = NUM_BUCKETS // 2
    nf = jnp.maximum(n, 1).astype(jnp.float32)
    large = max_exact + (jnp.log(nf / max_exact) / math.log(MAX_DISTANCE / max_exact)
                         * (NUM_BUCKETS - max_exact)).astype(jnp.int32)
    large = jnp.minimum(large, NUM_BUCKETS - 1)
    return jnp.where(n < max_exact, n, large)


def _diff_attn_kernel(kinds, pair_ref, tab_ref, lq1_ref, lk1_ref, lq2_ref, lk2_ref,
                      q_ref, k_ref, v_ref, g_ref, w_ref, o_ref, w_out_ref,
                      bias_near_ref, bias_diag_ref, vt_ref, *tile_and_state_refs):
    h = pl.program_id(0)
    b = pl.program_id(1)
    blk = ATTN_BLOCK
    w_out_ref[...] = w_ref[...].astype(w_out_ref.dtype)
    key = lax.broadcasted_iota(jnp.int32, (blk, blk), 0)
    qry = lax.broadcasted_iota(jnp.int32, (blk, blk), 1)
    u_refs = tile_and_state_refs[:PIPE_BUFS]
    bm_refs = tile_and_state_refs[PIPE_BUFS:2 * PIPE_BUFS]
    m_ref, l_ref, acc_ref = tile_and_state_refs[2 * PIPE_BUFS:]
    far_bias = tab_ref[NUM_BUCKETS - 1, h]

    @pl.when(b == 0)
    def _build_bias():
        for ref, offset in ((bias_diag_ref, 0), (bias_near_ref, blk)):
            bucket = _t5_bucket(jnp.maximum(qry - key + offset, 0))
            bias = jnp.zeros((blk, blk), jnp.float32)
            for t in range(NUM_BUCKETS):
                bias = jnp.where(bucket == t, tab_ref[t, h], bias)
            ref[...] = (bias - far_bias) * LOG2E

    _store_transposed(v_ref, vt_ref)
    m_ref[...] = jnp.full(m_ref.shape, NEG_INF, jnp.float32)
    l_ref[...] = jnp.zeros(l_ref.shape, jnp.float32)
    acc_ref[...] = jnp.zeros(acc_ref.shape, jnp.float32)

    def starts(t):
        return (pl.multiple_of(pair_ref[0, t] * blk, blk), pl.multiple_of(pair_ref[1, t] * blk, blk))

    def stage_a(t, kind, par):
        qs, ks = starts(t)
        for mp in range(2):
            cols = slice(mp * HEAD_DIM, (mp + 1) * HEAD_DIM)
            u = _scores_t(k_ref[pl.ds(ks, blk), cols], q_ref[pl.ds(qs, blk), cols]) * (SCALE * LOG2E)
            if kind == NEAR:
                u = u + bias_near_ref[...]
            elif kind == DIAG:
                u = jnp.where(qry >= key, u + bias_diag_ref[...], NEG_INF)
            u_refs[par][mp] = u
            bm_refs[par][mp] = jnp.max(u, axis=0, keepdims=True)

    def stage_b(t, par):
        qs, ks = starts(t)
        v_t = vt_ref[:, pl.ds(ks, blk)]
        for mp in range(2):
            _softmax_pv_update(u_refs[par][mp], bm_refs[par][mp], far_bias * LOG2E, v_t,
                               m_ref.at[mp], l_ref.at[mp], acc_ref.at[mp], qs)

    _run_pipeline(kinds, stage_a, stage_b)

    lam = (jnp.exp(jnp.sum(lq1_ref[...] * lk1_ref[...], axis=-1, keepdims=True))
           - jnp.exp(jnp.sum(lq2_ref[...] * lk2_ref[...], axis=-1, keepdims=True))
           + LAM_INIT)
    for c in range(0, o_ref.shape[0], blk):
        cols = slice(c, c + blk)
        o_t = (acc_ref[0, :, cols] * (1.0 / l_ref[0, :, cols])
               - lam * (acc_ref[1, :, cols] * (1.0 / l_ref[1, :, cols])))
        y_t = o_t * lax.rsqrt(jnp.mean(o_t * o_t, axis=0, keepdims=True) + EPS)
        y_t = (y_t * g_ref[...]) * (1.0 - LAM_INIT)
        o_ref[cols, :] = y_t.astype(o_ref.dtype).T


def _slab_spec(weight, n_steps, step_of):
    rows, cols = weight.shape[-2:]
    slab = rows // n_steps
    assert slab * n_steps == rows and slab % BF16_SUBLANES == 0
    if weight.ndim == 3:
        return pl.BlockSpec((None, slab, cols), lambda *idx: (0, step_of(*idx), 0))
    return pl.BlockSpec((slab, cols), lambda *idx: (step_of(*idx), 0))


def _bf16_like(weight):
    return jax.ShapeDtypeStruct(weight.shape[-2:], jnp.bfloat16)


def _diff_attention(qkv, tab, lq1, lk1, lq2, lk2, g_col, w_conv):
    bsz, seq, _ = qkv.shape
    blk = ATTN_BLOCK
    w = 2 * HEAD_DIM
    k_off = DIFF_WIDTH // w
    v_off = 2 * DIFF_WIDTH // w
    kinds, pairs = _pair_schedule(seq // blk, with_near=True)
    smem = pl.BlockSpec(memory_space=pltpu.SMEM)
    vec = pl.BlockSpec((1, HEAD_DIM), lambda h, b: (0, 0))
    n_steps = N_DIFF_HEADS * bsz

    def step_of(h, b):
        return h * bsz + b

    return pl.pallas_call(
        functools.partial(_diff_attn_kernel, kinds),
        grid=(N_DIFF_HEADS, bsz),
        in_specs=[smem, smem, vec, vec, vec, vec,
                  pl.BlockSpec((None, seq, w), lambda h, b: (b, 0, h)),
                  pl.BlockSpec((None, seq, w), lambda h, b: (b, 0, k_off + h)),
                  pl.BlockSpec((None, seq, w), lambda h, b: (b, 0, v_off + h)),
                  pl.BlockSpec((w, 1), lambda h, b: (0, 0)),
                  _slab_spec(w_conv, n_steps, step_of)],
        out_specs=[pl.BlockSpec((None, seq, w), lambda h, b: (b, 0, h)),
                   _slab_spec(_bf16_like(w_conv), n_steps, step_of)],
        out_shape=[jax.ShapeDtypeStruct((bsz, seq, DIFF_WIDTH), jnp.bfloat16), _bf16_like(w_conv)],
        scratch_shapes=([pltpu.VMEM((blk, blk), jnp.float32),
                         pltpu.VMEM((blk, blk), jnp.float32),
                         pltpu.VMEM((w, seq), jnp.bfloat16)]
                        + [pltpu.VMEM((2, blk, blk), jnp.float32)] * PIPE_BUFS
                        + [pltpu.VMEM((2, 1, blk), jnp.float32)] * PIPE_BUFS
                        + [pltpu.VMEM((2, 1, seq), jnp.float32),
                           pltpu.VMEM((2, 1, seq), jnp.float32),
                           pltpu.VMEM((2, w, seq), jnp.float32)]),
        compiler_params=_params(("arbitrary", "arbitrary")),
        name="diff_attention",
    )(jnp.asarray(pairs), tab, lq1, lk1, lq2, lk2, qkv, qkv, qkv, g_col, w_conv)


def _fox_attn_kernel(kinds, pairs, q_ref, k_ref, v_ref, cum_ref, wa_ref, wb_ref,
                     o_ref, wa_out_ref, wb_out_ref,
                     vt_ref, q2t_ref, k2_ref, *tile_and_state_refs):
    blk = ATTN_BLOCK
    d = HEAD_DIM
    f32, bf16 = jnp.float32, jnp.bfloat16
    key = lax.broadcasted_iota(jnp.int32, (blk, blk), 0)
    qry = lax.broadcasted_iota(jnp.int32, (blk, blk), 1)
    sub = lax.broadcasted_iota(jnp.int32, (LANES, blk), 0)
    u_refs = tile_and_state_refs[:PIPE_BUFS]
    bm_refs = tile_and_state_refs[PIPE_BUFS:2 * PIPE_BUFS]
    m_ref, acc_ref = tile_and_state_refs[2 * PIPE_BUFS:]

    wa_out_ref[...] = wa_ref[...].astype(wa_out_ref.dtype)
    wb_out_ref[...] = wb_ref[...].astype(wb_out_ref.dtype)

    _store_transposed(v_ref, vt_ref.at[:d, :])
    vt_ref[d:, :] = jnp.ones((vt_ref.shape[0] - d, vt_ref.shape[1]), vt_ref.dtype)
    for c in range(0, q_ref.shape[0], blk):
        rows = slice(c, c + blk)
        x = cum_ref[:, rows] * (-1.0 / SCALE)
        hi = x.astype(bf16).astype(f32)
        mid = (x - hi).astype(bf16).astype(f32)
        lo = ((x - hi) - mid).astype(bf16).astype(f32)
        aug_t = jnp.where(sub == 0, hi, jnp.where(sub == 1, mid, jnp.where(sub == 2, lo, 0.0)))
        k2_ref[rows, :d] = k_ref[rows, :]
        k2_ref[rows, d:] = aug_t.astype(bf16).T
    _store_transposed(q_ref, q2t_ref.at[:d, :])
    ones_row = lax.broadcasted_iota(jnp.int32, (d, q2t_ref.shape[1]), 0) < 3
    q2t_ref[d:, :] = jnp.where(ones_row, 1.0, 0.0).astype(bf16)
    m_ref[...] = jnp.full(m_ref.shape, NEG_INF, jnp.float32)
    acc_ref[...] = jnp.zeros(acc_ref.shape, jnp.float32)

    def starts(t):
        return int(pairs[0][t]) * blk, int(pairs[1][t]) * blk

    def stage_a(t, kind, buf):
        qs, ks = starts(t)
        u = jnp.dot(k2_ref[pl.ds(ks, blk), :], q2t_ref[:, pl.ds(qs, blk)],
                    preferred_element_type=jnp.float32) * (SCALE * LOG2E)
        if kind == DIAG:
            u = jnp.where(qry >= key, u, NEG_INF)
        u_refs[buf][...] = u
        bm_refs[buf][...] = jnp.max(u, axis=0, keepdims=True)

    def stage_b(t, buf):
        qs, ks = starts(t)
        cq = cum_ref[:, pl.ds(qs, blk)] * LOG2E
        _softmax_pv_update(u_refs[buf][...], bm_refs[buf][...], cq, vt_ref[:, pl.ds(ks, blk)],
                           m_ref, None, acc_ref, qs)

    _run_pipeline(kinds, stage_a, stage_b, unroll_all=True)

    for c in range(0, o_ref.shape[0], blk):
        cols = slice(c, c + blk)
        o_t = acc_ref[:d, cols] * (1.0 / acc_ref[d:d + 1, cols])
        o_ref[cols, :] = o_t.astype(o_ref.dtype).T


def _fox_attention(qkv, cum, w_conv_a, w_conv_b):
    bsz, seq, _ = qkv.shape
    blk = ATTN_BLOCK
    q_off = 3 * DIFF_WIDTH // HEAD_DIM
    k_off = q_off + N_FOX_HEADS
    v_off = k_off + N_FOX_HEADS
    kinds, pairs = _pair_schedule(seq // blk, with_near=False)
    n_steps = N_FOX_HEADS * bsz

    def step_of(h, b):
        return h * bsz + b

    return pl.pallas_call(
        functools.partial(_fox_attn_kernel, kinds, pairs),
        grid=(N_FOX_HEADS, bsz),
        in_specs=[pl.BlockSpec((None, seq, HEAD_DIM), lambda h, b: (b, 0, q_off + h)),
                  pl.BlockSpec((None, seq, HEAD_DIM), lambda h, b: (b, 0, k_off + h)),
                  pl.BlockSpec((None, seq, HEAD_DIM), lambda h, b: (b, 0, v_off + h)),
                  pl.BlockSpec((None, None, 1, seq), lambda h, b: (h, b, 0, 0)),
                  _slab_spec(w_conv_a, n_steps, step_of),
                  _slab_spec(w_conv_b, n_steps, step_of)],
        out_specs=[pl.BlockSpec((None, seq, HEAD_DIM), lambda h, b: (b, 0, h)),
                   _slab_spec(_bf16_like(w_conv_a), n_steps, step_of),
                   _slab_spec(_bf16_like(w_conv_b), n_steps, step_of)],
        out_shape=[jax.ShapeDtypeStruct((bsz, seq, FOX_WIDTH), jnp.bfloat16),
                   _bf16_like(w_conv_a), _bf16_like(w_conv_b)],
        scratch_shapes=([pltpu.VMEM((HEAD_DIM + FOX_SUM_ROWS, seq), jnp.bfloat16),
                         pltpu.VMEM((2 * HEAD_DIM, seq), jnp.bfloat16),
                         pltpu.VMEM((seq, 2 * HEAD_DIM), jnp.bfloat16)]
                        + [pltpu.VMEM((blk, blk), jnp.float32)] * PIPE_BUFS
                        + [pltpu.VMEM((1, blk), jnp.float32)] * PIPE_BUFS
                        + [pltpu.VMEM((1, seq), jnp.float32),
                           pltpu.VMEM((HEAD_DIM + FOX_SUM_ROWS, seq), jnp.float32)]),
        compiler_params=_params(("arbitrary", "arbitrary")),
        name="fox_attention",
    )(qkv, qkv, qkv, cum, w_conv_a, w_conv_b)


def _out_proj_kernel(ad_ref, af_ref, w_ref, res_ref, g_ref, x1_ref, xg_ref, ssq_ref):
    j = pl.program_id(1)
    kd = ad_ref.shape[1]
    wd = w_ref[:kd, :]
    wf = w_ref[kd:, :]
    parts = []
    for rows in _row_chunks(ad_ref.shape[0]):
        y = (res_ref[rows, :] + jnp.dot(ad_ref[rows, :], wd, preferred_element_type=jnp.float32)
             + jnp.dot(af_ref[rows, :], wf, preferred_element_type=jnp.float32))
        x1_ref[rows, :] = y
        xg_ref[rows, :] = (y * g_ref[...]).astype(xg_ref.dtype)
        sq = y * y
        part = sq[:, :LANES]
        for lc in range(LANES, sq.shape[1], LANES):
            part = part + sq[:, lc:lc + LANES]
        parts.append(part)
    part = jnp.concatenate(parts, axis=0)

    @pl.when(j == 0)
    def _first():
        ssq_ref[...] = part

    @pl.when(j > 0)
    def _rest():
        ssq_ref[...] += part


def _out_proj(a_diff, a_fox, w, res, g, bm=1024, bn=512):
    m, kd = a_diff.shape
    kf = a_fox.shape[1]
    d = w.shape[-1]
    return pl.pallas_call(
        _out_proj_kernel,
        grid=(m // bm, d // bn),
        in_specs=[pl.BlockSpec((bm, kd), lambda i, j: (i, 0)),
                  pl.BlockSpec((bm, kf), lambda i, j: (i, 0)),
                  pl.BlockSpec((kd + kf, bn), lambda i, j: (0, j)),
                  pl.BlockSpec((bm, bn), lambda i, j: (i, j)),
                  pl.BlockSpec((1, bn), lambda i, j: (0, j))],
        out_specs=[pl.BlockSpec((bm, bn), lambda i, j: (i, j)),
                   pl.BlockSpec((bm, bn), lambda i, j: (i, j)),
                   pl.BlockSpec((bm, LANES), lambda i, j: (i, 0))],
        out_shape=[jax.ShapeDtypeStruct((m, d), jnp.float32),
                   jax.ShapeDtypeStruct((m, d), jnp.bfloat16),
                   jax.ShapeDtypeStruct((m, LANES), jnp.float32)],
        compiler_params=_params(("arbitrary", "arbitrary")),
        name="out_proj",
    )(a_diff, a_fox, w, res, g)


def _down_kernel(a_ref, w_ref, res_ref, g_ref, o_ref, *, n_k, k_last, n_res):
    k = pl.program_id(1)
    bk = a_ref.shape[1]
    rc = res_ref.shape[1]

    def accumulate(k_len, first):
        a = a_ref[:, :k_len]
        for c in range(0, o_ref.shape[1], PROJ_COL_CHUNK):
            cols = slice(c, c + PROJ_COL_CHUNK)
            prod = jnp.dot(a, w_ref[:k_len, cols], preferred_element_type=jnp.float32)
            o_ref[:, cols] = prod if first else o_ref[:, cols] + prod

    pl.when(k == 0)(lambda: accumulate(bk, True))
    pl.when((k > 0) & (k < n_k - 1))(lambda: accumulate(bk, False))

    @pl.when(k < n_res)
    def _add_residual_tile():
        cols = pl.ds(pl.multiple_of(k * rc, rc), rc)
        o_ref[:, cols] += res_ref[...]

    @pl.when(k == n_k - 1)
    def _finish():
        accumulate(k_last, False)
        for r in range(0, o_ref.shape[0], PROJ_ROW_CHUNK):
            rows = slice(r, r + PROJ_ROW_CHUNK)
            x = o_ref[rows, :]
            o_ref[rows, :] = x * lax.rsqrt(jnp.mean(x * x, axis=-1, keepdims=True) + EPS) * g_ref[...]


def _down_proj_norm(a, w, res, g, bm=1024, bk=DOWN_K_TILE, rc=DOWN_RES_TILE):
    m, kdim = a.shape
    d = w.shape[1]
    n_k = pl.cdiv(kdim, bk)
    k_last = kdim - (n_k - 1) * bk
    n_res = d // rc
    assert n_res * rc == d and n_res <= n_k - 1
    row_block = pl.BlockSpec((bm, d), lambda i, k: (i, 0))
    return pl.pallas_call(
        functools.partial(_down_kernel, n_k=n_k, k_last=k_last, n_res=n_res),
        grid=(m // bm, n_k),
        in_specs=[pl.BlockSpec((bm, bk), lambda i, k: (i, k)),
                  pl.BlockSpec((bk, d), lambda i, k: (k, 0)),
                  pl.BlockSpec((bm, rc), lambda i, k: (i, jnp.minimum(k, n_res - 1))),
                  pl.BlockSpec((1, d), lambda i, k: (0, 0))],
        out_specs=row_block,
        out_shape=jax.ShapeDtypeStruct((m, d), jnp.float32),
        compiler_params=_params(("arbitrary", "arbitrary")),
        name="ffn_down",
    )(a, w, res, g)


def _gate_up_kernel(xg_ref, ssq_ref, wg_ref, wu_ref, wd_ref, o_ref, wd_out_ref, r_ref):
    @pl.when(pl.program_id(1) == 0)
    def _row_scale():
        mean_sq = jnp.sum(ssq_ref[...], axis=-1, keepdims=True) * (1.0 / xg_ref.shape[1])
        r_ref[...] = jnp.broadcast_to(lax.rsqrt(mean_sq + EPS), r_ref.shape)

    wg = wg_ref[...]
    wu = wu_ref[...]
    for rows in _row_chunks(xg_ref.shape[0]):
        xg = xg_ref[rows, :]
        r = jnp.concatenate([r_ref[rows, :]] * (o_ref.shape[1] // LANES), axis=1)
        gate = jnp.dot(xg, wg, preferred_element_type=jnp.float32) * r
        up = jnp.dot(xg, wu, preferred_element_type=jnp.float32) * r
        o_ref[rows, :] = (gate * jax.nn.sigmoid(gate) * up).astype(o_ref.dtype)
    wd_out_ref[...] = wd_ref[...].astype(wd_out_ref.dtype)


def _gate_up(xg, ssq, wg, wu, wd, bm=ROW_TILE, bn=FF_TILE):
    m, k = xg.shape
    f_rows, d = wd.shape[-2:]
    n = wg.shape[1]
    n_j = n // bn
    n_steps = (m // bm) * n_j

    def step_of(i, j):
        return i * n_j + j

    return pl.pallas_call(
        _gate_up_kernel,
        grid=(m // bm, n_j),
        in_specs=[pl.BlockSpec((bm, k), lambda i, j: (i, 0)),
                  pl.BlockSpec((bm, LANES), lambda i, j: (i, 0)),
                  pl.BlockSpec((k, bn), lambda i, j: (0, j)),
                  pl.BlockSpec((k, bn), lambda i, j: (0, j)),
                  _slab_spec(wd, n_steps, step_of)],
        out_specs=[pl.BlockSpec((bm, bn), lambda i, j: (i, j)),
                   _slab_spec(_bf16_like(wd), n_steps, step_of)],
        out_shape=[jax.ShapeDtypeStruct((m, n), jnp.bfloat16), _bf16_like(wd)],
        scratch_shapes=[pltpu.VMEM((bm, LANES), jnp.float32)],
        compiler_params=_params(("arbitrary", "arbitrary")),
        name="ffn_gate_up",
    )(xg, ssq, wg, wu, wd)


def kernel(x, attn_norm_g, w_in, b_f, lambda_q1, lambda_k1, lambda_q2, lambda_k2, rel_bias_table,
           diff_subln_g, w_o, ffn_norm_g, w_gate, w_up, w_down, final_norm_g):
    bsz, seq, d = x.shape
    m = bsz * seq
    x2d = x.reshape(m, d)

    w_in_t = jnp.swapaxes(w_in, 1, 2)
    h, fl_t = _norm_and_gate_logits(x2d, attn_norm_g[0].reshape(1, d), w_in_t, QKV_COLS, N_FOX_HEADS)
    qkv = _matmul(h, w_in_t, QKV_COLS, name="in_proj").reshape(bsz, seq, QKV_COLS)

    cum = _cum_log_forget(fl_t, b_f[0].reshape(N_FOX_HEADS, 1), seq)
    cum = cum.reshape(N_FOX_HEADS, bsz, 1, seq)

    o_diff, w_o_b = _diff_attention(
        qkv, rel_bias_table,
        lambda_q1[0].reshape(1, HEAD_DIM), lambda_k1[0].reshape(1, HEAD_DIM),
        lambda_q2[0].reshape(1, HEAD_DIM), lambda_k2[0].reshape(1, HEAD_DIM),
        diff_subln_g[0].reshape(2 * HEAD_DIM, 1), w_o)
    o_fox, w_gate_b, w_up_b = _fox_attention(qkv, cum, w_gate, w_up)

    x1, x1g, ssq = _out_proj(o_diff.reshape(m, DIFF_WIDTH), o_fox.reshape(m, FOX_WIDTH), w_o_b, x2d,
                             ffn_norm_g[0].reshape(1, d))
    act, w_down_b = _gate_up(x1g, ssq, w_gate_b, w_up_b, w_down)
    out = _down_proj_norm(act, w_down_b, x1, final_norm_g.reshape(1, d))
    return out.reshape(bsz, seq, d)
```

```python
import functools
import math

import numpy as np

import jax
import jax.numpy as jnp
from jax import lax
from jax.experimental import pallas as pl
from jax.experimental.pallas import tpu as pltpu

D_MODEL = 4096
HEAD_DIM = 128
N_DIFF_HEADS = D_MODEL // (4 * HEAD_DIM)
N_FOX_HEADS = D_MODEL // (2 * HEAD_DIM)
DIFF_WIDTH = N_DIFF_HEADS * 2 * HEAD_DIM
FOX_WIDTH = N_FOX_HEADS * HEAD_DIM
QKV_COLS = 3 * DIFF_WIDTH + 3 * FOX_WIDTH
D_FF = ((8 * D_MODEL + 3 * 256 - 1) // (3 * 256)) * 256
NUM_BUCKETS = 32
MAX_DISTANCE = 128
EPS = 1e-6
NEG_INF = -1e30
LAM_INIT = 0.8 - 0.6 * math.exp(-0.3 * 0)
SCALE = HEAD_DIM ** -0.5
LOG2E = math.log2(math.e)

LANES = 128
BF16_SUBLANES = 16
ROW_TILE = 2048
FF_TILE = 256
DOWN_K_TILE = 512
DOWN_RES_TILE = 256
VMEM_LIMIT = 56 * 1024 * 1024
VMEM_LIMIT_HIGH = 60 * 1024 * 1024

ATTN_BLOCK = 512
PIPE_DEPTH = 3
PIPE_BUFS = PIPE_DEPTH + 1
FOX_SUM_ROWS = BF16_SUBLANES
MATMUL_ROW_CHUNK = 512
PROJ_COL_CHUNK = 1024
PROJ_ROW_CHUNK = 128

FAR, NEAR, DIAG = "far", "near", "diag"


def _params(sem, vmem=VMEM_LIMIT):
    return pltpu.CompilerParams(dimension_semantics=sem, vmem_limit_bytes=vmem)


def _row_chunks(n_rows):
    return [slice(s, s + MATMUL_ROW_CHUNK) for s in range(0, n_rows, MATMUL_ROW_CHUNK)]


def _norm_kernel(x_ref, g_ref, wf_ref, h_ref, fl_ref):
    x = x_ref[...]
    y = x * lax.rsqrt(jnp.mean(x * x, axis=-1, keepdims=True) + EPS)
    h = (y * g_ref[...]).astype(jnp.bfloat16)
    h_ref[...] = h
    fl_ref[...] = lax.dot_general(wf_ref[...].astype(jnp.bfloat16), h, (((1,), (1,)), ((), ())),
                                  preferred_element_type=jnp.float32)


def _norm_and_gate_logits(x2d, g, w_in_t, gate_row, nh, bm=512):
    m, d = x2d.shape
    assert gate_row % nh == 0 and gate_row + nh == w_in_t.shape[1]
    return pl.pallas_call(
        _norm_kernel,
        grid=(m // bm,),
        in_specs=[pl.BlockSpec((bm, d), lambda i: (i, 0)),
                  pl.BlockSpec((1, d), lambda i: (0, 0)),
                  pl.BlockSpec((None, nh, d), lambda i: (0, gate_row // nh, 0))],
        out_specs=[pl.BlockSpec((bm, d), lambda i: (i, 0)),
                   pl.BlockSpec((nh, bm), lambda i: (0, i))],
        out_shape=[jax.ShapeDtypeStruct((m, d), jnp.bfloat16),
                   jax.ShapeDtypeStruct((nh, m), jnp.float32)],
        compiler_params=_params(("arbitrary",)),
        name="attn_norm",
    )(x2d, g, w_in_t)


def _matmul_kernel(a_ref, wt_ref, o_ref):
    w_t = wt_ref[...].astype(a_ref.dtype)
    for rows in _row_chunks(a_ref.shape[0]):
        o_ref[rows, :] = lax.dot_general(a_ref[rows, :], w_t, (((1,), (1,)), ((), ())),
                                         preferred_element_type=jnp.float32).astype(o_ref.dtype)


def _matmul(a, w_t, n_out, bm=ROW_TILE, bn=512, name="matmul"):
    m, k = a.shape
    return pl.pallas_call(
        _matmul_kernel,
        grid=(m // bm, n_out // bn),
        in_specs=[pl.BlockSpec((bm, k), lambda i, j: (i, 0)),
                  pl.BlockSpec((None, bn, k), lambda i, j: (0, j, 0))],
        out_specs=pl.BlockSpec((bm, bn), lambda i, j: (i, j)),
        out_shape=jax.ShapeDtypeStruct((m, n_out), jnp.bfloat16),
        compiler_params=_params(("arbitrary", "arbitrary"), vmem=VMEM_LIMIT_HIGH),
        name=name,
    )(a, w_t)


def _cumsum_kernel(fl_ref, bf_ref, c_ref):
    z = fl_ref[...] + bf_ref[...]
    x = jnp.minimum(z, 0.0) - jnp.log(1.0 + jnp.exp(-jnp.abs(z)))
    n = x.shape[-1]
    pos = lax.broadcasted_iota(jnp.int32, x.shape, 1)
    shift = 1
    while shift < n:
        x = x + jnp.where(pos >= shift, pltpu.roll(x, shift, 1), 0.0)
        shift *= 2
    c_ref[...] = x


def _cum_log_forget(fl_t, b_f, seq):
    nh, m = fl_t.shape
    return pl.pallas_call(
        _cumsum_kernel,
        grid=(m // seq,),
        in_specs=[pl.BlockSpec((nh, seq), lambda b: (0, b)),
                  pl.BlockSpec((nh, 1), lambda b: (0, 0))],
        out_specs=pl.BlockSpec((nh, seq), lambda b: (0, b)),
        out_shape=jax.ShapeDtypeStruct((nh, m), jnp.float32),
        compiler_params=_params(("arbitrary",)),
        name="cum_log_forget",
    )(fl_t, b_f)


def _scores_t(k, q):
    return lax.dot_general(k, q, (((1,), (1,)), ((), ())), preferred_element_type=jnp.float32)


def _softmax_pv_update(u, blk_max, shift, v_t, m_ref, l_ref, acc_ref, qs):
    blk = u.shape[1]
    m_old = m_ref[:, pl.ds(qs, blk)]
    m_new = jnp.maximum(m_old, blk_max + shift)
    alpha = jnp.exp2(m_old - m_new)
    p = jnp.exp2(u + (shift - m_new))
    if l_ref is not None:
        l_ref[:, pl.ds(qs, blk)] = (alpha * l_ref[:, pl.ds(qs, blk)]
                                    + jnp.sum(p, axis=0, keepdims=True))
    acc_ref[:, pl.ds(qs, blk)] = alpha * acc_ref[:, pl.ds(qs, blk)] + jnp.dot(
        v_t, p.astype(v_t.dtype), preferred_element_type=jnp.float32)
    m_ref[:, pl.ds(qs, blk)] = m_new


def _pair_schedule(n_blocks, with_near):
    far, near, diag = [], [], []
    for i in range(n_blocks):
        for j in range(i + 1):
            if j == i:
                diag.append((i, j))
            elif with_near and j == i - 1:
                near.append((i, j))
            else:
                far.append((i, j))
    pairs = far + near + diag
    kinds = [FAR] * len(far) + [NEAR] * len(near) + [DIAG] * len(diag)
    return kinds, np.asarray(pairs, np.int32).T.copy()


def _run_pipeline(kinds, stage_a, stage_b, unroll_all=False):
    n = len(kinds)
    ahead_kind = list(kinds[PIPE_DEPTH:]) + [None] * PIPE_DEPTH
    for t in range(min(PIPE_DEPTH, n)):
        stage_a(t, kinds[t], t % PIPE_BUFS)
    s = 0
    while s < n:
        e = s
        while e < n and ahead_kind[e] == ahead_kind[s]:
            e += 1
        kind, n_loop = ahead_kind[s], 0 if unroll_all else (e - s) // PIPE_BUFS

        def one(step, buf):
            if kind is not None:
                stage_a(step + PIPE_DEPTH, kind, (buf + PIPE_DEPTH) % PIPE_BUFS)
            stage_b(step, buf)

        if n_loop:
            def body(r, carry):
                base = s + PIPE_BUFS * r
                for i in range(PIPE_BUFS):
                    one(base + i, (s + i) % PIPE_BUFS)
                return carry

            lax.fori_loop(0, n_loop, body, 0)
        for step in range(s + n_loop * PIPE_BUFS, e):
            one(step, step % PIPE_BUFS)
        s = e


def _store_transposed(src_ref, dst_ref):
    for c in range(0, src_ref.shape[0], ATTN_BLOCK):
        dst_ref[:, c:c + ATTN_BLOCK] = src_ref[c:c + ATTN_BLOCK, :].T.astype(dst_ref.dtype)


def _t5_bucket(n):
    max_exact = NUM_BUCKETS // 2
    nf = jnp.maximum(n, 1).astype(jnp.float32)
    large = max_exact + (jnp.log(nf / max_exact) / math.log(MAX_DISTANCE / max_exact)
                         * (NUM_BUCKETS - max_exact)).astype(jnp.int32)
    large = jnp.minimum(large, NUM_BUCKETS - 1)
    return jnp.where(n < max_exact, n, large)


def _diff_attn_kernel(kinds, pair_ref, tab_ref, lq1_ref, lk1_ref, lq2_ref, lk2_ref,
                      q_ref, k_ref, v_ref, g_ref, w_ref, o_ref, w_out_ref,
                      bias_near_ref, bias_diag_ref, vt_ref, *tile_and_state_refs):
    h = pl.program_id(0)
    b = pl.program_id(1)
    blk = ATTN_BLOCK
    w_out_ref[...] = w_ref[...].astype(w_out_ref.dtype)
    key = lax.broadcasted_iota(jnp.int32, (blk, blk), 0)
    qry = lax.broadcasted_iota(jnp.int32, (blk, blk), 1)
    u_refs = tile_and_state_refs[:PIPE_BUFS]
    bm_refs = tile_and_state_refs[PIPE_BUFS:2 * PIPE_BUFS]
    m_ref, l_ref, acc_ref = tile_and_state_refs[2 * PIPE_BUFS:]
    far_bias = tab_ref[NUM_BUCKETS - 1, h]

    @pl.when(b == 0)
    def _build_bias():
        for ref, offset in ((bias_diag_ref, 0), (bias_near_ref, blk)):
            bucket = _t5_bucket(jnp.maximum(qry - key + offset, 0))
            bias = jnp.zeros((blk, blk), jnp.float32)
            for t in range(NUM_BUCKETS):
                bias = jnp.where(bucket == t, tab_ref[t, h], bias)
            ref[...] = (bias - far_bias) * LOG2E

    _store_transposed(v_ref, vt_ref)
    m_ref[...] = jnp.full(m_ref.shape, NEG_INF, jnp.float32)
    l_ref[...] = jnp.zeros(l_ref.shape, jnp.float32)
    acc_ref[...] = jnp.zeros(acc_ref.shape, jnp.float32)

    def starts(t):
        return (pl.multiple_of(pair_ref[0, t] * blk, blk), pl.multiple_of(pair_ref[1, t] * blk, blk))

    def stage_a(t, kind, par):
        qs, ks = starts(t)
        for mp in range(2):
            cols = slice(mp * HEAD_DIM, (mp + 1) * HEAD_DIM)
            u = _scores_t(k_ref[pl.ds(ks, blk), cols], q_ref[pl.ds(qs, blk), cols]) * (SCALE * LOG2E)
            if kind == NEAR:
                u = u + bias_near_ref[...]
            elif kind == DIAG:
                u = jnp.where(qry >= key, u + bias_diag_ref[...], NEG_INF)
            u_refs[par][mp] = u
            bm_refs[par][mp] = jnp.max(u, axis=0, keepdims=True)

    def stage_b(t, par):
        qs, ks = starts(t)
        v_t = vt_ref[:, pl.ds(ks, blk)]
        for mp in range(2):
            _softmax_pv_update(u_refs[par][mp], bm_refs[par][mp], far_bias * LOG2E, v_t,
                               m_ref.at[mp], l_ref.at[mp], acc_ref.at[mp], qs)

    _run_pipeline(kinds, stage_a, stage_b)

    lam = (jnp.exp(jnp.sum(lq1_ref[...] * lk1_ref[...], axis=-1, keepdims=True))
           - jnp.exp(jnp.sum(lq2_ref[...] * lk2_ref[...], axis=-1, keepdims=True))
           + LAM_INIT)
    for c in range(0, o_ref.shape[0], blk):
        cols = slice(c, c + blk)
        o_t = (acc_ref[0, :, cols] * (1.0 / l_ref[0, :, cols])
               - lam * (acc_ref[1, :, cols] * (1.0 / l_ref[1, :, cols])))
        y_t = o_t * lax.rsqrt(jnp.mean(o_t * o_t, axis=0, keepdims=True) + EPS)
        y_t = (y_t * g_ref[...]) * (1.0 - LAM_INIT)
        o_ref[cols, :] = y_t.astype(o_ref.dtype).T


def _slab_spec(weight, n_steps, step_of):
    rows, cols = weight.shape[-2:]
    slab = rows // n_steps
    assert slab * n_steps == rows and slab % BF16_SUBLANES == 0
    if weight.ndim == 3:
        return pl.BlockSpec((None, slab, cols), lambda *idx: (0, step_of(*idx), 0))
    return pl.BlockSpec((slab, cols), lambda *idx: (step_of(*idx), 0))


def _bf16_like(weight):
    return jax.ShapeDtypeStruct(weight.shape[-2:], jnp.bfloat16)


def _diff_attention(qkv, tab, lq1, lk1, lq2, lk2, g_col, w_conv):
    bsz, seq, _ = qkv.shape
    blk = ATTN_BLOCK
    w = 2 * HEAD_DIM
    k_off = DIFF_WIDTH // w
    v_off = 2 * DIFF_WIDTH // w
    kinds, pairs = _pair_schedule(seq // blk, with_near=True)
    smem = pl.BlockSpec(memory_space=pltpu.SMEM)
    vec = pl.BlockSpec((1, HEAD_DIM), lambda h, b: (0, 0))
    n_steps = N_DIFF_HEADS * bsz

    def step_of(h, b):
        return h * bsz + b

    return pl.pallas_call(
        functools.partial(_diff_attn_kernel, kinds),
        grid=(N_DIFF_HEADS, bsz),
        in_specs=[smem, smem, vec, vec, vec, vec,
                  pl.BlockSpec((None, seq, w), lambda h, b: (b, 0, h)),
                  pl.BlockSpec((None, seq, w), lambda h, b: (b, 0, k_off + h)),
                  pl.BlockSpec((None, seq, w), lambda h, b: (b, 0, v_off + h)),
                  pl.BlockSpec((w, 1), lambda h, b: (0, 0)),
                  _slab_spec(w_conv, n_steps, step_of)],
        out_specs=[pl.BlockSpec((None, seq, w), lambda h, b: (b, 0, h)),
                   _slab_spec(_bf16_like(w_conv), n_steps, step_of)],
        out_shape=[jax.ShapeDtypeStruct((bsz, seq, DIFF_WIDTH), jnp.bfloat16), _bf16_like(w_conv)],
        scratch_shapes=([pltpu.VMEM((blk, blk), jnp.float32),
                         pltpu.VMEM((blk, blk), jnp.float32),
                         pltpu.VMEM((w, seq), jnp.bfloat16)]
                        + [pltpu.VMEM((2, blk, blk), jnp.float32)] * PIPE_BUFS
                        + [pltpu.VMEM((2, 1, blk), jnp.float32)] * PIPE_BUFS
                        + [pltpu.VMEM((2, 1, seq), jnp.float32),
                           pltpu.VMEM((2, 1, seq), jnp.float32),
                           pltpu.VMEM((2, w, seq), jnp.float32)]),
        compiler_params=_params(("arbitrary", "arbitrary")),
        name="diff_attention",
    )(jnp.asarray(pairs), tab, lq1, lk1, lq2, lk2, qkv, qkv, qkv, g_col, w_conv)


def _fox_attn_kernel(kinds, pairs, q_ref, k_ref, v_ref, cum_ref, wa_ref, wb_ref,
                     o_ref, wa_out_ref, wb_out_ref,
                     vt_ref, q2t_ref, k2_ref, *tile_and_state_refs):
    blk = ATTN_BLOCK
    d = HEAD_DIM
    f32, bf16 = jnp.float32, jnp.bfloat16
    key = lax.broadcasted_iota(jnp.int32, (blk, blk), 0)
    qry = lax.broadcasted_iota(jnp.int32, (blk, blk), 1)
    lane = lax.broadcasted_iota(jnp.int32, (blk, LANES), 1)
    u_refs = tile_and_state_refs[:PIPE_BUFS]
    bm_refs = tile_and_state_refs[PIPE_BUFS:2 * PIPE_BUFS]
    m_ref, acc_ref = tile_and_state_refs[2 * PIPE_BUFS:]

    wa_out_ref[...] = wa_ref[...].astype(wa_out_ref.dtype)
    wb_out_ref[...] = wb_ref[...].astype(wb_out_ref.dtype)

    _store_transposed(v_ref, vt_ref.at[:d, :])
    vt_ref[d:, :] = jnp.ones((vt_ref.shape[0] - d, vt_ref.shape[1]), vt_ref.dtype)
    for c in range(0, q_ref.shape[0], blk):
        rows = slice(c, c + blk)
        x = jnp.broadcast_to(cum_ref[:, rows] * (-1.0 / SCALE), (LANES, blk)).T
        hi = x.astype(bf16).astype(f32)
        mid = (x - hi).astype(bf16).astype(f32)
        lo = ((x - hi) - mid).astype(bf16).astype(f32)
        aug = jnp.where(lane == 0, hi, jnp.where(lane == 1, mid, jnp.where(lane == 2, lo, 0.0)))
        k2_ref[rows, :d] = k_ref[rows, :]
        k2_ref[rows, d:] = aug.astype(bf16)
    _store_transposed(q_ref, q2t_ref.at[:d, :])
    ones_row = lax.broadcasted_iota(jnp.int32, (d, q2t_ref.shape[1]), 0) < 3
    q2t_ref[d:, :] = jnp.where(ones_row, 1.0, 0.0).astype(bf16)
    m_ref[...] = jnp.full(m_ref.shape, NEG_INF, jnp.float32)
    acc_ref[...] = jnp.zeros(acc_ref.shape, jnp.float32)

    def starts(t):
        return int(pairs[0][t]) * blk, int(pairs[1][t]) * blk

    def stage_a(t, kind, buf):
        qs, ks = starts(t)
        u = jnp.dot(k2_ref[pl.ds(ks, blk), :], q2t_ref[:, pl.ds(qs, blk)],
                    preferred_element_type=jnp.float32) * (SCALE * LOG2E)
        if kind == DIAG:
            u = jnp.where(qry >= key, u, NEG_INF)
        u_refs[buf][...] = u
        bm_refs[buf][...] = jnp.max(u, axis=0, keepdims=True)

    def stage_b(t, buf):
        qs, ks = starts(t)
        cq = cum_ref[:, pl.ds(qs, blk)] * LOG2E
        _softmax_pv_update(u_refs[buf][...], bm_refs[buf][...], cq, vt_ref[:, pl.ds(ks, blk)],
                           m_ref, None, acc_ref, qs)

    _run_pipeline(kinds, stage_a, stage_b, unroll_all=True)

    for c in range(0, o_ref.shape[0], blk):
        cols = slice(c, c + blk)
        o_t = acc_ref[:d, cols] * (1.0 / acc_ref[d:d + 1, cols])
        o_ref[cols, :] = o_t.astype(o_ref.dtype).T


def _fox_attention(qkv, cum, w_conv_a, w_conv_b):
    bsz, seq, _ = qkv.shape
    blk = ATTN_BLOCK
    q_off = 3 * DIFF_WIDTH // HEAD_DIM
    k_off = q_off + N_FOX_HEADS
    v_off = k_off + N_FOX_HEADS
    kinds, pairs = _pair_schedule(seq // blk, with_near=False)
    n_steps = N_FOX_HEADS * bsz

    def step_of(h, b):
        return h * bsz + b

    return pl.pallas_call(
        functools.partial(_fox_attn_kernel, kinds, pairs),
        grid=(N_FOX_HEADS, bsz),
        in_specs=[pl.BlockSpec((None, seq, HEAD_DIM), lambda h, b: (b, 0, q_off + h)),
                  pl.BlockSpec((None, seq, HEAD_DIM), lambda h, b: (b, 0, k_off + h)),
                  pl.BlockSpec((None, seq, HEAD_DIM), lambda h, b: (b, 0, v_off + h)),
                  pl.BlockSpec((None, None, 1, seq), lambda h, b: (h, b, 0, 0)),
                  _slab_spec(w_conv_a, n_steps, step_of),
                  _slab_spec(w_conv_b, n_steps, step_of)],
        out_specs=[pl.BlockSpec((None, seq, HEAD_DIM), lambda h, b: (b, 0, h)),
                   _slab_spec(_bf16_like(w_conv_a), n_steps, step_of),
                   _slab_spec(_bf16_like(w_conv_b), n_steps, step_of)],
        out_shape=[jax.ShapeDtypeStruct((bsz, seq, FOX_WIDTH), jnp.bfloat16),
                   _bf16_like(w_conv_a), _bf16_like(w_conv_b)],
        scratch_shapes=([pltpu.VMEM((HEAD_DIM + FOX_SUM_ROWS, seq), jnp.bfloat16),
                         pltpu.VMEM((2 * HEAD_DIM, seq), jnp.bfloat16),
                         pltpu.VMEM((seq, 2 * HEAD_DIM), jnp.bfloat16)]
                        + [pltpu.VMEM((blk, blk), jnp.float32)] * PIPE_BUFS
                        + [pltpu.VMEM((1, blk), jnp.float32)] * PIPE_BUFS
                        + [pltpu.VMEM((1, seq), jnp.float32),
                           pltpu.VMEM((HEAD_DIM + FOX_SUM_ROWS, seq), jnp.float32)]),
        compiler_params=_params(("arbitrary", "arbitrary")),
        name="fox_attention",
    )(qkv, qkv, qkv, cum, w_conv_a, w_conv_b)


def _out_proj_kernel(ad_ref, af_ref, w_ref, res_ref, g_ref, x1_ref, xg_ref, ssq_ref):
    j = pl.program_id(1)
    kd = ad_ref.shape[1]
    wd = w_ref[:kd, :]
    wf = w_ref[kd:, :]
    parts = []
    for rows in _row_chunks(ad_ref.shape[0]):
        y = (res_ref[rows, :] + jnp.dot(ad_ref[rows, :], wd, preferred_element_type=jnp.float32)
             + jnp.dot(af_ref[rows, :], wf, preferred_element_type=jnp.float32))
        x1_ref[rows, :] = y
        xg_ref[rows, :] = (y * g_ref[...]).astype(xg_ref.dtype)
        sq = y * y
        part = sq[:, :LANES]
        for lc in range(LANES, sq.shape[1], LANES):
            part = part + sq[:, lc:lc + LANES]
        parts.append(part)
    part = jnp.concatenate(parts, axis=0)

    @pl.when(j == 0)
    def _first():
        ssq_ref[...] = part

    @pl.when(j > 0)
    def _rest():
        ssq_ref[...] += part


def _out_proj(a_diff, a_fox, w, res, g, bm=1024, bn=512):
    m, kd = a_diff.shape
    kf = a_fox.shape[1]
    d = w.shape[-1]
    return pl.pallas_call(
        _out_proj_kernel,
        grid=(m // bm, d // bn),
        in_specs=[pl.BlockSpec((bm, kd), lambda i, j: (i, 0)),
                  pl.BlockSpec((bm, kf), lambda i, j: (i, 0)),
                  pl.BlockSpec((kd + kf, bn), lambda i, j: (0, j)),
                  pl.BlockSpec((bm, bn), lambda i, j: (i, j)),
                  pl.BlockSpec((1, bn), lambda i, j: (0, j))],
        out_specs=[pl.BlockSpec((bm, bn), lambda i, j: (i, j)),
                   pl.BlockSpec((bm, bn), lambda i, j: (i, j)),
                   pl.BlockSpec((bm, LANES), lambda i, j: (i, 0))],
        out_shape=[jax.ShapeDtypeStruct((m, d), jnp.float32),
                   jax.ShapeDtypeStruct((m, d), jnp.bfloat16),
                   jax.ShapeDtypeStruct((m, LANES), jnp.float32)],
        compiler_params=_params(("arbitrary", "arbitrary")),
        name="out_proj",
    )(a_diff, a_fox, w, res, g)


def _down_kernel(a_ref, w_ref, res_ref, g_ref, o_ref, *, n_k, k_last, n_res):
    k = pl.program_id(1)
    bk = a_ref.shape[1]
    rc = res_ref.shape[1]

    def accumulate(k_len, first):
        a = a_ref[:, :k_len]
        for c in range(0, o_ref.shape[1], PROJ_COL_CHUNK):
            cols = slice(c, c + PROJ_COL_CHUNK)
            prod = jnp.dot(a, w_ref[:k_len, cols], preferred_element_type=jnp.float32)
            o_ref[:, cols] = prod if first else o_ref[:, cols] + prod

    pl.when(k == 0)(lambda: accumulate(bk, True))
    pl.when((k > 0) & (k < n_k - 1))(lambda: accumulate(bk, False))

    @pl.when(k < n_res)
    def _add_residual_tile():
        cols = pl.ds(pl.multiple_of(k * rc, rc), rc)
        o_ref[:, cols] += res_ref[...]

    @pl.when(k == n_k - 1)
    def _finish():
        accumulate(k_last, False)
        for r in range(0, o_ref.shape[0], PROJ_ROW_CHUNK):
            rows = slice(r, r + PROJ_ROW_CHUNK)
            x = o_ref[rows, :]
            o_ref[rows, :] = x * lax.rsqrt(jnp.mean(x * x, axis=-1, keepdims=True) + EPS) * g_ref[...]


def _down_proj_norm(a, w, res, g, bm=1024, bk=DOWN_K_TILE, rc=DOWN_RES_TILE):
    m, kdim = a.shape
    d = w.shape[1]
    n_k = pl.cdiv(kdim, bk)
    k_last = kdim - (n_k - 1) * bk
    n_res = d // rc
    assert n_res * rc == d and n_res <= n_k - 1
    row_block = pl.BlockSpec((bm, d), lambda i, k: (i, 0))
    return pl.pallas_call(
        functools.partial(_down_kernel, n_k=n_k, k_last=k_last, n_res=n_res),
        grid=(m // bm, n_k),
        in_specs=[pl.BlockSpec((bm, bk), lambda i, k: (i, k)),
                  pl.BlockSpec((bk, d), lambda i, k: (k, 0)),
                  pl.BlockSpec((bm, rc), lambda i, k: (i, jnp.minimum(k, n_res - 1))),
                  pl.BlockSpec((1, d), lambda i, k: (0, 0))],
        out_specs=row_block,
        out_shape=jax.ShapeDtypeStruct((m, d), jnp.float32),
        compiler_params=_params(("arbitrary", "arbitrary")),
        name="ffn_down",
    )(a, w, res, g)


def _gate_up_kernel(xg_ref, ssq_ref, wg_ref, wu_ref, wd_ref, o_ref, wd_out_ref, r_ref):
    @pl.when(pl.program_id(1) == 0)
    def _row_scale():
        mean_sq = jnp.sum(ssq_ref[...], axis=-1, keepdims=True) * (1.0 / xg_ref.shape[1])
        r_ref[...] = jnp.broadcast_to(lax.rsqrt(mean_sq + EPS), r_ref.shape)

    wg = wg_ref[...]
    wu = wu_ref[...]
    for rows in _row_chunks(xg_ref.shape[0]):
        xg = xg_ref[rows, :]
        r = jnp.concatenate([r_ref[rows, :]] * (o_ref.shape[1] // LANES), axis=1)
        gate = jnp.dot(xg, wg, preferred_element_type=jnp.float32) * r
        up = jnp.dot(xg, wu, preferred_element_type=jnp.float32) * r
        o_ref[rows, :] = (gate * jax.nn.sigmoid(gate) * up).astype(o_ref.dtype)
    wd_out_ref[...] = wd_ref[...].astype(wd_out_ref.dtype)


def _gate_up(xg, ssq, wg, wu, wd, bm=ROW_TILE, bn=FF_TILE):
    m, k = xg.shape
    f_rows, d = wd.shape[-2:]
    n = wg.shape[1]
    n_j = n // bn
    n_steps = (m // bm) * n_j

    def step_of(i, j):
        return i * n_j + j

    return pl.pallas_call(
        _gate_up_kernel,
        grid=(m // bm, n_j),
        in_specs=[pl.BlockSpec((bm, k), lambda i, j: (i, 0)),
                  pl.BlockSpec((bm, LANES), lambda i, j: (i, 0)),
                  pl.BlockSpec((k, bn), lambda i, j: (0, j)),
                  pl.BlockSpec((k, bn), lambda i, j: (0, j)),
                  _slab_spec(wd, n_steps, step_of)],
        out_specs=[pl.BlockSpec((bm, bn), lambda i, j: (i, j)),
                   _slab_spec(_bf16_like(wd), n_steps, step_of)],
        out_shape=[jax.ShapeDtypeStruct((m, n), jnp.bfloat16), _bf16_like(wd)],
        scratch_shapes=[pltpu.VMEM((bm, LANES), jnp.float32)],
        compiler_params=_params(("arbitrary", "arbitrary")),
        name="ffn_gate_up",
    )(xg, ssq, wg, wu, wd)


def kernel(x, attn_norm_g, w_in, b_f, lambda_q1, lambda_k1, lambda_q2, lambda_k2, rel_bias_table,
           diff_subln_g, w_o, ffn_norm_g, w_gate, w_up, w_down, final_norm_g):
    bsz, seq, d = x.shape
    m = bsz * seq
    x2d = x.reshape(m, d)

    w_in_t = jnp.swapaxes(w_in, 1, 2)
    h, fl_t = _norm_and_gate_logits(x2d, attn_norm_g[0].reshape(1, d), w_in_t, QKV_COLS, N_FOX_HEADS)
    qkv = _matmul(h, w_in_t, QKV_COLS, name="in_proj").reshape(bsz, seq, QKV_COLS)

    cum = _cum_log_forget(fl_t, b_f[0].reshape(N_FOX_HEADS, 1), seq)
    cum = cum.reshape(N_FOX_HEADS, bsz, 1, seq)

    o_diff, w_o_b = _diff_attention(
        qkv, rel_bias_table,
        lambda_q1[0].reshape(1, HEAD_DIM), lambda_k1[0].reshape(1, HEAD_DIM),
        lambda_q2[0].reshape(1, HEAD_DIM), lambda_k2[0].reshape(1, HEAD_DIM),
        diff_subln_g[0].reshape(2 * HEAD_DIM, 1), w_o)
    o_fox, w_gate_b, w_up_b = _fox_attention(qkv, cum, w_gate, w_up)

    x1, x1g, ssq = _out_proj(o_diff.reshape(m, DIFF_WIDTH), o_fox.reshape(m, FOX_WIDTH), w_o_b, x2d,
                             ffn_norm_g[0].reshape(1, d))
    act, w_down_b = _gate_up(x1g, ssq, w_gate_b, w_up_b, w_down)
    out = _down_proj_norm(act, w_down_b, x1, final_norm_g.reshape(1, d))
    return out.reshape(bsz, seq, d)
```

```python
import functools
import math

import numpy as np

import jax
import jax.numpy as jnp
from jax import lax
from jax.experimental import pallas as pl
from jax.experimental.pallas import tpu as pltpu

D_MODEL = 4096
HEAD_DIM = 128
N_DIFF_HEADS = D_MODEL // (4 * HEAD_DIM)
N_FOX_HEADS = D_MODEL // (2 * HEAD_DIM)
DIFF_WIDTH = N_DIFF_HEADS * 2 * HEAD_DIM
FOX_WIDTH = N_FOX_HEADS * HEAD_DIM
QKV_COLS = 3 * DIFF_WIDTH + 3 * FOX_WIDTH
D_FF = ((8 * D_MODEL + 3 * 256 - 1) // (3 * 256)) * 256
NUM_BUCKETS = 32
MAX_DISTANCE = 128
EPS = 1e-6
NEG_INF = -1e30
LAM_INIT = 0.8 - 0.6 * math.exp(-0.3 * 0)
SCALE = HEAD_DIM ** -0.5
LOG2E = math.log2(math.e)

LANES = 128
BF16_SUBLANES = 16
ROW_TILE = 2048
FF_TILE = 256
DOWN_K_TILE = 512
DOWN_RES_TILE = 256
VMEM_LIMIT = 56 * 1024 * 1024
VMEM_LIMIT_HIGH = 60 * 1024 * 1024

ATTN_BLOCK = 512
PIPE_DEPTH = 3
PIPE_BUFS = PIPE_DEPTH + 1
FOX_SUM_ROWS = BF16_SUBLANES
MATMUL_ROW_CHUNK = 512
PROJ_COL_CHUNK = 1024
PROJ_ROW_CHUNK = 128

FAR, NEAR, DIAG = "far", "near", "diag"


def _params(sem, vmem=VMEM_LIMIT):
    return pltpu.CompilerParams(dimension_semantics=sem, vmem_limit_bytes=vmem)


def _row_chunks(n_rows):
    return [slice(s, s + MATMUL_ROW_CHUNK) for s in range(0, n_rows, MATMUL_ROW_CHUNK)]


def _norm_kernel(x_ref, g_ref, wf_ref, h_ref, fl_ref):
    x = x_ref[...]
    y = x * lax.rsqrt(jnp.mean(x * x, axis=-1, keepdims=True) + EPS)
    h = (y * g_ref[...]).astype(jnp.bfloat16)
    h_ref[...] = h
    fl_ref[...] = lax.dot_general(wf_ref[...].astype(jnp.bfloat16), h, (((1,), (1,)), ((), ())),
                                  preferred_element_type=jnp.float32)


def _norm_and_gate_logits(x2d, g, w_in_t, gate_row, nh, bm=512):
    m, d = x2d.shape
    assert gate_row % nh == 0 and gate_row + nh == w_in_t.shape[1]
    return pl.pallas_call(
        _norm_kernel,
        grid=(m // bm,),
        in_specs=[pl.BlockSpec((bm, d), lambda i: (i, 0)),
                  pl.BlockSpec((1, d), lambda i: (0, 0)),
                  pl.BlockSpec((None, nh, d), lambda i: (0, gate_row // nh, 0))],
        out_specs=[pl.BlockSpec((bm, d), lambda i: (i, 0)),
                   pl.BlockSpec((nh, bm), lambda i: (0, i))],
        out_shape=[jax.ShapeDtypeStruct((m, d), jnp.bfloat16),
                   jax.ShapeDtypeStruct((nh, m), jnp.float32)],
        compiler_params=_params(("arbitrary",)),
        name="attn_norm",
    )(x2d, g, w_in_t)


def _matmul_kernel(a_ref, wt_ref, o_ref):
    w_t = wt_ref[...].astype(a_ref.dtype)
    for rows in _row_chunks(a_ref.shape[0]):
        o_ref[rows, :] = lax.dot_general(a_ref[rows, :], w_t, (((1,), (1,)), ((), ())),
                                         preferred_element_type=jnp.float32).astype(o_ref.dtype)


def _matmul(a, w_t, n_out, bm=ROW_TILE, bn=512, name="matmul"):
    m, k = a.shape
    return pl.pallas_call(
        _matmul_kernel,
        grid=(m // bm, n_out // bn),
        in_specs=[pl.BlockSpec((bm, k), lambda i, j: (i, 0)),
                  pl.BlockSpec((None, bn, k), lambda i, j: (0, j, 0))],
        out_specs=pl.BlockSpec((bm, bn), lambda i, j: (i, j)),
        out_shape=jax.ShapeDtypeStruct((m, n_out), jnp.bfloat16),
        compiler_params=_params(("arbitrary", "arbitrary"), vmem=VMEM_LIMIT_HIGH),
        name=name,
    )(a, w_t)


def _cumsum_kernel(fl_ref, bf_ref, c_ref):
    z = fl_ref[...] + bf_ref[...]
    x = jnp.minimum(z, 0.0) - jnp.log(1.0 + jnp.exp(-jnp.abs(z)))
    n = x.shape[-1]
    pos = lax.broadcasted_iota(jnp.int32, x.shape, 1)
    shift = 1
    while shift < n:
        x = x + jnp.where(pos >= shift, pltpu.roll(x, shift, 1), 0.0)
        shift *= 2
    c_ref[...] = x


def _cum_log_forget(fl_t, b_f, seq):
    nh, m = fl_t.shape
    return pl.pallas_call(
        _cumsum_kernel,
        grid=(m // seq,),
        in_specs=[pl.BlockSpec((nh, seq), lambda b: (0, b)),
                  pl.BlockSpec((nh, 1), lambda b: (0, 0))],
        out_specs=pl.BlockSpec((nh, seq), lambda b: (0, b)),
        out_shape=jax.ShapeDtypeStruct((nh, m), jnp.float32),
        compiler_params=_params(("arbitrary",)),
        name="cum_log_forget",
    )(fl_t, b_f)


def _scores_t(k, q):
    return lax.dot_general(k, q, (((1,), (1,)), ((), ())), preferred_element_type=jnp.float32)


def _softmax_pv_update(u, blk_max, shift, v_t, m_ref, l_ref, acc_ref, qs):
    blk = u.shape[1]
    m_old = m_ref[:, pl.ds(qs, blk)]
    m_new = jnp.maximum(m_old, blk_max + shift)
    alpha = jnp.exp2(m_old - m_new)
    p = jnp.exp2(u + (shift - m_new))
    if l_ref is not None:
        l_ref[:, pl.ds(qs, blk)] = (alpha * l_ref[:, pl.ds(qs, blk)]
                                    + jnp.sum(p, axis=0, keepdims=True))
    acc_ref[:, pl.ds(qs, blk)] = alpha * acc_ref[:, pl.ds(qs, blk)] + jnp.dot(
        v_t, p.astype(v_t.dtype), preferred_element_type=jnp.float32)
    m_ref[:, pl.ds(qs, blk)] = m_new


def _pair_schedule(n_blocks, with_near):
    far, near, diag = [], [], []
    for i in range(n_blocks):
        for j in range(i + 1):
            if j == i:
                diag.append((i, j))
            elif with_near and j == i - 1:
                near.append((i, j))
            else:
                far.append((i, j))
    pairs = far + near + diag
    kinds = [FAR] * len(far) + [NEAR] * len(near) + [DIAG] * len(diag)
    return kinds, np.asarray(pairs, np.int32).T.copy()


def _run_pipeline(kinds, stage_a, stage_b, unroll_all=False):
    n = len(kinds)
    ahead_kind = list(kinds[PIPE_DEPTH:]) + [None] * PIPE_DEPTH
    for t in range(min(PIPE_DEPTH, n)):
        stage_a(t, kinds[t], t % PIPE_BUFS)
    s = 0
    while s < n:
        e = s
        while e < n and ahead_kind[e] == ahead_kind[s]:
            e += 1
        kind, n_loop = ahead_kind[s], 0 if unroll_all else (e - s) // PIPE_BUFS

        def one(step, buf):
            if kind is not None:
                stage_a(step + PIPE_DEPTH, kind, (buf + PIPE_DEPTH) % PIPE_BUFS)
            stage_b(step, buf)

        if n_loop:
            def body(r, carry):
                base = s + PIPE_BUFS * r
                for i in range(PIPE_BUFS):
                    one(base + i, (s + i) % PIPE_BUFS)
                return carry

            lax.fori_loop(0, n_loop, body, 0)
        for step in range(s + n_loop * PIPE_BUFS, e):
            one(step, step % PIPE_BUFS)
        s = e


def _store_transposed(src_ref, dst_ref):
    for c in range(0, src_ref.shape[0], ATTN_BLOCK):
        dst_ref[:, c:c + ATTN_BLOCK] = src_ref[c:c + ATTN_BLOCK, :].T.astype(dst_ref.dtype)


def _t5_bucket(n):
    max_exact = NUM_BUCKETS // 2
    nf = jnp.maximum(n, 1).astype(jnp.float32)
    large = max_exact + (jnp.log(nf / max_exact) / math.log(MAX_DISTANCE / max_exact)
                         * (NUM_BUCKETS - max_exact)).astype(jnp.int32)
    large = jnp.minimum(large, NUM_BUCKETS - 1)
    return jnp.where(n < max_exact, n, large)


def _diff_attn_kernel(kinds, pair_ref, tab_ref, lq1_ref, lk1_ref, lq2_ref, lk2_ref,
                      q_ref, k_ref, v_ref, g_ref, w_ref, o_ref, w_out_ref,
                      bias_near_ref, bias_diag_ref, vt_ref, *tile_and_state_refs):
    h = pl.program_id(0)
    b = pl.program_id(1)
    blk = ATTN_BLOCK
    w_out_ref[...] = w_ref[...].astype(w_out_ref.dtype)
    key = lax.broadcasted_iota(jnp.int32, (blk, blk), 0)
    qry = lax.broadcasted_iota(jnp.int32, (blk, blk), 1)
    u_refs = tile_and_state_refs[:PIPE_BUFS]
    bm_refs = tile_and_state_refs[PIPE_BUFS:2 * PIPE_BUFS]
    m_ref, l_ref, acc_ref = tile_and_state_refs[2 * PIPE_BUFS:]
    far_bias = tab_ref[NUM_BUCKETS - 1, h]

    @pl.when(b == 0)
    def _build_bias():
        for ref, offset in ((bias_diag_ref, 0), (bias_near_ref, blk)):
            bucket = _t5_bucket(jnp.maximum(qry - key + offset, 0))
            bias = jnp.zeros((blk, blk), jnp.float32)
            for t in range(NUM_BUCKETS):
                bias = jnp.where(bucket == t, tab_ref[t, h], bias)
            ref[...] = (bias - far_bias) * LOG2E

    _store_transposed(v_ref, vt_ref)
    m_ref[...] = jnp.full(m_ref.shape, NEG_INF, jnp.float32)
    l_ref[...] = jnp.zeros(l_ref.shape, jnp.float32)
    acc_ref[...] = jnp.zeros(acc_ref.shape, jnp.float32)

    def starts(t):
        return (pl.multiple_of(pair_ref[0, t] * blk, blk), pl.multiple_of(pair_ref[1, t] * blk, blk))

    def stage_a(t, kind, par):
        qs, ks = starts(t)
        for mp in range(2):
            cols = slice(mp * HEAD_DIM, (mp + 1) * HEAD_DIM)
            u = _scores_t(k_ref[pl.ds(ks, blk), cols], q_ref[pl.ds(qs, blk), cols]) * (SCALE * LOG2E)
            if kind == NEAR:
                u = u + bias_near_ref[...]
            elif kind == DIAG:
                u = jnp.where(qry >= key, u + bias_diag_ref[...], NEG_INF)
            u_refs[par][mp] = u
            bm_refs[par][mp] = jnp.max(u, axis=0, keepdims=True)

    def stage_b(t, par):
        qs, ks = starts(t)
        v_t = vt_ref[:, pl.ds(ks, blk)]
        for mp in range(2):
            _softmax_pv_update(u_refs[par][mp], bm_refs[par][mp], far_bias * LOG2E, v_t,
                               m_ref.at[mp], l_ref.at[mp], acc_ref.at[mp], qs)

    _run_pipeline(kinds, stage_a, stage_b)

    lam = (jnp.exp(jnp.sum(lq1_ref[...] * lk1_ref[...], axis=-1, keepdims=True))
           - jnp.exp(jnp.sum(lq2_ref[...] * lk2_ref[...], axis=-1, keepdims=True))
           + LAM_INIT)
    for c in range(0, o_ref.shape[0], blk):
        cols = slice(c, c + blk)
        o_t = (acc_ref[0, :, cols] * (1.0 / l_ref[0, :, cols])
               - lam * (acc_ref[1, :, cols] * (1.0 / l_ref[1, :, cols])))
        y_t = o_t * lax.rsqrt(jnp.mean(o_t * o_t, axis=0, keepdims=True) + EPS)
        y_t = (y_t * g_ref[...]) * (1.0 - LAM_INIT)
        o_ref[cols, :] = y_t.astype(o_ref.dtype).T


def _slab_spec(weight, n_steps, step_of):
    rows, cols = weight.shape[-2:]
    slab = rows // n_steps
    assert slab * n_steps == rows and slab % BF16_SUBLANES == 0
    if weight.ndim == 3:
        return pl.BlockSpec((None, slab, cols), lambda *idx: (0, step_of(*idx), 0))
    return pl.BlockSpec((slab, cols), lambda *idx: (step_of(*idx), 0))


def _bf16_like(weight):
    return jax.ShapeDtypeStruct(weight.shape[-2:], jnp.bfloat16)


def _diff_attention(qkv, tab, lq1, lk1, lq2, lk2, g_col, w_conv):
    bsz, seq, _ = qkv.shape
    blk = ATTN_BLOCK
    w = 2 * HEAD_DIM
    k_off = DIFF_WIDTH // w
    v_off = 2 * DIFF_WIDTH // w
    kinds, pairs = _pair_schedule(seq // blk, with_near=True)
    smem = pl.BlockSpec(memory_space=pltpu.SMEM)
    vec = pl.BlockSpec((1, HEAD_DIM), lambda h, b: (0, 0))
    n_steps = N_DIFF_HEADS * bsz

    def step_of(h, b):
        return h * bsz + b

    return pl.pallas_call(
        functools.partial(_diff_attn_kernel, kinds),
        grid=(N_DIFF_HEADS, bsz),
        in_specs=[smem, smem, vec, vec, vec, vec,
                  pl.BlockSpec((None, seq, w), lambda h, b: (b, 0, h)),
                  pl.BlockSpec((None, seq, w), lambda h, b: (b, 0, k_off + h)),
                  pl.BlockSpec((None, seq, w), lambda h, b: (b, 0, v_off + h)),
                  pl.BlockSpec((w, 1), lambda h, b: (0, 0)),
                  _slab_spec(w_conv, n_steps, step_of)],
        out_specs=[pl.BlockSpec((None, seq, w), lambda h, b: (b, 0, h)),
                   _slab_spec(_bf16_like(w_conv), n_steps, step_of)],
        out_shape=[jax.ShapeDtypeStruct((bsz, seq, DIFF_WIDTH), jnp.bfloat16), _bf16_like(w_conv)],
        scratch_shapes=([pltpu.VMEM((blk, blk), jnp.float32),
                         pltpu.VMEM((blk, blk), jnp.float32),
                         pltpu.VMEM((w, seq), jnp.bfloat16)]
                        + [pltpu.VMEM((2, blk, blk), jnp.float32)] * PIPE_BUFS
                        + [pltpu.VMEM((2, 1, blk), jnp.float32)] * PIPE_BUFS
                        + [pltpu.VMEM((2, 1, seq), jnp.float32),
                           pltpu.VMEM((2, 1, seq), jnp.float32),
                           pltpu.VMEM((2, w, seq), jnp.float32)]),
        compiler_params=_params(("arbitrary", "arbitrary")),
        name="diff_attention",
    )(jnp.asarray(pairs), tab, lq1, lk1, lq2, lk2, qkv, qkv, qkv, g_col, w_conv)


def _fox_attn_kernel(kinds, pairs, q_ref, k_ref, v_ref, cum_ref, wa_ref, wb_ref,
                     o_ref, wa_out_ref, wb_out_ref,
                     vt_ref, q2t_ref, k2_ref, *tile_and_state_refs):
    blk = ATTN_BLOCK
    d = HEAD_DIM
    f32, bf16 = jnp.float32, jnp.bfloat16
    half = blk // 2
    _iota = functools.partial(lax.broadcasted_iota, jnp.int32)
    lane = _iota((blk, LANES), 1)
    u_refs = tile_and_state_refs[:PIPE_BUFS]
    bm_refs = tile_and_state_refs[PIPE_BUFS:2 * PIPE_BUFS]
    m_ref, acc_ref = tile_and_state_refs[2 * PIPE_BUFS:]

    wa_out_ref[...] = wa_ref[...].astype(wa_out_ref.dtype)
    wb_out_ref[...] = wb_ref[...].astype(wb_out_ref.dtype)

    _store_transposed(v_ref, vt_ref.at[:d, :])
    vt_ref[d:, :] = jnp.ones((vt_ref.shape[0] - d, vt_ref.shape[1]), vt_ref.dtype)
    for c in range(0, q_ref.shape[0], blk):
        rows = slice(c, c + blk)
        x = jnp.broadcast_to(cum_ref[:, rows] * (-1.0 / SCALE), (LANES, blk)).T
        hi = x.astype(bf16).astype(f32)
        mid = (x - hi).astype(bf16).astype(f32)
        lo = ((x - hi) - mid).astype(bf16).astype(f32)
        aug = jnp.where(lane == 0, hi, jnp.where(lane == 1, mid, jnp.where(lane == 2, lo, 0.0)))
        k2_ref[rows, :d] = k_ref[rows, :]
        k2_ref[rows, d:] = aug.astype(bf16)
    _store_transposed(q_ref, q2t_ref.at[:d, :])
    ones_row = lax.broadcasted_iota(jnp.int32, (d, q2t_ref.shape[1]), 0) < 3
    q2t_ref[d:, :] = jnp.where(ones_row, 1.0, 0.0).astype(bf16)
    m_ref[...] = jnp.full(m_ref.shape, NEG_INF, jnp.float32)
    acc_ref[...] = jnp.zeros(acc_ref.shape, jnp.float32)

    def starts(t):
        return int(pairs[0][t]) * blk, int(pairs[1][t]) * blk

    def scores(ks, n_keys, qs, n_q):
        return jnp.dot(k2_ref[ks:ks + n_keys, :], q2t_ref[:, qs:qs + n_q],
                       preferred_element_type=jnp.float32) * (SCALE * LOG2E)

    def stage_a(t, kind, buf):
        qs, ks = starts(t)
        if kind == DIAG:
            top = jnp.where(_iota((half, blk), 1) >= _iota((half, blk), 0),
                            scores(ks, half, qs, blk), NEG_INF)
            bot = jnp.where(_iota((half, half), 1) >= _iota((half, half), 0),
                            scores(ks + half, half, qs + half, half), NEG_INF)
            u_refs[buf][:half, :] = top
            u_refs[buf][half:, half:] = bot
            top_max = jnp.max(top, axis=0, keepdims=True)
            bm_refs[buf][:, :half] = top_max[:, :half]
            bm_refs[buf][:, half:] = jnp.maximum(top_max[:, half:],
                                                 jnp.max(bot, axis=0, keepdims=True))
        else:
            u = scores(ks, blk, qs, blk)
            u_refs[buf][...] = u
            bm_refs[buf][...] = jnp.max(u, axis=0, keepdims=True)

    def update(buf, keys, cols, qs, ks):
        q0 = qs + cols.start
        cq = cum_ref[:, q0:q0 + cols.stop - cols.start] * LOG2E
        _softmax_pv_update(u_refs[buf][:keys, cols], bm_refs[buf][:, cols], cq,
                           vt_ref[:, ks:ks + keys], m_ref, None, acc_ref, q0)

    def stage_b(t, buf):
        qs, ks = starts(t)
        if kinds[t] == DIAG:
            update(buf, half, slice(0, half), qs, ks)
            update(buf, blk, slice(half, blk), qs, ks)
        else:
            update(buf, blk, slice(0, blk), qs, ks)

    _run_pipeline(kinds, stage_a, stage_b, unroll_all=True)

    for c in range(0, o_ref.shape[0], blk):
        cols = slice(c, c + blk)
        o_t = acc_ref[:d, cols] * (1.0 / acc_ref[d:d + 1, cols])
        o_ref[cols, :] = o_t.astype(o_ref.dtype).T


def _fox_attention(qkv, cum, w_conv_a, w_conv_b):
    bsz, seq, _ = qkv.shape
    blk = ATTN_BLOCK
    q_off = 3 * DIFF_WIDTH // HEAD_DIM
    k_off = q_off + N_FOX_HEADS
    v_off = k_off + N_FOX_HEADS
    kinds, pairs = _pair_schedule(seq // blk, with_near=False)
    n_steps = N_FOX_HEADS * bsz

    def step_of(h, b):
        return h * bsz + b

    return pl.pallas_call(
        functools.partial(_fox_attn_kernel, kinds, pairs),
        grid=(N_FOX_HEADS, bsz),
        in_specs=[pl.BlockSpec((None, seq, HEAD_DIM), lambda h, b: (b, 0, q_off + h)),
                  pl.BlockSpec((None, seq, HEAD_DIM), lambda h, b: (b, 0, k_off + h)),
                  pl.BlockSpec((None, seq, HEAD_DIM), lambda h, b: (b, 0, v_off + h)),
                  pl.BlockSpec((None, None, 1, seq), lambda h, b: (h, b, 0, 0)),
                  _slab_spec(w_conv_a, n_steps, step_of),
                  _slab_spec(w_conv_b, n_steps, step_of)],
        out_specs=[pl.BlockSpec((None, seq, HEAD_DIM), lambda h, b: (b, 0, h)),
                   _slab_spec(_bf16_like(w_conv_a), n_steps, step_of),
                   _slab_spec(_bf16_like(w_conv_b), n_steps, step_of)],
        out_shape=[jax.ShapeDtypeStruct((bsz, seq, FOX_WIDTH), jnp.bfloat16),
                   _bf16_like(w_conv_a), _bf16_like(w_conv_b)],
        scratch_shapes=([pltpu.VMEM((HEAD_DIM + FOX_SUM_ROWS, seq), jnp.bfloat16),
                         pltpu.VMEM((2 * HEAD_DIM, seq), jnp.bfloat16),
                         pltpu.VMEM((seq, 2 * HEAD_DIM), jnp.bfloat16)]
                        + [pltpu.VMEM((blk, blk), jnp.float32)] * PIPE_BUFS
                        + [pltpu.VMEM((1, blk), jnp.float32)] * PIPE_BUFS
                        + [pltpu.VMEM((1, seq), jnp.float32),
                           pltpu.VMEM((HEAD_DIM + FOX_SUM_ROWS, seq), jnp.float32)]),
        compiler_params=_params(("arbitrary", "arbitrary")),
        name="fox_attention",
    )(qkv, qkv, qkv, cum, w_conv_a, w_conv_b)


def _out_proj_kernel(ad_ref, af_ref, w_ref, res_ref, g_ref, x1_ref, xg_ref, ssq_ref):
    j = pl.program_id(1)
    kd = ad_ref.shape[1]
    wd = w_ref[:kd, :]
    wf = w_ref[kd:, :]
    parts = []
    for rows in _row_chunks(ad_ref.shape[0]):
        y = (res_ref[rows, :] + jnp.dot(ad_ref[rows, :], wd, preferred_element_type=jnp.float32)
             + jnp.dot(af_ref[rows, :], wf, preferred_element_type=jnp.float32))
        x1_ref[rows, :] = y
        xg_ref[rows, :] = (y * g_ref[...]).astype(xg_ref.dtype)
        sq = y * y
        part = sq[:, :LANES]
        for lc in range(LANES, sq.shape[1], LANES):
            part = part + sq[:, lc:lc + LANES]
        parts.append(part)
    part = jnp.concatenate(parts, axis=0)

    @pl.when(j == 0)
    def _first():
        ssq_ref[...] = part

    @pl.when(j > 0)
    def _rest():
        ssq_ref[...] += part


def _out_proj(a_diff, a_fox, w, res, g, bm=1024, bn=512):
    m, kd = a_diff.shape
    kf = a_fox.shape[1]
    d = w.shape[-1]
    return pl.pallas_call(
        _out_proj_kernel,
        grid=(m // bm, d // bn),
        in_specs=[pl.BlockSpec((bm, kd), lambda i, j: (i, 0)),
                  pl.BlockSpec((bm, kf), lambda i, j: (i, 0)),
                  pl.BlockSpec((kd + kf, bn), lambda i, j: (0, j)),
                  pl.BlockSpec((bm, bn), lambda i, j: (i, j)),
                  pl.BlockSpec((1, bn), lambda i, j: (0, j))],
        out_specs=[pl.BlockSpec((bm, bn), lambda i, j: (i, j)),
                   pl.BlockSpec((bm, bn), lambda i, j: (i, j)),
                   pl.BlockSpec((bm, LANES), lambda i, j: (i, 0))],
        out_shape=[jax.ShapeDtypeStruct((m, d), jnp.float32),
                   jax.ShapeDtypeStruct((m, d), jnp.bfloat16),
                   jax.ShapeDtypeStruct((m, LANES), jnp.float32)],
        compiler_params=_params(("arbitrary", "arbitrary")),
        name="out_proj",
    )(a_diff, a_fox, w, res, g)


def _down_kernel(a_ref, w_ref, res_ref, g_ref, o_ref, *, n_k, k_last, n_res):
    k = pl.program_id(1)
    bk = a_ref.shape[1]
    rc = res_ref.shape[1]

    def accumulate(k_len, first):
        a = a_ref[:, :k_len]
        for c in range(0, o_ref.shape[1], PROJ_COL_CHUNK):
            cols = slice(c, c + PROJ_COL_CHUNK)
            prod = jnp.dot(a, w_ref[:k_len, cols], preferred_element_type=jnp.float32)
            o_ref[:, cols] = prod if first else o_ref[:, cols] + prod

    pl.when(k == 0)(lambda: accumulate(bk, True))
    pl.when((k > 0) & (k < n_k - 1))(lambda: accumulate(bk, False))

    @pl.when(k < n_res)
    def _add_residual_tile():
        cols = pl.ds(pl.multiple_of(k * rc, rc), rc)
        o_ref[:, cols] += res_ref[...]

    @pl.when(k == n_k - 1)
    def _finish():
        accumulate(k_last, False)
        for r in range(0, o_ref.shape[0], PROJ_ROW_CHUNK):
            rows = slice(r, r + PROJ_ROW_CHUNK)
            x = o_ref[rows, :]
            o_ref[rows, :] = x * lax.rsqrt(jnp.mean(x * x, axis=-1, keepdims=True) + EPS) * g_ref[...]


def _down_proj_norm(a, w, res, g, bm=1024, bk=DOWN_K_TILE, rc=DOWN_RES_TILE):
    m, kdim = a.shape
    d = w.shape[1]
    n_k = pl.cdiv(kdim, bk)
    k_last = kdim - (n_k - 1) * bk
    n_res = d // rc
    assert n_res * rc == d and n_res <= n_k - 1
    row_block = pl.BlockSpec((bm, d), lambda i, k: (i, 0))
    return pl.pallas_call(
        functools.partial(_down_kernel, n_k=n_k, k_last=k_last, n_res=n_res),
        grid=(m // bm, n_k),
        in_specs=[pl.BlockSpec((bm, bk), lambda i, k: (i, k)),
                  pl.BlockSpec((bk, d), lambda i, k: (k, 0)),
                  pl.BlockSpec((bm, rc), lambda i, k: (i, jnp.minimum(k, n_res - 1))),
                  pl.BlockSpec((1, d), lambda i, k: (0, 0))],
        out_specs=row_block,
        out_shape=jax.ShapeDtypeStruct((m, d), jnp.float32),
        compiler_params=_params(("arbitrary", "arbitrary")),
        name="ffn_down",
    )(a, w, res, g)


def _gate_up_kernel(xg_ref, ssq_ref, wg_ref, wu_ref, wd_ref, o_ref, wd_out_ref, r_ref):
    @pl.when(pl.program_id(1) == 0)
    def _row_scale():
        mean_sq = jnp.sum(ssq_ref[...], axis=-1, keepdims=True) * (1.0 / xg_ref.shape[1])
        r_ref[...] = jnp.broadcast_to(lax.rsqrt(mean_sq + EPS), r_ref.shape)

    wg = wg_ref[...]
    wu = wu_ref[...]
    for rows in _row_chunks(xg_ref.shape[0]):
        xg = xg_ref[rows, :]
        r = jnp.concatenate([r_ref[rows, :]] * (o_ref.shape[1] // LANES), axis=1)
        gate = jnp.dot(xg, wg, preferred_element_type=jnp.float32) * r
        up = jnp.dot(xg, wu, preferred_element_type=jnp.float32) * r
        o_ref[rows, :] = (gate * jax.nn.sigmoid(gate) * up).astype(o_ref.dtype)
    wd_out_ref[...] = wd_ref[...].astype(wd_out_ref.dtype)


def _gate_up(xg, ssq, wg, wu, wd, bm=ROW_TILE, bn=FF_TILE):
    m, k = xg.shape
    f_rows, d = wd.shape[-2:]
    n = wg.shape[1]
    n_j = n // bn
    n_steps = (m // bm) * n_j

    def step_of(i, j):
        return i * n_j + j

    return pl.pallas_call(
        _gate_up_kernel,
        grid=(m // bm, n_j),
        in_specs=[pl.BlockSpec((bm, k), lambda i, j: (i, 0)),
                  pl.BlockSpec((bm, LANES), lambda i, j: (i, 0)),
                  pl.BlockSpec((k, bn), lambda i, j: (0, j)),
                  pl.BlockSpec((k, bn), lambda i, j: (0, j)),
                  _slab_spec(wd, n_steps, step_of)],
        out_specs=[pl.BlockSpec((bm, bn), lambda i, j: (i, j)),
                   _slab_spec(_bf16_like(wd), n_steps, step_of)],
        out_shape=[jax.ShapeDtypeStruct((m, n), jnp.bfloat16), _bf16_like(wd)],
        scratch_shapes=[pltpu.VMEM((bm, LANES), jnp.float32)],
        compiler_params=_params(("arbitrary", "arbitrary")),
        name="ffn_gate_up",
    )(xg, ssq, wg, wu, wd)


def kernel(x, attn_norm_g, w_in, b_f, lambda_q1, lambda_k1, lambda_q2, lambda_k2, rel_bias_table,
           diff_subln_g, w_o, ffn_norm_g, w_gate, w_up, w_down, final_norm_g):
    bsz, seq, d = x.shape
    m = bsz * seq
    x2d = x.reshape(m, d)

    w_in_t = jnp.swapaxes(w_in, 1, 2)
    h, fl_t = _norm_and_gate_logits(x2d, attn_norm_g[0].reshape(1, d), w_in_t, QKV_COLS, N_FOX_HEADS)
    qkv = _matmul(h, w_in_t, QKV_COLS, name="in_proj").reshape(bsz, seq, QKV_COLS)

    cum = _cum_log_forget(fl_t, b_f[0].reshape(N_FOX_HEADS, 1), seq)
    cum = cum.reshape(N_FOX_HEADS, bsz, 1, seq)

    o_diff, w_o_b = _diff_attention(
        qkv, rel_bias_table,
        lambda_q1[0].reshape(1, HEAD_DIM), lambda_k1[0].reshape(1, HEAD_DIM),
        lambda_q2[0].reshape(1, HEAD_DIM), lambda_k2[0].reshape(1, HEAD_DIM),
        diff_subln_g[0].reshape(2 * HEAD_DIM, 1), w_o)
    o_fox, w_gate_b, w_up_b = _fox_attention(qkv, cum, w_gate, w_up)

    x1, x1g, ssq = _out_proj(o_diff.reshape(m, DIFF_WIDTH), o_fox.reshape(m, FOX_WIDTH), w_o_b, x2d,
                             ffn_norm_g[0].reshape(1, d))
    act, w_down_b = _gate_up(x1g, ssq, w_gate_b, w_up_b, w_down)
    out = _down_proj_norm(act, w_down_b, x1, final_norm_g.reshape(1, d))
    return out.reshape(bsz, seq, d)
```
